```python
import math
import jax, jax.numpy as jnp
from jax import lax
import numpy as np

D_MODEL = 1024
BATCH = 16
SEQ = 4096
DEPTH = 4

MIX_WIDTH = D_MODEL
GMLP_GROUPS = 4
GMLP_GROUP_DIM = 128
GMLP_CHUNK = 128
GMLP_WIDTH = GMLP_GROUPS * GMLP_GROUP_DIM
GDN_HEADS = 4
GDN_HEAD_DIM = 128
GDN_WIDTH = GDN_HEADS * GDN_HEAD_DIM
GDN_CHUNK = 64
GDN_CONV = 4
DIFF_HEADS = 4
DIFF_QK_DIM = 64
DIFF_V_DIM = 2 * DIFF_QK_DIM
DIFF_WIDTH = DIFF_HEADS * DIFF_V_DIM
DSA_HEADS = 4
DSA_HEAD_DIM = 128
DSA_WIDTH = DSA_HEADS * DSA_HEAD_DIM
IDX_HEADS = 8
IDX_DIM = 64
IDX_TOPK_MAX = 256
Q_BLOCK = 128
REL_BUCKETS = 32
REL_MAX_DIST = 128
N_BIAS_HEADS = DIFF_HEADS + DSA_HEADS
D_FF = 2816
FFN_CONV = 3
EPS = 1e-6

EVEN_SIZES = (GMLP_WIDTH, GMLP_WIDTH, 3 * GDN_WIDTH, GDN_HEADS, GDN_HEADS, GDN_WIDTH)
ODD_SIZES = (DIFF_HEADS * 2 * DIFF_QK_DIM, DIFF_HEADS * 2 * DIFF_QK_DIM, DIFF_WIDTH,
             DSA_WIDTH, DSA_HEAD_DIM, DSA_HEAD_DIM, IDX_HEADS * IDX_DIM, IDX_DIM, IDX_HEADS)
EVEN_IN = sum(EVEN_SIZES)
ODD_IN = sum(ODD_SIZES)
N_EVEN = (DEPTH + 1) // 2
N_ODD = DEPTH // 2

kernel_name = 'hybrid_gmlp_gdn_diff_dsa_trunk'


def split_cols(t, sizes):
    return jnp.split(t, [int(s) for s in np.cumsum(sizes)[:-1]], axis=-1)


def rms_norm(x, g=None):
    xf = x.astype(jnp.float32)
    y = xf * lax.rsqrt(jnp.mean(xf * xf, axis=-1, keepdims=True) + EPS)
    if g is not None:
        y = y * g.astype(jnp.float32)
    return y.astype(x.dtype)


def layer_norm_plain(x):
    xf = x.astype(jnp.float32)
    xc = xf - jnp.mean(xf, axis=-1, keepdims=True)
    return (xc * lax.rsqrt(jnp.mean(xc * xc, axis=-1, keepdims=True) + EPS)).astype(x.dtype)


def l2_normalize(x):
    return x * lax.rsqrt(jnp.sum(x * x, axis=-1, keepdims=True) + EPS)


def causal_depthwise_conv(x, w):
    width, L = w.shape[0], x.shape[1]
    xp = jnp.pad(x, ((0, 0), (width - 1, 0), (0, 0)))
    out = xp[:, 0:L] * w[0]
    for j in range(1, width):
        out = out + xp[:, j:j + L] * w[j]
    return out


def rel_bucket(dist):
    exact = REL_BUCKETS // 2
    n = jnp.maximum(dist, 0)
    nf = jnp.maximum(n, exact).astype(jnp.float32)
    far = exact + (jnp.log(nf / exact) / math.log(REL_MAX_DIST / exact) * (REL_BUCKETS - exact)).astype(jnp.int32)
    return jnp.where(n < exact, n, jnp.minimum(far, REL_BUCKETS - 1))


def chunked_gmlp(u, v, w_s, b_s):
    B, L, _ = u.shape
    n = L // GMLP_CHUNK
    shp = (B, n, GMLP_CHUNK, GMLP_GROUPS, GMLP_GROUP_DIM)
    v = layer_norm_plain(v.reshape(shp))
    causal = jnp.tril(jnp.ones((GMLP_CHUNK, GMLP_CHUNK), dtype=bool))
    w = jnp.where(causal[None], w_s, 0.0)
    mixed = jnp.einsum('gts,bnsgc->bntgc', w, v) + b_s.T[None, None, :, :, None]
    return (u.reshape(shp) * mixed).reshape(B, L, GMLP_WIDTH)


def gated_delta_rule(q, k, v, beta, g):
    B, L, H, dk = q.shape
    dv = v.shape[-1]
    C = GDN_CHUNK
    n = L // C
    q = l2_normalize(q) * (dk ** -0.5)
    k = l2_normalize(k)

    def chunks(t):
        return jnp.moveaxis(t.reshape((B, n, C, H) + t.shape[3:]), 3, 1)

    q, k, v, beta, g = chunks(q), chunks(k), chunks(v), chunks(beta), chunks(g)
    g = jnp.cumsum(g, axis=-1)
    k_beta = k * beta[..., None]
    v_beta = v * beta[..., None]
    incl = jnp.tril(jnp.ones((C, C), dtype=bool))
    strict = jnp.tril(jnp.ones((C, C), dtype=bool), -1)
    decay = jnp.exp(jnp.where(incl, g[..., :, None] - g[..., None, :], -jnp.inf))
    lower = jnp.where(strict, jnp.einsum('bhnid,bhnjd->bhnij', k_beta, k) * decay, 0.0)
    eye = jnp.eye(C, dtype=q.dtype)
    t_inv = lax.linalg.triangular_solve(lower + eye, jnp.broadcast_to(eye, lower.shape),
                                        left_side=True, lower=True, unit_diagonal=True)
    u = t_inv @ v_beta
    w = t_inv @ (k_beta * jnp.exp(g)[..., None])
    intra = jnp.where(incl, jnp.einsum('bhnid,bhnjd->bhnij', q, k) * decay, 0.0)

    def step(state, inp):
        q_c, k_c, u_c, w_c, g_c, a_c = inp
        v_new = u_c - w_c @ state
        out = (q_c * jnp.exp(g_c)[..., None]) @ state + a_c @ v_new
        g_last = g_c[..., -1:]
        state = state * jnp.exp(g_last)[..., None] + jnp.einsum(
            'bhck,bhcv->bhkv', k_c * jnp.exp(g_last - g_c)[..., None], v_new)
        return state, out

    xs = tuple(jnp.moveaxis(t, 2, 0) for t in (q, k, u, w, g, intra))
    state0 = jnp.zeros((B, H, dk, dv), q.dtype)
    _, out = lax.scan(step, state0, xs)
    return jnp.transpose(out, (1, 0, 3, 2, 4)).reshape(B, L, H, dv)


def even_mixer(h, w_in, w_out, w_s, b_s, conv_w, a_log, dt_bias, out_norm_g):
    B, L, _ = h.shape
    u, v, qkv, b_raw, a_raw, z = split_cols(h @ w_in, EVEN_SIZES)
    y_a = chunked_gmlp(jax.nn.gelu(u), jax.nn.gelu(v), w_s, b_s)
    qkv = jax.nn.silu(causal_depthwise_conv(qkv, conv_w))
    q, k, vv = [t.reshape(B, L, GDN_HEADS, GDN_HEAD_DIM).astype(jnp.float32) for t in jnp.split(qkv, 3, axis=-1)]
    beta = jax.nn.sigmoid(b_raw.astype(jnp.float32))
    g = -jnp.exp(a_log.astype(jnp.float32)) * jax.nn.softplus(a_raw.astype(jnp.float32) + dt_bias.astype(jnp.float32))
    o = gated_delta_rule(q, k, vv, beta, g).astype(h.dtype)
    o = rms_norm(o, out_norm_g) * jax.nn.silu(z.reshape(B, L, GDN_HEADS, GDN_HEAD_DIM))
    y = jnp.concatenate([y_a, o.reshape(B, L, GDN_WIDTH)], axis=-1)
    return y @ w_out


def diff_attention(q, k, v, lam, bias_tab, sub_g, lambda_init):
    B, L, H, _, d = q.shape
    nb = L // Q_BLOCK
    k_pos = jnp.arange(L, dtype=jnp.int32)
    q_blocks = jnp.moveaxis(q.reshape(B, nb, Q_BLOCK, H, 2, d), 1, 0)

    def one_block(args):
        q_blk, blk = args
        q_pos = blk * Q_BLOCK + jnp.arange(Q_BLOCK, dtype=jnp.int32)
        dist = q_pos[:, None] - k_pos[None, :]
        bias = jnp.moveaxis(bias_tab[rel_bucket(dist)], -1, 0).astype(jnp.float32)
        logits = jnp.einsum('bqhmd,bkhmd->bhmqk', q_blk, k).astype(jnp.float32) * (d ** -0.5) + bias[None, :, None]
        logits = jnp.where(dist >= 0, logits, -jnp.inf)
        probs = jax.nn.softmax(logits, axis=-1)
        weights = probs[:, :, 0] - lam * probs[:, :, 1]
        return jnp.einsum('bhqk,bkhe->bqhe', weights.astype(v.dtype), v)

    out = lax.map(one_block, (q_blocks, jnp.arange(nb, dtype=jnp.int32)))
    out = jnp.moveaxis(out, 0, 1).reshape(B, L, H, v.shape[-1])
    return rms_norm(out, sub_g) * (1.0 - lambda_init)


def dsa_attention(q, k, v, q_idx, k_idx, w_idx, bias_tab):
    B, L, H, d = q.shape
    nb = L // Q_BLOCK
    top_k = min(IDX_TOPK_MAX, L // 4)
    k_pos = jnp.arange(L, dtype=jnp.int32)
    gather = jax.vmap(lambda t, i: t[i])

    def blocks(t):
        return jnp.moveaxis(t.reshape((B, nb, Q_BLOCK) + t.shape[2:]), 1, 0)

    def one_block(args):
        q_blk, qi_blk, wi_blk, blk = args
        q_pos = blk * Q_BLOCK + jnp.arange(Q_BLOCK, dtype=jnp.int32)
        s = jnp.einsum('bqhd,bkd->bqhk', qi_blk, k_idx).astype(jnp.float32) * (IDX_DIM ** -0.5)
        index = jnp.einsum('bqh,bqhk->bqk', wi_blk.astype(jnp.float32), jax.nn.relu(s))
        index = jnp.where(k_pos[None, :] <= q_pos[:, None], index, -jnp.inf)
        _, sel = lax.top_k(index, top_k)
        valid = sel <= q_pos[None, :, None]
        k_sel = gather(k, sel)
        v_sel = gather(v, sel)
        bias = jnp.moveaxis(bias_tab[rel_bucket(q_pos[None, :, None] - sel)], -1, 1).astype(jnp.float32)
        logits = jnp.einsum('bqhd,bqkd->bhqk', q_blk, k_sel).astype(jnp.float32) * (d ** -0.5) + bias
        logits = jnp.where(valid[:, None], logits, -jnp.inf)
        probs = jax.nn.softmax(logits, axis=-1)
        return jnp.einsum('bhqk,bqkd->bqhd', probs.astype(v.dtype), v_sel)

    out = lax.map(one_block, (blocks(q), blocks(q_idx), blocks(w_idx), jnp.arange(nb, dtype=jnp.int32)))
    return jnp.moveaxis(out, 0, 1).reshape(B, L, H * d)


def odd_mixer(h, w_in, w_out, diff_q_g, diff_k_g, diff_lam, diff_sub_g, dsa_q_g, dsa_k_g, rel_bias, lambda_init):
    B, L, _ = h.shape
    dq, dk, dv, sq, sk, sv, iq, ik, iw = split_cols(h @ w_in, ODD_SIZES)
    dq = rms_norm(dq.reshape(B, L, DIFF_HEADS, 2, DIFF_QK_DIM), diff_q_g)
    dk = rms_norm(dk.reshape(B, L, DIFF_HEADS, 2, DIFF_QK_DIM), diff_k_g)
    dv = dv.reshape(B, L, DIFF_HEADS, DIFF_V_DIM)
    lf = diff_lam.astype(jnp.float32)
    lam = jnp.exp(jnp.sum(lf[0] * lf[1])) - jnp.exp(jnp.sum(lf[2] * lf[3])) + lambda_init
    y_c = diff_attention(dq, dk, dv, lam, rel_bias[:, :DIFF_HEADS], diff_sub_g, lambda_init).reshape(B, L, DIFF_WIDTH)
    sq = rms_norm(sq.reshape(B, L, DSA_HEADS, DSA_HEAD_DIM), dsa_q_g)
    sk = rms_norm(sk, dsa_k_g)
    iq = iq.reshape(B, L, IDX_HEADS, IDX_DIM)
    iw = iw * (IDX_HEADS ** -0.5)
    y_d = dsa_attention(sq, sk, sv, iq, ik, iw, rel_bias[:, DIFF_HEADS:])
    y = jnp.concatenate([y_c, y_d], axis=-1)
    return y @ w_out


def conv_ffn(h, w_up, conv_w, conv_b, w_down):
    up = causal_depthwise_conv(h @ w_up, conv_w) + conv_b
    gate, val = jnp.split(up, 2, axis=-1)
    return (jax.nn.silu(gate) * val) @ w_down


def setup_inputs(seed: int = 0) -> dict:
    key = jax.random.key(seed)
    ks = iter(jax.random.split(key, 32))

    def nrm(shape, scale):
        return jax.random.normal(next(ks), shape, jnp.float32) * scale

    def gain(shape):
        return 1.0 + nrm(shape, 0.02)

    out_scale = (2 * DEPTH) ** -0.5
    dt = jnp.exp(jax.random.uniform(next(ks), (N_EVEN, GDN_HEADS), jnp.float32, math.log(1e-3), math.log(1e-1)))
    a_init = jax.random.uniform(next(ks), (N_EVEN, GDN_HEADS), jnp.float32, 1.0, 16.0)
    return {
        'x': nrm((BATCH, SEQ, D_MODEL), 1.0),
        'rel_bias': nrm((REL_BUCKETS, N_BIAS_HEADS), 0.5),
        'mix_norm_g': gain((DEPTH, D_MODEL)),
        'ev_w_in': nrm((N_EVEN, D_MODEL, EVEN_IN), D_MODEL ** -0.5),
        'ev_w_out': nrm((N_EVEN, MIX_WIDTH, D_MODEL), MIX_WIDTH ** -0.5 * out_scale),
        'gmlp_w_s': nrm((N_EVEN, GMLP_GROUPS, GMLP_CHUNK, GMLP_CHUNK), GMLP_CHUNK ** -0.5),
        'gmlp_b_s': 1.0 + nrm((N_EVEN, GMLP_GROUPS, GMLP_CHUNK), 0.1),
        'gdn_conv_w': nrm((N_EVEN, GDN_CONV, 3 * GDN_WIDTH), GDN_CONV ** -0.5),
        'gdn_a_log': jnp.log(a_init),
        'gdn_dt_bias': dt + jnp.log(-jnp.expm1(-dt)),
        'gdn_norm_g': gain((N_EVEN, GDN_HEAD_DIM)),
        'od_w_in': nrm((N_ODD, D_MODEL, ODD_IN), D_MODEL ** -0.5),
        'od_w_out': nrm((N_ODD, MIX_WIDTH, D_MODEL), MIX_WIDTH ** -0.5 * out_scale),
        'diff_q_norm_g': gain((N_ODD, DIFF_QK_DIM)),
        'diff_k_norm_g': gain((N_ODD, DIFF_QK_DIM)),
        'diff_lambda': nrm((N_ODD, 4, DIFF_QK_DIM), 0.1),
        'diff_sub_norm_g': gain((N_ODD, DIFF_V_DIM)),
        'dsa_q_norm_g': gain((N_ODD, DSA_HEAD_DIM)),
        'dsa_k_norm_g': gain((N_ODD, DSA_HEAD_DIM)),
        'ffn_norm_g': gain((DEPTH, D_MODEL)),
        'ffn_w_up': nrm((DEPTH, D_MODEL, 2 * D_FF), D_MODEL ** -0.5),
        'ffn_conv_w': nrm((DEPTH, FFN_CONV, 2 * D_FF), FFN_CONV ** -0.5),
        'ffn_conv_b': nrm((DEPTH, 2 * D_FF), 0.01),
        'ffn_w_down': nrm((DEPTH, D_FF, D_MODEL), D_FF ** -0.5 * out_scale),
    }


def reference(x, rel_bias, mix_norm_g, ev_w_in, ev_w_out, gmlp_w_s, gmlp_b_s, gdn_conv_w, gdn_a_log,
              gdn_dt_bias, gdn_norm_g, od_w_in, od_w_out, diff_q_norm_g, diff_k_norm_g, diff_lambda,
              diff_sub_norm_g, dsa_q_norm_g, dsa_k_norm_g, ffn_norm_g, ffn_w_up, ffn_conv_w, ffn_conv_b,
              ffn_w_down):
    h = x
    for layer in range(DEPTH):
        j = layer // 2
        hn = rms_norm(h, mix_norm_g[layer])
        if layer % 2 == 0:
            h = h + even_mixer(hn, ev_w_in[j], ev_w_out[j], gmlp_w_s[j], gmlp_b_s[j], gdn_conv_w[j],
                               gdn_a_log[j], gdn_dt_bias[j], gdn_norm_g[j])
        else:
            lambda_init = 0.8 - 0.6 * math.exp(-0.3 * layer)
            h = h + odd_mixer(hn, od_w_in[j], od_w_out[j], diff_q_norm_g[j], diff_k_norm_g[j], diff_lambda[j],
                              diff_sub_norm_g[j], dsa_q_norm_g[j], dsa_k_norm_g[j], rel_bias, lambda_init)
        h = h + conv_ffn(rms_norm(h, ffn_norm_g[layer]), ffn_w_up[layer], ffn_conv_w[layer],
                         ffn_conv_b[layer], ffn_w_down[layer])
    return h
```

```python
import functools
import math

import numpy as np
import jax
import jax.numpy as jnp
from jax import lax
from jax.experimental import pallas as pl
from jax.experimental.pallas import tpu as pltpu

F32 = jnp.float32
BF16 = jnp.bfloat16
I32 = jnp.int32
HIGHEST = lax.Precision.HIGHEST

D_MODEL = 1024
GMLP_GROUPS = 4
GMLP_CHUNK = 128
GDN_HEADS = 4
GDN_DIM = 128
GDN_CHUNK = 64
GDN_CONV = 4
DIFF_HEADS = 4
DIFF_QK = 64
DSA_HEADS = 4
DSA_DIM = 128
IDX_HEADS = 8
IDX_DIM = 64
IDX_TOPK_MAX = 256
REL_BUCKETS = 32
REL_MAX_DIST = 128
D_FF = 2816
FFN_CONV = 3
EPS = 1e-6

LANES = 128
SUBLANES_F32 = 8
SUBLANES_BF16 = 16
VMEM_LIMIT = 56 * 1024 * 1024

TM_PROJ = 512
TL_PREP = 512
TL_GDN = 512
TQ = 256
TK = 256
FF_CHUNK = 256
NEG = -1e30
INT_MIN = -2 ** 31


def _cparams(n_axes):
    return pltpu.CompilerParams(dimension_semantics=("arbitrary",) * n_axes, vmem_limit_bytes=VMEM_LIMIT)


def _dot(a, b, **kw):
    return jnp.dot(a, b, preferred_element_type=F32, **kw)


def _dot_nt(a, b, **kw):
    return lax.dot_general(a, b, (((1,), (1,)), ((), ())), preferred_element_type=F32, **kw)


def _dot_tn(a, b, **kw):
    return lax.dot_general(a, b, (((0,), (0,)), ((), ())), preferred_element_type=F32, **kw)


def _rms(x, g):
    return x * lax.rsqrt(jnp.mean(x * x, axis=-1, keepdims=True) + EPS) * g


def _sigmoid(x):
    return 1.0 / (1.0 + jnp.exp(-x))


def _silu(x):
    return x * _sigmoid(x)


def _gelu_tanh(x):
    return 0.5 * x * (1.0 + jnp.tanh(math.sqrt(2.0 / math.pi) * (x + 0.044715 * (x * x * x))))


def _softplus(x):
    return jnp.maximum(x, 0.0) + jnp.log(1.0 + jnp.exp(-jnp.abs(x)))


def _in_proj_body(h_ref, g_ref, w_ref, wg_ref, o_ref, og_ref, *, nout):
    xn = _rms(h_ref[...], g_ref[...]).astype(BF16)
    for c in range(0, nout, 512):
        e = min(c + 512, nout)
        o_ref[:, c:e] = _dot(xn, w_ref[:, c:e]).astype(BF16)
    og_ref[...] = _dot(xn, wg_ref[...])


def _in_proj(h, g, w, wg):
    n, d = h.shape
    nout = w.shape[1]
    return pl.pallas_call(
        functools.partial(_in_proj_body, nout=nout),
        grid=(n // TM_PROJ,),
        in_specs=[
            pl.BlockSpec((TM_PROJ, d), lambda i: (i, 0)),
            pl.BlockSpec((1, d), lambda i: (0, 0)),
            pl.BlockSpec((d, nout), lambda i: (0, 0)),
            pl.BlockSpec((d, LANES), lambda i: (0, 0)),
        ],
        out_specs=[
            pl.BlockSpec((TM_PROJ, nout), lambda i: (i, 0)),
            pl.BlockSpec((TM_PROJ, LANES), lambda i: (i, 0)),
        ],
        out_shape=[jax.ShapeDtypeStruct((n, nout), BF16), jax.ShapeDtypeStruct((n, LANES), F32)],
        compiler_params=_cparams(1),
        name="in_proj",
    )(h, g, w, wg)


def _out_proj_body(h_ref, y1_ref, y2_ref, w_ref, o_ref):
    half = y1_ref.shape[1]
    o_ref[...] = h_ref[...] + _dot(y1_ref[...], w_ref[:half, :]) + _dot(y2_ref[...], w_ref[half:, :])


def _out_proj(h, y1, y2, w):
    n, d = h.shape
    half = y1.shape[1]
    return pl.pallas_call(
        _out_proj_body,
        grid=(n // TM_PROJ,),
        in_specs=[
            pl.BlockSpec((TM_PROJ, d), lambda i: (i, 0)),
            pl.BlockSpec((TM_PROJ, half), lambda i: (i, 0)),
            pl.BlockSpec((TM_PROJ, half), lambda i: (i, 0)),
            pl.BlockSpec((2 * half, d), lambda i: (0, 0)),
        ],
        out_specs=pl.BlockSpec((TM_PROJ, d), lambda i: (i, 0)),
        out_shape=jax.ShapeDtypeStruct((n, d), F32),
        compiler_params=_cparams(1),
        name="out_proj",
    )(h, y1, y2, w)


def _ffn_body(h_ref, halo_ref, g_ref, wup_ref, cw_ref, cb_ref, wdn_ref, o_ref, xn_ref, *, tiles_per_seq):
    tm = h_ref.shape[0]
    i = pl.program_id(0)
    h = h_ref[...]
    g = g_ref[...]
    hb = SUBLANES_BF16
    xn_ref[hb:, :] = _rms(h, g).astype(BF16)
    keep = jnp.where(i % tiles_per_seq == 0, 0.0, 1.0)
    xn_ref[:hb, :] = (_rms(halo_ref[...], g) * keep).astype(BF16)
    xn = xn_ref[...]
    acc = h
    for c in range(0, D_FF, FF_CHUNK):
        parts = []
        for base in (c, D_FF + c):
            up = _dot(xn, wup_ref[:, base:base + FF_CHUNK])
            cw = cw_ref[:, base:base + FF_CHUNK]
            y = (cw[0:1] * up[hb - 2:tm + hb - 2]
                 + cw[1:2] * up[hb - 1:tm + hb - 1]
                 + cw[2:3] * up[hb:tm + hb]
                 + cb_ref[:, base:base + FF_CHUNK])
            parts.append(y)
        act = (_silu(parts[0]) * parts[1]).astype(BF16)
        acc = acc + _dot(act, wdn_ref[c:c + FF_CHUNK, :])
    o_ref[...] = acc


def _ffn(h, g, wup, cw, cb, wdn, seq):
    n, d = h.shape
    tiles_per_seq = seq // TM_PROJ
    halo_blocks = TM_PROJ // SUBLANES_BF16
    const = dict(pipeline_mode=pl.Buffered(1))
    return pl.pallas_call(
        functools.partial(_ffn_body, tiles_per_seq=tiles_per_seq),
        grid=(n // TM_PROJ,),
        in_specs=[
            pl.BlockSpec((TM_PROJ, d), lambda i: (i, 0)),
            pl.BlockSpec((SUBLANES_BF16, d), lambda i: (jnp.maximum(i * halo_blocks - 1, 0), 0)),
            pl.BlockSpec((1, d), lambda i: (0, 0)),
            pl.BlockSpec((d, 2 * D_FF), lambda i: (0, 0), **const),
            pl.BlockSpec((FFN_CONV, 2 * D_FF), lambda i: (0, 0)),
            pl.BlockSpec((1, 2 * D_FF), lambda i: (0, 0)),
            pl.BlockSpec((D_FF, d), lambda i: (0, 0), **const),
        ],
        out_specs=pl.BlockSpec((TM_PROJ, d), lambda i: (i, 0)),
        out_shape=jax.ShapeDtypeStruct((n, d), F32),
        scratch_shapes=[pltpu.VMEM((TM_PROJ + SUBLANES_BF16, d), BF16)],
        compiler_params=_cparams(1),
        name="ffn",
    )(h, h, g, wup, cw, cb, wdn)


def _even_prep_body(uv_ref, qkv_ref, halo_ref, gate_ref, ws_ref, bs_ref, cw_ref, gp_ref,
                    ya_ref, qkvo_ref, gout_ref):
    tl = uv_ref.shape[0]
    t = pl.program_id(1)
    gw = LANES
    row = lax.broadcasted_iota(I32, (GMLP_CHUNK, GMLP_CHUNK), 0)
    col = lax.broadcasted_iota(I32, (GMLP_CHUNK, GMLP_CHUNK), 1)
    for gi in range(GMLP_GROUPS):
        u = _gelu_tanh(uv_ref[:, gi * gw:(gi + 1) * gw].astype(F32))
        v = _gelu_tanh(uv_ref[:, (GMLP_GROUPS + gi) * gw:(GMLP_GROUPS + gi + 1) * gw].astype(F32))
        vc = v - jnp.mean(v, axis=-1, keepdims=True)
        vn = (vc * lax.rsqrt(jnp.mean(vc * vc, axis=-1, keepdims=True) + EPS)).astype(BF16)
        w = jnp.where(col <= row, ws_ref[gi], 0.0).astype(BF16)
        b = bs_ref[:, gi:gi + 1]
        for c in range(tl // GMLP_CHUNK):
            r0 = c * GMLP_CHUNK
            mixed = _dot(w, vn[r0:r0 + GMLP_CHUNK]) + b
            ya_ref[r0:r0 + GMLP_CHUNK, gi * gw:(gi + 1) * gw] = (u[r0:r0 + GMLP_CHUNK] * mixed).astype(BF16)
    keep = jnp.where(t == 0, 0.0, 1.0)
    hb = SUBLANES_BF16
    for j in range(3 * GDN_HEADS):
        sl = slice(j * gw, (j + 1) * gw)
        x = jnp.concatenate([halo_ref[:, sl].astype(F32) * keep, qkv_ref[:, sl].astype(F32)], axis=0)
        cw = cw_ref[:, sl]
        y = cw[0:1] * x[hb - 3:tl + hb - 3]
        for k in range(1, GDN_CONV):
            y = y + cw[k:k + 1] * x[hb - 3 + k:tl + hb - 3 + k]
        y = _silu(y)
        if j < 2 * GDN_HEADS:
            y = y * lax.rsqrt(jnp.sum(y * y, axis=-1, keepdims=True) + EPS)
        if j < GDN_HEADS:
            y = y * (GDN_DIM ** -0.5)
        qkvo_ref[:, sl] = y.astype(BF16)
    x = gate_ref[...]
    lane = lax.broadcasted_iota(I32, x.shape, 1)
    beta = _sigmoid(x)
    gdec = -jnp.exp(gp_ref[0:1, :]) * _softplus(x + gp_ref[1:2, :])
    gout_ref[...] = jnp.where(lane < GDN_HEADS, beta, gdec)


def _even_prep(proj, gates, ws, bs_t, cw, gp, batch, seq):
    n = proj.shape[0]
    nt = seq // TL_PREP
    halo_blocks = TL_PREP // SUBLANES_BF16
    qkv_w = 3 * GDN_HEADS * GDN_DIM
    row = lambda b, t: b * nt + t
    return pl.pallas_call(
        _even_prep_body,
        grid=(batch, nt),
        in_specs=[
            pl.BlockSpec((TL_PREP, 2 * GMLP_GROUPS * LANES), lambda b, t: (row(b, t), 0)),
            pl.BlockSpec((TL_PREP, qkv_w), lambda b, t: (row(b, t), 1)),
            pl.BlockSpec((SUBLANES_BF16, qkv_w), lambda b, t: (jnp.maximum(row(b, t) * halo_blocks - 1, 0), 1)),
            pl.BlockSpec((TL_PREP, LANES), lambda b, t: (row(b, t), 0)),
            pl.BlockSpec((GMLP_GROUPS, GMLP_CHUNK, GMLP_CHUNK), lambda b, t: (0, 0, 0)),
            pl.BlockSpec((GMLP_CHUNK, GMLP_GROUPS), lambda b, t: (0, 0)),
            pl.BlockSpec((GDN_CONV, qkv_w), lambda b, t: (0, 0)),
            pl.BlockSpec((2, LANES), lambda b, t: (0, 0)),
        ],
        out_specs=[
            pl.BlockSpec((TL_PREP, GMLP_GROUPS * LANES), lambda b, t: (row(b, t), 0)),
            pl.BlockSpec((TL_PREP, qkv_w), lambda b, t: (row(b, t), 0)),
            pl.BlockSpec((TL_PREP, LANES), lambda b, t: (row(b, t), 0)),
        ],
        out_shape=[
            jax.ShapeDtypeStruct((n, GMLP_GROUPS * LANES), BF16),
            jax.ShapeDtypeStruct((n, qkv_w), BF16),
            jax.ShapeDtypeStruct((n, LANES), F32),
        ],
        compiler_params=_cparams(2),
        name="even_prep",
    )(proj, proj, proj, gates, ws, bs_t, cw, gp)


def _unit_lower_inverse(low):
    c = low.shape[0]
    row = lax.broadcasted_iota(I32, (c, c), 0)
    col = lax.broadcasted_iota(I32, (c, c), 1)
    eye = jnp.where(row == col, 1.0, 0.0)
    same16 = jnp.right_shift(row, 4) == jnp.right_shift(col, 4)
    same32 = jnp.right_shift(row, 5) == jnp.right_shift(col, 5)
    hp = dict(precision=HIGHEST)
    m = jnp.where(same16, -low, 0.0)
    x = eye + m
    p = _dot(m, m, **hp)
    x = x + _dot(x, p, **hp)
    p = _dot(p, p, **hp)
    x = x + _dot(x, p, **hp)
    p = _dot(p, p, **hp)
    x = x + _dot(x, p, **hp)
    n1 = jnp.where(jnp.logical_and(same32, jnp.logical_not(same16)), low, 0.0)
    x = x - _dot(x, _dot(n1, x, **hp), **hp)
    n2 = jnp.where(same32, 0.0, low)
    x = x - _dot(x, _dot(n2, x, **hp), **hp)
    return x


def _gdn_body(qkv_ref, gate_ref, z_ref, gn_ref, o_ref, s_ref):
    tl = qkv_ref.shape[0]
    c = GDN_CHUNK
    d = GDN_DIM
    t = pl.program_id(1)

    @pl.when(t == 0)
    def _():
        s_ref[...] = jnp.zeros_like(s_ref)

    row = lax.broadcasted_iota(I32, (c, c), 0)
    col = lax.broadcasted_iota(I32, (c, c), 1)
    incl = col <= row
    strict = col < row
    lmat = jnp.where(incl, 1.0, 0.0)
    umat = jnp.where(strict, 1.0, 0.0)
    gn = gn_ref[...]

    def chunk(ci, carry):
        r0 = pl.multiple_of(ci * c, c)
        gates = gate_ref[pl.ds(r0, c), :]
        for hd in range(GDN_HEADS):
            q = qkv_ref[pl.ds(r0, c), hd * d:(hd + 1) * d]
            k = qkv_ref[pl.ds(r0, c), (GDN_HEADS + hd) * d:(GDN_HEADS + hd + 1) * d]
            v = qkv_ref[pl.ds(r0, c), (2 * GDN_HEADS + hd) * d:(2 * GDN_HEADS + hd + 1) * d]
            qf, kf, vf = q.astype(F32), k.astype(F32), v.astype(F32)
            beta_b = jnp.broadcast_to(gates[:, hd:hd + 1], (c, d))
            g_b = jnp.broadcast_to(gates[:, GDN_HEADS + hd:GDN_HEADS + hd + 1], (c, d))
            gc_b = _dot(lmat, g_b, precision=HIGHEST)
            dmat = _dot(lmat, g_b[:, :c] * umat, precision=HIGHEST)
            decay = jnp.where(incl, jnp.exp(jnp.where(incl, dmat, 0.0)), 0.0)
            kb = kf * beta_b
            vb = vf * beta_b
            low = jnp.where(strict, _dot_nt(kb.astype(BF16), k) * decay, 0.0)
            tinv = _unit_lower_inverse(low).astype(BF16)
            egc = jnp.exp(gc_b)
            u = _dot(tinv, vb.astype(BF16))
            w = _dot(tinv, (kb * egc).astype(BF16))
            intra = jnp.where(incl, _dot_nt(q, k) * decay, 0.0)
            g_last = gc_b[c - 1:c, :]
            state = s_ref[hd]
            sb = state.astype(BF16)
            v_new = u - _dot(w.astype(BF16), sb)
            vnb = v_new.astype(BF16)
            out = _dot((qf * egc).astype(BF16), sb) + _dot(intra.astype(BF16), vnb)
            kd = (kf * jnp.exp(g_last - gc_b)).astype(BF16)
            s_ref[hd] = state * jnp.exp(g_last) + _dot_tn(kd, vnb)
            z = z_ref[pl.ds(r0, c), hd * d:(hd + 1) * d].astype(F32)
            o_ref[pl.ds(r0, c), hd * d:(hd + 1) * d] = (_rms(out, gn) * _silu(z)).astype(BF16)
        return carry

    lax.fori_loop(0, tl // c, chunk, 0)


def _gdn(qkv, gates, proj, gn, batch, seq):
    n = qkv.shape[0]
    nt = seq // TL_GDN
    width = GDN_HEADS * GDN_DIM
    row = lambda b, t: b * nt + t
    return pl.pallas_call(
        _gdn_body,
        grid=(batch, nt),
        in_specs=[
            pl.BlockSpec((TL_GDN, 3 * width), lambda b, t: (row(b, t), 0)),
            pl.BlockSpec((TL_GDN, LANES), lambda b, t: (row(b, t), 0)),
            pl.BlockSpec((TL_GDN, width), lambda b, t: (row(b, t), 2)),
            pl.BlockSpec((1, GDN_DIM), lambda b, t: (0, 0)),
        ],
        out_specs=pl.BlockSpec((TL_GDN, width), lambda b, t: (row(b, t), 0)),
        out_shape=jax.ShapeDtypeStruct((n, width), BF16),
        scratch_shapes=[pltpu.VMEM((GDN_HEADS, GDN_DIM, GDN_DIM), F32)],
        compiler_params=_cparams(2),
        name="gdn",
    )(qkv, gates, proj, gn)


def _group_mean_sq(x, ones_bd, group):
    x2 = x * x
    hi = x2.astype(BF16)
    lo = (x2 - hi.astype(F32)).astype(BF16)
    return (_dot(hi, ones_bd) + _dot(lo, ones_bd)) * (1.0 / group)


def _odd_prep_body(dq_ref, dk_ref, sq_ref, sk_ref, gq_ref, gk_ref, gsq_ref, gsk_ref, bd64_ref, bd128_ref,
                   dqo_ref, dko_ref, sqo_ref, sko_ref):
    bd64 = bd64_ref[...]
    x = dq_ref[...].astype(F32)
    dqo_ref[...] = (x * lax.rsqrt(_group_mean_sq(x, bd64, DIFF_QK) + EPS) * gq_ref[...]
                    * (DIFF_QK ** -0.5)).astype(BF16)
    x = dk_ref[...].astype(F32)
    dko_ref[...] = (x * lax.rsqrt(_group_mean_sq(x, bd64, DIFF_QK) + EPS) * gk_ref[...]).astype(BF16)
    x = sq_ref[...].astype(F32)
    sqo_ref[...] = (x * lax.rsqrt(_group_mean_sq(x, bd128_ref[...], DSA_DIM) + EPS) * gsq_ref[...]
                    * (DSA_DIM ** -0.5)).astype(BF16)
    x = sk_ref[...].astype(F32)
    sko_ref[...] = _rms(x, gsk_ref[...]).astype(BF16)


def _odd_prep(proj, gq, gk, gsq, gsk, bd64, bd128):
    n = proj.shape[0]
    w = 512
    full = lambda shape: pl.BlockSpec(shape, lambda i: (0,) * len(shape))
    return pl.pallas_call(
        _odd_prep_body,
        grid=(n // TL_PREP,),
        in_specs=[
            pl.BlockSpec((TL_PREP, w), lambda i: (i, 0)),
            pl.BlockSpec((TL_PREP, w), lambda i: (i, 1)),
            pl.BlockSpec((TL_PREP, w), lambda i: (i, 3)),
            pl.BlockSpec((TL_PREP, LANES), lambda i: (i, 5 * w // LANES)),
            full((1, w)), full((1, w)), full((1, w)), full((1, LANES)), full((w, w)), full((w, w)),
        ],
        out_specs=[
            pl.BlockSpec((TL_PREP, w), lambda i: (i, 0)),
            pl.BlockSpec((TL_PREP, w), lambda i: (i, 0)),
            pl.BlockSpec((TL_PREP, w), lambda i: (i, 0)),
            pl.BlockSpec((TL_PREP, LANES), lambda i: (i, 0)),
        ],
        out_shape=[
            jax.ShapeDtypeStruct((n, w), BF16), jax.ShapeDtypeStruct((n, w), BF16),
            jax.ShapeDtypeStruct((n, w), BF16), jax.ShapeDtypeStruct((n, LANES), BF16),
        ],
        compiler_params=_cparams(1),
        name="odd_prep",
    )(proj, proj, proj, proj, gq, gk, gsq, gsk, bd64, bd128)


def _softmax_step(s, v, m_ref, l_ref, acc_ref):
    m_prev = m_ref[...]
    m_new = jnp.maximum(m_prev, jnp.max(s, axis=-1, keepdims=True))
    alpha = jnp.exp(m_prev - m_new)
    p = jnp.exp(s - m_new)
    l_ref[...] = alpha * l_ref[...] + jnp.sum(p, axis=-1, keepdims=True)
    acc_ref[...] = alpha * acc_ref[...] + _dot(p.astype(BF16), v)
    m_ref[...] = m_new


def _diff_attn_body(far_ref, q_ref, k_ref, v_ref, bias_ref, lam_ref, sg_ref, o_ref,
                    qs_ref, m_ref, l_ref, acc_ref, *, lambda_init):
    hd = pl.program_id(1)
    qi = pl.program_id(2)
    q = q_ref[...]
    lane = lax.broadcasted_iota(I32, q.shape, 1)
    zero = jnp.zeros_like(q)
    qs_ref[:TQ, :] = jnp.where(lane < DIFF_QK, q, zero)
    qs_ref[TQ:, :] = jnp.where(lane >= DIFF_QK, q, zero)
    m_ref[...] = jnp.full_like(m_ref, NEG)
    l_ref[...] = jnp.zeros_like(l_ref)
    acc_ref[...] = jnp.zeros_like(acc_ref)
    far = far_ref[hd]

    def far_block(kb, carry):
        r0 = pl.multiple_of(kb * TK, TK)
        s = _dot_nt(qs_ref[...], k_ref[pl.ds(r0, TK), :]) + far
        _softmax_step(s, v_ref[pl.ds(r0, TK), :], m_ref, l_ref, acc_ref)
        return carry

    lax.fori_loop(0, jnp.maximum(qi - 1, 0), far_block, 0)

    @pl.when(qi > 0)
    def _():
        r0 = pl.multiple_of((qi - 1) * TK, TK)
        b = bias_ref[:, :TK]
        s = _dot_nt(qs_ref[...], k_ref[pl.ds(r0, TK), :]) + jnp.concatenate([b, b], axis=0)
        _softmax_step(s, v_ref[pl.ds(r0, TK), :], m_ref, l_ref, acc_ref)

    r0 = pl.multiple_of(qi * TK, TK)
    b = bias_ref[:, TK:]
    row = lax.broadcasted_iota(I32, (2 * TQ, TK), 0)
    col = lax.broadcasted_iota(I32, (2 * TQ, TK), 1)
    causal = col <= jnp.where(row >= TQ, row - TQ, row)
    s = _dot_nt(qs_ref[...], k_ref[pl.ds(r0, TK), :]) + jnp.concatenate([b, b], axis=0)
    _softmax_step(jnp.where(causal, s, NEG), v_ref[pl.ds(r0, TK), :], m_ref, l_ref, acc_ref)

    lf = lam_ref[...]
    lam = (jnp.exp(jnp.sum(lf[0:1] * lf[1:2], axis=-1, keepdims=True))
           - jnp.exp(jnp.sum(lf[2:3] * lf[3:4], axis=-1, keepdims=True)) + lambda_init)
    o = acc_ref[:TQ, :] / l_ref[:TQ, :] - lam * (acc_ref[TQ:, :] / l_ref[TQ:, :])
    o_ref[...] = (_rms(o, sg_ref[...]) * (1.0 - lambda_init)).astype(BF16)


def _diff_attn(far, dqn, dkn, proj, bias, lam_p, sub_g, batch, seq, lambda_init):
    n = dqn.shape[0]
    nq = seq // TQ
    dv_block0 = 2 * DIFF_HEADS
    return pl.pallas_call(
        functools.partial(_diff_attn_body, lambda_init=lambda_init),
        grid=(batch, DIFF_HEADS, nq),
        in_specs=[
            pl.BlockSpec(memory_space=pltpu.SMEM),
            pl.BlockSpec((TQ, LANES), lambda b, h, q: (b * nq + q, h)),
            pl.BlockSpec((seq, LANES), lambda b, h, q: (b, h)),
            pl.BlockSpec((seq, LANES), lambda b, h, q: (b, dv_block0 + h)),
            pl.BlockSpec((None, TQ, 2 * TK), lambda b, h, q: (h, 0, 0)),
            pl.BlockSpec((4, DIFF_QK), lambda b, h, q: (0, 0)),
            pl.BlockSpec((1, LANES), lambda b, h, q: (0, 0)),
        ],
        out_specs=pl.BlockSpec((TQ, LANES), lambda b, h, q: (b * nq + q, h)),
        out_shape=jax.ShapeDtypeStruct((n, DIFF_HEADS * LANES), BF16),
        scratch_shapes=[
            pltpu.VMEM((2 * TQ, LANES), BF16),
            pltpu.VMEM((2 * TQ, 1), F32),
            pltpu.VMEM((2 * TQ, 1), F32),
            pltpu.VMEM((2 * TQ, LANES), F32),
        ],
        compiler_params=_cparams(3),
        name="diff_attn",
    )(far, dqn, dkn, proj, bias, lam_p, sub_g)


def _dsa_body(far_ref, q_ref, k_ref, v_ref, iq_ref, ik_ref, iw_ref, bias_ref, o_ref,
              qs_ref, qi_ref, wb_ref, keys_ref, t_ref, m_ref, l_ref, acc_ref, *, top_k, pos_bits):
    qb = pl.program_id(1)
    nkb = qb + 1
    nsub = TK // LANES

    for p in range(IDX_HEADS // 2):
        grp = iq_ref[:, p * LANES:(p + 1) * LANES]
        lane = lax.broadcasted_iota(I32, grp.shape, 1)
        zero = jnp.zeros_like(grp)
        qi_ref[(2 * p) * TQ:(2 * p + 1) * TQ, :] = jnp.where(lane < IDX_DIM, grp, zero)
        qi_ref[(2 * p + 1) * TQ:(2 * p + 2) * TQ, :] = jnp.where(lane >= IDX_DIM, grp, zero)
    for hd in range(DSA_HEADS):
        qs_ref[hd * TQ:(hd + 1) * TQ, :] = q_ref[:, hd * LANES:(hd + 1) * LANES]
    wscale = (IDX_HEADS ** -0.5) * (IDX_DIM ** -0.5)
    for hi in range(IDX_HEADS):
        wb_ref[hi] = jnp.broadcast_to(iw_ref[:, hi:hi + 1] * wscale, (TQ, LANES))

    row = lax.broadcasted_iota(I32, (TQ, TK), 0)
    col = lax.broadcasted_iota(I32, (TQ, TK), 1)
    causal = col <= row

    def index_block(kb, masked):
        r0 = pl.multiple_of(kb * TK, TK)
        s = _dot_nt(qi_ref[...], ik_ref[pl.ds(r0, TK), :])
        idx = jnp.zeros((TQ, TK), F32)
        for hi in range(IDX_HEADS):
            wb = wb_ref[hi]
            wt = jnp.concatenate([wb] * nsub, axis=1)
            idx = idx + jnp.maximum(s[hi * TQ:(hi + 1) * TQ], 0.0) * wt
        idx = jnp.where(idx == 0.0, 0.0, idx)
        bits = pltpu.bitcast(idx, I32)
        key = jnp.where(bits < 0, bits ^ jnp.int32(0x7FFFFFFF), bits)
        if masked:
            key = jnp.where(causal, key, jnp.int32(INT_MIN))
        keys_ref[kb] = key

    def index_loop(kb, carry):
        index_block(kb, False)
        return carry

    lax.fori_loop(0, nkb - 1, index_loop, 0)
    index_block(nkb - 1, True)

    lane = lax.broadcasted_iota(I32, (TQ, LANES), 1)

    def count(pred):
        def body(kb, acc):
            kk = keys_ref[kb]
            for sub in range(nsub):
                pos = lane + (kb * TK + sub * LANES)
                acc = acc + jnp.where(pred(kk[:, sub * LANES:(sub + 1) * LANES], pos), 1.0, 0.0)
            return acc
        acc = lax.fori_loop(0, nkb, body, jnp.zeros((TQ, LANES), F32))
        return jnp.sum(acc, axis=-1, keepdims=True)

    kf = float(top_k)

    def count_ge(cand):
        cb = jnp.broadcast_to(cand, (TQ, LANES))
        return count(lambda kk, pos: kk >= cb)

    total = (nkb * TK).astype(F32) * jnp.ones((TQ, 1), F32)
    c0 = count_ge(jnp.zeros((TQ, 1), I32))
    acc0 = c0 >= kf
    t0 = jnp.where(acc0, jnp.int32(0), jnp.int32(INT_MIN))
    cge0 = jnp.where(acc0, c0, total)

    def bit_step(i, carry):
        t, cge = carry
        cand = t | jnp.left_shift(jnp.int32(1), 30 - i)
        cnt = count_ge(cand)
        ok = cnt >= kf
        return jnp.where(ok, cand, t), jnp.where(ok, cnt, cge)

    t, cge = lax.fori_loop(0, 31, bit_step, (t0, cge0))
    t_ref[...] = t

    @pl.when(jnp.max(cge) > kf)
    def _():
        tt = t_ref[...]
        tb = jnp.broadcast_to(tt, (TQ, LANES))
        r = kf - count(lambda kk, pos: kk > tb)

        def count_tied_before(cand):
            cb = jnp.broadcast_to(cand, (TQ, LANES))
            return count(lambda kk, pos: jnp.logical_and(kk == tb, pos < cb))

        def pos_step(i, pcut):
            cand = pcut | jnp.left_shift(jnp.int32(1), pos_bits - 1 - i)
            return jnp.where(count_tied_before(cand) <= r, cand, pcut)

        pcut = lax.fori_loop(0, pos_bits, pos_step, jnp.zeros((TQ, 1), I32))
        pw = jnp.broadcast_to(pcut, (TQ, TK))
        tw = jnp.broadcast_to(tt, (TQ, TK))

        def demote(kb, carry):
            kk = keys_ref[kb]
            drop = jnp.logical_and(kk == tw, col + kb * TK >= pw)
            keys_ref[kb] = jnp.where(drop, kk - 1, kk)
            return carry

        lax.fori_loop(0, nkb, demote, 0)

    m_ref[...] = jnp.full_like(m_ref, NEG)
    l_ref[...] = jnp.zeros_like(l_ref)
    acc_ref[...] = jnp.zeros_like(acc_ref)

    def attend(kb, bias_of_head, diagonal):
        r0 = pl.multiple_of(kb * TK, TK)
        sel = keys_ref[kb] >= jnp.broadcast_to(t_ref[...], (TQ, TK))
        if diagonal:
            sel = jnp.logical_and(sel, causal)
        s = _dot_nt(qs_ref[...], k_ref[pl.ds(r0, TK), :])
        s = jnp.concatenate(
            [jnp.where(sel, s[hd * TQ:(hd + 1) * TQ] + bias_of_head(hd), NEG) for hd in range(DSA_HEADS)], axis=0)
        _softmax_step(s, v_ref[pl.ds(r0, TK), :], m_ref, l_ref, acc_ref)

    def far_block(kb, carry):
        attend(kb, lambda hd: far_ref[DIFF_HEADS + hd], False)
        return carry

    lax.fori_loop(0, jnp.maximum(qb - 1, 0), far_block, 0)

    @pl.when(qb > 0)
    def _():
        attend(qb - 1, lambda hd: bias_ref[hd, :, :TK], False)

    attend(qb, lambda hd: bias_ref[hd, :, TK:], True)

    for hd in range(DSA_HEADS):
        sl = slice(hd * TQ, (hd + 1) * TQ)
        o_ref[:, hd * LANES:(hd + 1) * LANES] = (acc_ref[sl, :] / l_ref[sl, :]).astype(BF16)


def _dsa(far, sqn, skn, proj, iw, bias, batch, seq):
    n = sqn.shape[0]
    nq = seq // TQ
    top_k = min(IDX_TOPK_MAX, seq // 4)
    pos_bits = int(seq).bit_length()
    iq_block = 4
    sv_block = (5 * 512 + LANES) // LANES
    ik_block = sv_block + 1
    return pl.pallas_call(
        functools.partial(_dsa_body, top_k=top_k, pos_bits=pos_bits),
        grid=(batch, nq),
        in_specs=[
            pl.BlockSpec(memory_space=pltpu.SMEM),
            pl.BlockSpec((TQ, DSA_HEADS * LANES), lambda b, q: (b * nq + q, 0)),
            pl.BlockSpec((seq, LANES), lambda b, q: (b, 0)),
            pl.BlockSpec((seq, LANES), lambda b, q: (b, sv_block)),
            pl.BlockSpec((TQ, IDX_HEADS * IDX_DIM), lambda b, q: (b * nq + q, iq_block)),
            pl.BlockSpec((seq, LANES), lambda b, q: (b, ik_block)),
            pl.BlockSpec((TQ, LANES), lambda b, q: (b * nq + q, 0)),
            pl.BlockSpec((DSA_HEADS, TQ, 2 * TK), lambda b, q: (1, 0, 0)),
        ],
        out_specs=pl.BlockSpec((TQ, DSA_HEADS * LANES), lambda b, q: (b * nq + q, 0)),
        out_shape=jax.ShapeDtypeStruct((n, DSA_HEADS * LANES), BF16),
        scratch_shapes=[
            pltpu.VMEM((DSA_HEADS * TQ, LANES), BF16),
            pltpu.VMEM((IDX_HEADS * TQ, LANES), BF16),
            pltpu.VMEM((IDX_HEADS, TQ, LANES), F32),
            pltpu.VMEM((seq // TK, TQ, TK), I32),
            pltpu.VMEM((TQ, 1), I32),
            pltpu.VMEM((DSA_HEADS * TQ, 1), F32),
            pltpu.VMEM((DSA_HEADS * TQ, 1), F32),
            pltpu.VMEM((DSA_HEADS * TQ, LANES), F32),
        ],
        compiler_params=_cparams(2),
        name="dsa",
    )(far, sqn, skn, proj, proj, proj, iw, bias)


def _rel_bucket(dist):
    exact = REL_BUCKETS // 2
    n = jnp.maximum(dist, 0)
    nf = jnp.maximum(n, exact).astype(F32)
    far = exact + (jnp.log(nf / exact) / math.log(REL_MAX_DIST / exact) * (REL_BUCKETS - exact)).astype(I32)
    return jnp.where(n < exact, n, jnp.minimum(far, REL_BUCKETS - 1))


def _bias_tables(rel_bias):
    i = jnp.arange(TQ, dtype=I32)[:, None]
    j = jnp.arange(2 * TK, dtype=I32)[None, :]
    near = rel_bias[_rel_bucket(i - j + TK)]
    far = rel_bias[_rel_bucket(jnp.full((), TK + 1, I32))]
    return jnp.moveaxis(near, -1, 0).astype(F32), far.astype(F32)


def _block_diag_ones(width, group):
    r = np.arange(width)
    return jnp.asarray((r[:, None] // group) == (r[None, :] // group), dtype=BF16)


def _pad_lanes(w, width=LANES):
    return jnp.pad(w, ((0, 0), (0, width - w.shape[1])))


def kernel(x, rel_bias, mix_norm_g, ev_w_in, ev_w_out, gmlp_w_s, gmlp_b_s, gdn_conv_w, gdn_a_log, gdn_dt_bias,
           gdn_norm_g, od_w_in, od_w_out, diff_q_norm_g, diff_k_norm_g, diff_lambda, diff_sub_norm_g,
           dsa_q_norm_g, dsa_k_norm_g, ffn_norm_g, ffn_w_up, ffn_conv_w, ffn_conv_b, ffn_w_down):
    batch, seq, d = x.shape
    n = batch * seq
    depth = mix_norm_g.shape[0]
    assert d == D_MODEL and seq % max(TM_PROJ, TL_PREP, TL_GDN, TQ) == 0
    h = x.reshape(n, d)
    bias_near, bias_far = _bias_tables(rel_bias)
    bd64 = _block_diag_ones(512, DIFF_QK)
    bd128 = _block_diag_ones(512, DSA_DIM)
    gw = GMLP_GROUPS * LANES
    qkv_w = 3 * GDN_HEADS * GDN_DIM

    for layer in range(depth):
        j = layer // 2
        g_mix = mix_norm_g[layer].reshape(1, d)
        if layer % 2 == 0:
            w = ev_w_in[j]
            o_u, o_v, o_qkv = 0, gw, 2 * gw
            o_b = o_qkv + qkv_w
            o_a = o_b + GDN_HEADS
            o_z = o_a + GDN_HEADS
            w_main = jnp.concatenate(
                [w[:, o_u:o_v], w[:, o_v:o_qkv], w[:, o_z:o_z + GDN_HEADS * GDN_DIM], w[:, o_qkv:o_b]],
                axis=1).astype(BF16)
            w_gate = _pad_lanes(w[:, o_b:o_z]).astype(BF16)
            proj, gates = _in_proj(h, g_mix, w_main, w_gate)
            gp = jnp.stack([
                _pad_lanes(jnp.concatenate([jnp.zeros((GDN_HEADS,), F32), gdn_a_log[j]])[None])[0],
                _pad_lanes(jnp.concatenate([jnp.zeros((GDN_HEADS,), F32), gdn_dt_bias[j]])[None])[0]])
            y_a, qkvn, gates2 = _even_prep(proj, gates, gmlp_w_s[j], gmlp_b_s[j].T, gdn_conv_w[j], gp, batch, seq)
            y_b = _gdn(qkvn, gates2, proj, gdn_norm_g[j].reshape(1, GDN_DIM), batch, seq)
            h = _out_proj(h, y_a, y_b, ev_w_out[j].astype(BF16))
        else:
            lambda_init = 0.8 - 0.6 * math.exp(-0.3 * layer)
            w = od_w_in[j]
            c = np.cumsum([0, 512, 512, 512, 512, 128, 128, 512, 64, 8])
            dq, dk, dv, sq, sk, sv, iq, ik, iw = [w[:, c[i]:c[i + 1]] for i in range(9)]
            w_main = jnp.concatenate([dq, dk, dv, sq, iq, sk, sv, ik, ik], axis=1).astype(BF16)
            w_gate = _pad_lanes(iw).astype(BF16)
            proj, iw_out = _in_proj(h, g_mix, w_main, w_gate)
            dqn, dkn, sqn, skn = _odd_prep(
                proj,
                jnp.tile(diff_q_norm_g[j], 2 * DIFF_HEADS)[None], jnp.tile(diff_k_norm_g[j], 2 * DIFF_HEADS)[None],
                jnp.tile(dsa_q_norm_g[j], DSA_HEADS)[None], dsa_k_norm_g[j][None], bd64, bd128)
            y_c = _diff_attn(bias_far, dqn, dkn, proj, bias_near, diff_lambda[j], diff_sub_norm_g[j][None],
                             batch, seq, lambda_init)
            y_d = _dsa(bias_far, sqn, skn, proj, iw_out, bias_near, batch, seq)
            h = _out_proj(h, y_c, y_d, od_w_out[j].astype(BF16))
        h = _ffn(h, ffn_norm_g[layer].reshape(1, d), ffn_w_up[layer].astype(BF16), ffn_conv_w[layer],
                 ffn_conv_b[layer].reshape(1, 2 * D_FF), ffn_w_down[layer].astype(BF16), seq)
    return h.reshape(batch, seq, d)
```

```python
import functools
import math

import numpy as np
import jax
import jax.numpy as jnp
from jax import lax
from jax.experimental import pallas as pl
from jax.experimental.pallas import tpu as pltpu

F32 = jnp.float32
BF16 = jnp.bfloat16
I32 = jnp.int32
HIGHEST = lax.Precision.HIGHEST

D_MODEL = 1024
GMLP_GROUPS = 4
GMLP_CHUNK = 128
GDN_HEADS = 4
GDN_DIM = 128
GDN_CHUNK = 64
GDN_CONV = 4
DIFF_HEADS = 4
DIFF_QK = 64
DSA_HEADS = 4
DSA_DIM = 128
IDX_HEADS = 8
IDX_DIM = 64
IDX_TOPK_MAX = 256
REL_BUCKETS = 32
REL_MAX_DIST = 128
D_FF = 2816
FFN_CONV = 3
EPS = 1e-6

LANES = 128
SUBLANES_F32 = 8
SUBLANES_BF16 = 16
VMEM_LIMIT = 56 * 1024 * 1024

TM_PROJ = 512
TL_PREP = 512
TL_GDN = 512
TQ = 256
TK = 256
FF_CHUNK = 256
NEG = -1e30
INT_MIN = -2 ** 31


def _cparams(n_axes):
    return pltpu.CompilerParams(dimension_semantics=("arbitrary",) * n_axes, vmem_limit_bytes=VMEM_LIMIT)


def _dot(a, b, **kw):
    return jnp.dot(a, b, preferred_element_type=F32, **kw)


def _dot_nt(a, b, **kw):
    return lax.dot_general(a, b, (((1,), (1,)), ((), ())), preferred_element_type=F32, **kw)


def _dot_tn(a, b, **kw):
    return lax.dot_general(a, b, (((0,), (0,)), ((), ())), preferred_element_type=F32, **kw)


def _rms(x, g):
    return x * lax.rsqrt(jnp.mean(x * x, axis=-1, keepdims=True) + EPS) * g


def _sigmoid(x):
    return 1.0 / (1.0 + jnp.exp(-x))


def _silu(x):
    return x * _sigmoid(x)


def _gelu_tanh(x):
    return 0.5 * x * (1.0 + jnp.tanh(math.sqrt(2.0 / math.pi) * (x + 0.044715 * (x * x * x))))


def _softplus(x):
    return jnp.maximum(x, 0.0) + jnp.log(1.0 + jnp.exp(-jnp.abs(x)))


def _in_proj_body(h_ref, g_ref, w_ref, wg_ref, o_ref, og_ref, *, nout):
    xn = _rms(h_ref[...], g_ref[...]).astype(BF16)
    for c in range(0, nout, 512):
        e = min(c + 512, nout)
        o_ref[:, c:e] = _dot(xn, w_ref[:, c:e]).astype(BF16)
    og_ref[...] = _dot(xn, wg_ref[...])


def _in_proj(h, g, w, wg):
    n, d = h.shape
    nout = w.shape[1]
    return pl.pallas_call(
        functools.partial(_in_proj_body, nout=nout),
        grid=(n // TM_PROJ,),
        in_specs=[
            pl.BlockSpec((TM_PROJ, d), lambda i: (i, 0)),
            pl.BlockSpec((1, d), lambda i: (0, 0)),
            pl.BlockSpec((d, nout), lambda i: (0, 0)),
            pl.BlockSpec((d, LANES), lambda i: (0, 0)),
        ],
        out_specs=[
            pl.BlockSpec((TM_PROJ, nout), lambda i: (i, 0)),
            pl.BlockSpec((TM_PROJ, LANES), lambda i: (i, 0)),
        ],
        out_shape=[jax.ShapeDtypeStruct((n, nout), BF16), jax.ShapeDtypeStruct((n, LANES), F32)],
        compiler_params=_cparams(1),
        name="in_proj",
    )(h, g, w, wg)


def _out_proj_body(h_ref, y1_ref, y2_ref, w_ref, o_ref):
    half = y1_ref.shape[1]
    o_ref[...] = h_ref[...] + _dot(y1_ref[...], w_ref[:half, :]) + _dot(y2_ref[...], w_ref[half:, :])


def _out_proj(h, y1, y2, w):
    n, d = h.shape
    half = y1.shape[1]
    return pl.pallas_call(
        _out_proj_body,
        grid=(n // TM_PROJ,),
        in_specs=[
            pl.BlockSpec((TM_PROJ, d), lambda i: (i, 0)),
            pl.BlockSpec((TM_PROJ, half), lambda i: (i, 0)),
            pl.BlockSpec((TM_PROJ, half), lambda i: (i, 0)),
            pl.BlockSpec((2 * half, d), lambda i: (0, 0)),
        ],
        out_specs=pl.BlockSpec((TM_PROJ, d), lambda i: (i, 0)),
        out_shape=jax.ShapeDtypeStruct((n, d), F32),
        compiler_params=_cparams(1),
        name="out_proj",
    )(h, y1, y2, w)


def _ffn_body(h_ref, halo_ref, g_ref, wup_ref, cw_ref, cb_ref, wdn_ref, o_ref, xn_ref, *, tiles_per_seq):
    tm = h_ref.shape[0]
    i = pl.program_id(0)
    h = h_ref[...]
    g = g_ref[...]
    hb = SUBLANES_BF16
    xn_ref[hb:, :] = _rms(h, g).astype(BF16)
    keep = jnp.where(i % tiles_per_seq == 0, 0.0, 1.0)
    xn_ref[:hb, :] = (_rms(halo_ref[...], g) * keep).astype(BF16)
    xn = xn_ref[...]
    acc = h
    for c in range(0, D_FF, FF_CHUNK):
        parts = []
        for base in (c, D_FF + c):
            up = _dot(xn, wup_ref[:, base:base + FF_CHUNK])
            cw = cw_ref[:, base:base + FF_CHUNK]
            y = (cw[0:1] * up[hb - 2:tm + hb - 2]
                 + cw[1:2] * up[hb - 1:tm + hb - 1]
                 + cw[2:3] * up[hb:tm + hb]
                 + cb_ref[:, base:base + FF_CHUNK])
            parts.append(y)
        act = (_silu(parts[0]) * parts[1]).astype(BF16)
        acc = acc + _dot(act, wdn_ref[c:c + FF_CHUNK, :])
    o_ref[...] = acc


def _ffn(h, g, wup, cw, cb, wdn, seq):
    n, d = h.shape
    tiles_per_seq = seq // TM_PROJ
    halo_blocks = TM_PROJ // SUBLANES_BF16
    const = dict(pipeline_mode=pl.Buffered(1))
    return pl.pallas_call(
        functools.partial(_ffn_body, tiles_per_seq=tiles_per_seq),
        grid=(n // TM_PROJ,),
        in_specs=[
            pl.BlockSpec((TM_PROJ, d), lambda i: (i, 0)),
            pl.BlockSpec((SUBLANES_BF16, d), lambda i: (jnp.maximum(i * halo_blocks - 1, 0), 0)),
            pl.BlockSpec((1, d), lambda i: (0, 0)),
            pl.BlockSpec((d, 2 * D_FF), lambda i: (0, 0), **const),
            pl.BlockSpec((FFN_CONV, 2 * D_FF), lambda i: (0, 0)),
            pl.BlockSpec((1, 2 * D_FF), lambda i: (0, 0)),
            pl.BlockSpec((D_FF, d), lambda i: (0, 0), **const),
        ],
        out_specs=pl.BlockSpec((TM_PROJ, d), lambda i: (i, 0)),
        out_shape=jax.ShapeDtypeStruct((n, d), F32),
        scratch_shapes=[pltpu.VMEM((TM_PROJ + SUBLANES_BF16, d), BF16)],
        compiler_params=_cparams(1),
        name="ffn",
    )(h, h, g, wup, cw, cb, wdn)


def _even_prep_body(uv_ref, qkv_ref, halo_ref, gate_ref, ws_ref, bs_ref, cw_ref, gp_ref,
                    ya_ref, qkvo_ref, gout_ref):
    tl = uv_ref.shape[0]
    t = pl.program_id(1)
    gw = LANES
    row = lax.broadcasted_iota(I32, (GMLP_CHUNK, GMLP_CHUNK), 0)
    col = lax.broadcasted_iota(I32, (GMLP_CHUNK, GMLP_CHUNK), 1)
    for gi in range(GMLP_GROUPS):
        u = _gelu_tanh(uv_ref[:, gi * gw:(gi + 1) * gw].astype(F32))
        v = _gelu_tanh(uv_ref[:, (GMLP_GROUPS + gi) * gw:(GMLP_GROUPS + gi + 1) * gw].astype(F32))
        vc = v - jnp.mean(v, axis=-1, keepdims=True)
        vn = (vc * lax.rsqrt(jnp.mean(vc * vc, axis=-1, keepdims=True) + EPS)).astype(BF16)
        w = jnp.where(col <= row, ws_ref[gi], 0.0).astype(BF16)
        b = bs_ref[:, gi:gi + 1]
        for c in range(tl // GMLP_CHUNK):
            r0 = c * GMLP_CHUNK
            mixed = _dot(w, vn[r0:r0 + GMLP_CHUNK]) + b
            ya_ref[r0:r0 + GMLP_CHUNK, gi * gw:(gi + 1) * gw] = (u[r0:r0 + GMLP_CHUNK] * mixed).astype(BF16)
    keep = jnp.where(t == 0, 0.0, 1.0)
    hb = SUBLANES_BF16
    for j in range(3 * GDN_HEADS):
        sl = slice(j * gw, (j + 1) * gw)
        x = jnp.concatenate([halo_ref[:, sl].astype(F32) * keep, qkv_ref[:, sl].astype(F32)], axis=0)
        cw = cw_ref[:, sl]
        y = cw[0:1] * x[hb - 3:tl + hb - 3]
        for k in range(1, GDN_CONV):
            y = y + cw[k:k + 1] * x[hb - 3 + k:tl + hb - 3 + k]
        y = _silu(y)
        if j < 2 * GDN_HEADS:
            y = y * lax.rsqrt(jnp.sum(y * y, axis=-1, keepdims=True) + EPS)
        if j < GDN_HEADS:
            y = y * (GDN_DIM ** -0.5)
        qkvo_ref[:, sl] = y.astype(BF16)
    x = gate_ref[...]
    lane = lax.broadcasted_iota(I32, x.shape, 1)
    beta = _sigmoid(x)
    gdec = -jnp.exp(gp_ref[0:1, :]) * _softplus(x + gp_ref[1:2, :])
    gout_ref[...] = jnp.where(lane < GDN_HEADS, beta, gdec)


def _even_prep(proj, gates, ws, bs_t, cw, gp, batch, seq):
    n = proj.shape[0]
    nt = seq // TL_PREP
    halo_blocks = TL_PREP // SUBLANES_BF16
    qkv_w = 3 * GDN_HEADS * GDN_DIM
    row = lambda b, t: b * nt + t
    return pl.pallas_call(
        _even_prep_body,
        grid=(batch, nt),
        in_specs=[
            pl.BlockSpec((TL_PREP, 2 * GMLP_GROUPS * LANES), lambda b, t: (row(b, t), 0)),
            pl.BlockSpec((TL_PREP, qkv_w), lambda b, t: (row(b, t), 1)),
            pl.BlockSpec((SUBLANES_BF16, qkv_w), lambda b, t: (jnp.maximum(row(b, t) * halo_blocks - 1, 0), 1)),
            pl.BlockSpec((TL_PREP, LANES), lambda b, t: (row(b, t), 0)),
            pl.BlockSpec((GMLP_GROUPS, GMLP_CHUNK, GMLP_CHUNK), lambda b, t: (0, 0, 0)),
            pl.BlockSpec((GMLP_CHUNK, GMLP_GROUPS), lambda b, t: (0, 0)),
            pl.BlockSpec((GDN_CONV, qkv_w), lambda b, t: (0, 0)),
            pl.BlockSpec((2, LANES), lambda b, t: (0, 0)),
        ],
        out_specs=[
            pl.BlockSpec((TL_PREP, GMLP_GROUPS * LANES), lambda b, t: (row(b, t), 0)),
            pl.BlockSpec((TL_PREP, qkv_w), lambda b, t: (row(b, t), 0)),
            pl.BlockSpec((TL_PREP, LANES), lambda b, t: (row(b, t), 0)),
        ],
        out_shape=[
            jax.ShapeDtypeStruct((n, GMLP_GROUPS * LANES), BF16),
            jax.ShapeDtypeStruct((n, qkv_w), BF16),
            jax.ShapeDtypeStruct((n, LANES), F32),
        ],
        compiler_params=_cparams(2),
        name="even_prep",
    )(proj, proj, proj, gates, ws, bs_t, cw, gp)


def _unit_lower_inverse(low):
    c = low.shape[0]
    row = lax.broadcasted_iota(I32, (c, c), 0)
    col = lax.broadcasted_iota(I32, (c, c), 1)
    eye = jnp.where(row == col, 1.0, 0.0)
    same16 = jnp.right_shift(row, 4) == jnp.right_shift(col, 4)
    same32 = jnp.right_shift(row, 5) == jnp.right_shift(col, 5)
    hp = dict(precision=HIGHEST)
    m = jnp.where(same16, -low, 0.0)
    x = eye + m
    p = _dot(m, m, **hp)
    x = x + _dot(x, p, **hp)
    p = _dot(p, p, **hp)
    x = x + _dot(x, p, **hp)
    p = _dot(p, p, **hp)
    x = x + _dot(x, p, **hp)
    n1 = jnp.where(jnp.logical_and(same32, jnp.logical_not(same16)), low, 0.0)
    x = x - _dot(x, _dot(n1, x, **hp), **hp)
    n2 = jnp.where(same32, 0.0, low)
    x = x - _dot(x, _dot(n2, x, **hp), **hp)
    return x


def _gdn_body(qkv_ref, gate_ref, z_ref, gn_ref, o_ref, s_ref):
    tl = qkv_ref.shape[0]
    c = GDN_CHUNK
    d = GDN_DIM
    t = pl.program_id(1)

    @pl.when(t == 0)
    def _():
        s_ref[...] = jnp.zeros_like(s_ref)

    row = lax.broadcasted_iota(I32, (c, c), 0)
    col = lax.broadcasted_iota(I32, (c, c), 1)
    incl = col <= row
    strict = col < row
    lmat = jnp.where(incl, 1.0, 0.0)
    umat = jnp.where(strict, 1.0, 0.0)
    gn = gn_ref[...]

    def chunk(ci, carry):
        r0 = pl.multiple_of(ci * c, c)
        gates = gate_ref[pl.ds(r0, c), :]
        for hd in range(GDN_HEADS):
            q = qkv_ref[pl.ds(r0, c), hd * d:(hd + 1) * d]
            k = qkv_ref[pl.ds(r0, c), (GDN_HEADS + hd) * d:(GDN_HEADS + hd + 1) * d]
            v = qkv_ref[pl.ds(r0, c), (2 * GDN_HEADS + hd) * d:(2 * GDN_HEADS + hd + 1) * d]
            qf, kf, vf = q.astype(F32), k.astype(F32), v.astype(F32)
            beta_b = jnp.broadcast_to(gates[:, hd:hd + 1], (c, d))
            g_b = jnp.broadcast_to(gates[:, GDN_HEADS + hd:GDN_HEADS + hd + 1], (c, d))
            gc_b = _dot(lmat, g_b, precision=HIGHEST)
            dmat = _dot(lmat, g_b[:, :c] * umat, precision=HIGHEST)
            decay = jnp.where(incl, jnp.exp(jnp.where(incl, dmat, 0.0)), 0.0)
            kb = kf * beta_b
            vb = vf * beta_b
            low = jnp.where(strict, _dot_nt(kb.astype(BF16), k) * decay, 0.0)
            tinv = _unit_lower_inverse(low).astype(BF16)
            egc = jnp.exp(gc_b)
            u = _dot(tinv, vb.astype(BF16))
            w = _dot(tinv, (kb * egc).astype(BF16))
            intra = jnp.where(incl, _dot_nt(q, k) * decay, 0.0)
            g_last = gc_b[c - 1:c, :]
            state = s_ref[hd]
            sb = state.astype(BF16)
            v_new = u - _dot(w.astype(BF16), sb)
            vnb = v_new.astype(BF16)
            out = _dot((qf * egc).astype(BF16), sb) + _dot(intra.astype(BF16), vnb)
            kd = (kf * jnp.exp(g_last - gc_b)).astype(BF16)
            s_ref[hd] = state * jnp.exp(g_last) + _dot_tn(kd, vnb)
            z = z_ref[pl.ds(r0, c), hd * d:(hd + 1) * d].astype(F32)
            o_ref[pl.ds(r0, c), hd * d:(hd + 1) * d] = (_rms(out, gn) * _silu(z)).astype(BF16)
        return carry

    lax.fori_loop(0, tl // c, chunk, 0)


def _gdn(qkv, gates, proj, gn, batch, seq):
    n = qkv.shape[0]
    nt = seq // TL_GDN
    width = GDN_HEADS * GDN_DIM
    row = lambda b, t: b * nt + t
    return pl.pallas_call(
        _gdn_body,
        grid=(batch, nt),
        in_specs=[
            pl.BlockSpec((TL_GDN, 3 * width), lambda b, t: (row(b, t), 0)),
            pl.BlockSpec((TL_GDN, LANES), lambda b, t: (row(b, t), 0)),
            pl.BlockSpec((TL_GDN, width), lambda b, t: (row(b, t), 2)),
            pl.BlockSpec((1, GDN_DIM), lambda b, t: (0, 0)),
        ],
        out_specs=pl.BlockSpec((TL_GDN, width), lambda b, t: (row(b, t), 0)),
        out_shape=jax.ShapeDtypeStruct((n, width), BF16),
        scratch_shapes=[pltpu.VMEM((GDN_HEADS, GDN_DIM, GDN_DIM), F32)],
        compiler_params=_cparams(2),
        name="gdn",
    )(qkv, gates, proj, gn)


def _group_mean_sq(x, ones_bd, group):
    x2 = x * x
    hi = x2.astype(BF16)
    lo = (x2 - hi.astype(F32)).astype(BF16)
    return (_dot(hi, ones_bd) + _dot(lo, ones_bd)) * (1.0 / group)


def _odd_prep_body(dq_ref, dk_ref, dv_ref, sq_ref, sk_ref, sv_ref, gq_ref, gk_ref, gsq_ref, gsk_ref, bd64_ref,
                   bd128_ref, dqo_ref, dko_ref, sqo_ref, sko_ref, vt_ref):
    for j in range(DIFF_HEADS + 1):
        src = sv_ref[...] if j == DIFF_HEADS else dv_ref[:, j * LANES:(j + 1) * LANES]
        xt = src.astype(F32).T
        for c in range(xt.shape[1] // TK):
            vt_ref[j, c] = xt[:, c * TK:(c + 1) * TK].astype(BF16)
    bd64 = bd64_ref[...]
    x = dq_ref[...].astype(F32)
    dqo_ref[...] = (x * lax.rsqrt(_group_mean_sq(x, bd64, DIFF_QK) + EPS) * gq_ref[...]
                    * (DIFF_QK ** -0.5)).astype(BF16)
    x = dk_ref[...].astype(F32)
    dko_ref[...] = (x * lax.rsqrt(_group_mean_sq(x, bd64, DIFF_QK) + EPS) * gk_ref[...]).astype(BF16)
    x = sq_ref[...].astype(F32)
    sqo_ref[...] = (x * lax.rsqrt(_group_mean_sq(x, bd128_ref[...], DSA_DIM) + EPS) * gsq_ref[...]
                    * (DSA_DIM ** -0.5)).astype(BF16)
    x = sk_ref[...].astype(F32)
    sko_ref[...] = _rms(x, gsk_ref[...]).astype(BF16)


def _odd_prep(proj, gq, gk, gsq, gsk, bd64, bd128, batch, seq):
    n = proj.shape[0]
    w = 512
    nt = seq // TL_PREP
    kt = TL_PREP // TK
    full = lambda shape: pl.BlockSpec(shape, lambda b, t: (0,) * len(shape))
    row = lambda b, t: b * nt + t
    return pl.pallas_call(
        _odd_prep_body,
        grid=(batch, nt),
        in_specs=[
            pl.BlockSpec((TL_PREP, w), lambda b, t: (row(b, t), 0)),
            pl.BlockSpec((TL_PREP, w), lambda b, t: (row(b, t), 1)),
            pl.BlockSpec((TL_PREP, w), lambda b, t: (row(b, t), 2)),
            pl.BlockSpec((TL_PREP, w), lambda b, t: (row(b, t), 3)),
            pl.BlockSpec((TL_PREP, LANES), lambda b, t: (row(b, t), 5 * w // LANES)),
            pl.BlockSpec((TL_PREP, LANES), lambda b, t: (row(b, t), 5 * w // LANES + 1)),
            full((1, w)), full((1, w)), full((1, w)), full((1, LANES)), full((w, w)), full((w, w)),
        ],
        out_specs=[
            pl.BlockSpec((TL_PREP, w), lambda b, t: (row(b, t), 0)),
            pl.BlockSpec((TL_PREP, w), lambda b, t: (row(b, t), 0)),
            pl.BlockSpec((TL_PREP, w), lambda b, t: (row(b, t), 0)),
            pl.BlockSpec((TL_PREP, LANES), lambda b, t: (row(b, t), 0)),
            pl.BlockSpec((None, DIFF_HEADS + 1, kt, LANES, TK), lambda b, t: (b, 0, t, 0, 0)),
        ],
        out_shape=[
            jax.ShapeDtypeStruct((n, w), BF16), jax.ShapeDtypeStruct((n, w), BF16),
            jax.ShapeDtypeStruct((n, w), BF16), jax.ShapeDtypeStruct((n, LANES), BF16),
            jax.ShapeDtypeStruct((batch, DIFF_HEADS + 1, seq // TK, LANES, TK), BF16),
        ],
        compiler_params=_cparams(2),
        name="odd_prep",
    )(proj, proj, proj, proj, proj, proj, gq, gk, gsq, gsk, bd64, bd128)


def _softmax_step_t(s_t, vt, m_ref, l_ref, acc_ref):
    for g in range(s_t.shape[1] // LANES):
        sl = slice(g * LANES, (g + 1) * LANES)
        s = s_t[:, sl]
        m_prev = m_ref[:, sl]
        m_new = jnp.maximum(m_prev, jnp.max(s, axis=0, keepdims=True))
        alpha = jnp.exp(m_prev - m_new)
        p = jnp.exp(s - m_new)
        l_ref[:, sl] = alpha * l_ref[:, sl] + jnp.sum(p, axis=0, keepdims=True)
        acc_ref[:, sl] = alpha * acc_ref[:, sl] + _dot(vt, p.astype(BF16))
        m_ref[:, sl] = m_new


def _diff_attn_body(far_ref, q_ref, k_ref, vt_ref, bias_ref, lam_ref, sg_ref, o_ref,
                    qt_ref, m_ref, l_ref, acc_ref, *, lambda_init):
    hd = pl.program_id(1)
    qi = pl.program_id(2)
    qt = q_ref[...].astype(F32).T
    dim = lax.broadcasted_iota(I32, qt.shape, 0)
    qt_ref[:, :TQ] = jnp.where(dim < DIFF_QK, qt, 0.0).astype(BF16)
    qt_ref[:, TQ:] = jnp.where(dim >= DIFF_QK, qt, 0.0).astype(BF16)
    m_ref[...] = jnp.full_like(m_ref, NEG)
    l_ref[...] = jnp.zeros_like(l_ref)
    acc_ref[...] = jnp.zeros_like(acc_ref)
    far = far_ref[hd]

    def logits(kb):
        r0 = pl.multiple_of(kb * TK, TK)
        return _dot(k_ref[pl.ds(r0, TK), :], qt_ref[...])

    def far_block(kb, carry):
        _softmax_step_t(logits(kb) + far, vt_ref[kb], m_ref, l_ref, acc_ref)
        return carry

    lax.fori_loop(0, jnp.maximum(qi - 1, 0), far_block, 0)

    @pl.when(qi > 0)
    def _():
        b = bias_ref[:TK, :]
        _softmax_step_t(logits(qi - 1) + jnp.concatenate([b, b], axis=1), vt_ref[qi - 1], m_ref, l_ref, acc_ref)

    b = bias_ref[TK:, :]
    key = lax.broadcasted_iota(I32, (TK, 2 * TQ), 0)
    qry = lax.broadcasted_iota(I32, (TK, 2 * TQ), 1)
    causal = key <= jnp.where(qry >= TQ, qry - TQ, qry)
    s = logits(qi) + jnp.concatenate([b, b], axis=1)
    _softmax_step_t(jnp.where(causal, s, NEG), vt_ref[qi], m_ref, l_ref, acc_ref)

    lf = lam_ref[...]
    lam = (jnp.exp(jnp.sum(lf[0:1] * lf[1:2], axis=-1, keepdims=True))
           - jnp.exp(jnp.sum(lf[2:3] * lf[3:4], axis=-1, keepdims=True)) + lambda_init)
    ot = acc_ref[:, :TQ] / l_ref[:, :TQ] - lam * (acc_ref[:, TQ:] / l_ref[:, TQ:])
    ot = ot * lax.rsqrt(jnp.mean(ot * ot, axis=0, keepdims=True) + EPS)
    o_ref[...] = (ot.T * sg_ref[...] * (1.0 - lambda_init)).astype(BF16)


def _diff_attn(far, dqn, dkn, vt, bias, lam_p, sub_g, batch, seq, lambda_init):
    n = dqn.shape[0]
    nq = seq // TQ
    return pl.pallas_call(
        functools.partial(_diff_attn_body, lambda_init=lambda_init),
        grid=(batch, DIFF_HEADS, nq),
        in_specs=[
            pl.BlockSpec(memory_space=pltpu.SMEM),
            pl.BlockSpec((TQ, LANES), lambda b, h, q: (b * nq + q, h)),
            pl.BlockSpec((seq, LANES), lambda b, h, q: (b, h)),
            pl.BlockSpec((None, None, seq // TK, LANES, TK), lambda b, h, q: (b, h, 0, 0, 0)),
            pl.BlockSpec((None, 2 * TK, TQ), lambda b, h, q: (h, 0, 0)),
            pl.BlockSpec((4, DIFF_QK), lambda b, h, q: (0, 0)),
            pl.BlockSpec((1, LANES), lambda b, h, q: (0, 0)),
        ],
        out_specs=pl.BlockSpec((TQ, LANES), lambda b, h, q: (b * nq + q, h)),
        out_shape=jax.ShapeDtypeStruct((n, DIFF_HEADS * LANES), BF16),
        scratch_shapes=[
            pltpu.VMEM((LANES, 2 * TQ), BF16),
            pltpu.VMEM((1, 2 * TQ), F32),
            pltpu.VMEM((1, 2 * TQ), F32),
            pltpu.VMEM((LANES, 2 * TQ), F32),
        ],
        compiler_params=_cparams(3),
        name="diff_attn",
    )(far, dqn, dkn, vt, bias, lam_p, sub_g)


def _dsa_body(far_ref, q_ref, k_ref, vt_ref, iq_ref, ik_ref, iw_ref, bias_ref, o_ref,
              qt_ref, qit_ref, wt_ref, keys_ref, t_ref, m_ref, l_ref, acc_ref, *, top_k, pos_bits):
    qb = pl.program_id(1)
    nkb = qb + 1

    for p in range(IDX_HEADS // 2):
        gt = iq_ref[:, p * LANES:(p + 1) * LANES].astype(F32).T
        dim = lax.broadcasted_iota(I32, gt.shape, 0)
        qit_ref[:, (2 * p) * TQ:(2 * p + 1) * TQ] = jnp.where(dim < IDX_DIM, gt, 0.0).astype(BF16)
        qit_ref[:, (2 * p + 1) * TQ:(2 * p + 2) * TQ] = jnp.where(dim >= IDX_DIM, gt, 0.0).astype(BF16)
    for hd in range(DSA_HEADS):
        qt_ref[:, hd * TQ:(hd + 1) * TQ] = q_ref[:, hd * LANES:(hd + 1) * LANES].astype(F32).T.astype(BF16)
    wscale = (IDX_HEADS ** -0.5) * (IDX_DIM ** -0.5)
    wt_ref[...] = (iw_ref[...] * wscale).T[:IDX_HEADS, :]

    key_i = lax.broadcasted_iota(I32, (TK, TQ), 0)
    qry_i = lax.broadcasted_iota(I32, (TK, TQ), 1)
    causal = key_i <= qry_i

    def index_block(kb, masked):
        r0 = pl.multiple_of(kb * TK, TK)
        ik = ik_ref[pl.ds(r0, TK), :]
        idx = jnp.zeros((TK, TQ), F32)
        for hi in range(IDX_HEADS):
            s = _dot(ik, qit_ref[:, hi * TQ:(hi + 1) * TQ])
            idx = idx + jnp.maximum(s, 0.0) * wt_ref[hi:hi + 1, :]
        idx = jnp.where(idx == 0.0, 0.0, idx)
        bits = pltpu.bitcast(idx, I32)
        key = jnp.where(bits < 0, bits ^ jnp.int32(0x7FFFFFFF), bits)
        if masked:
            key = jnp.where(causal, key, jnp.int32(INT_MIN))
        keys_ref[kb] = key

    def index_loop(kb, carry):
        index_block(kb, False)
        return carry

    lax.fori_loop(0, nkb - 1, index_loop, 0)
    index_block(nkb - 1, True)

    def count(pred):
        def body(kb, acc):
            x = jnp.where(pred(keys_ref[kb], key_i + kb * TK), 1.0, 0.0)
            parts = [x[i * SUBLANES_F32:(i + 1) * SUBLANES_F32] for i in range(TK // SUBLANES_F32)]
            while len(parts) > 1:
                parts = [parts[i] + parts[i + 1] for i in range(0, len(parts), 2)]
            return acc + parts[0]
        acc = lax.fori_loop(0, nkb, body, jnp.zeros((SUBLANES_F32, TQ), F32))
        return jnp.sum(acc, axis=0, keepdims=True)

    kf = float(top_k)

    def count_ge(cand):
        return count(lambda kk, pos: kk >= cand)

    total = (nkb * TK).astype(F32) * jnp.ones((1, TQ), F32)
    c0 = count_ge(jnp.zeros((1, TQ), I32))
    acc0 = c0 >= kf
    t0 = jnp.where(acc0, jnp.int32(0), jnp.int32(INT_MIN))
    cge0 = jnp.where(acc0, c0, total)

    def bit_step(i, carry):
        t, cge = carry
        cand = t | jnp.left_shift(jnp.int32(1), 30 - i)
        cnt = count_ge(cand)
        ok = cnt >= kf
        return jnp.where(ok, cand, t), jnp.where(ok, cnt, cge)

    t, cge = lax.fori_loop(0, 31, bit_step, (t0, cge0))
    t_ref[...] = t

    @pl.when(jnp.max(cge) > kf)
    def _():
        tt = t_ref[...]
        r = kf - count(lambda kk, pos: kk > tt)

        def pos_step(i, pcut):
            cand = pcut | jnp.left_shift(jnp.int32(1), pos_bits - 1 - i)
            tied_before = count(lambda kk, pos: jnp.logical_and(kk == tt, pos < cand))
            return jnp.where(tied_before <= r, cand, pcut)

        pcut = lax.fori_loop(0, pos_bits, pos_step, jnp.zeros((1, TQ), I32))

        def demote(kb, carry):
            kk = keys_ref[kb]
            drop = jnp.logical_and(kk == tt, key_i + kb * TK >= pcut)
            keys_ref[kb] = jnp.where(drop, kk - 1, kk)
            return carry

        lax.fori_loop(0, nkb, demote, 0)

    m_ref[...] = jnp.full_like(m_ref, NEG)
    l_ref[...] = jnp.zeros_like(l_ref)
    acc_ref[...] = jnp.zeros_like(acc_ref)

    def attend(kb, bias_of_head, diagonal):
        r0 = pl.multiple_of(kb * TK, TK)
        sel = keys_ref[kb] >= t_ref[...]
        if diagonal:
            sel = jnp.logical_and(sel, causal)
        s = _dot(k_ref[pl.ds(r0, TK), :], qt_ref[...])
        s = jnp.concatenate(
            [jnp.where(sel, s[:, hd * TQ:(hd + 1) * TQ] + bias_of_head(hd), NEG) for hd in range(DSA_HEADS)], axis=1)
        _softmax_step_t(s, vt_ref[kb], m_ref, l_ref, acc_ref)

    def far_block(kb, carry):
        attend(kb, lambda hd: far_ref[DIFF_HEADS + hd], False)
        return carry

    lax.fori_loop(0, jnp.maximum(qb - 1, 0), far_block, 0)

    @pl.when(qb > 0)
    def _():
        attend(qb - 1, lambda hd: bias_ref[hd, :TK, :], False)

    attend(qb, lambda hd: bias_ref[hd, TK:, :], True)

    for hd in range(DSA_HEADS):
        sl = slice(hd * TQ, (hd + 1) * TQ)
        o_ref[:, hd * LANES:(hd + 1) * LANES] = (acc_ref[:, sl] / l_ref[:, sl]).T.astype(BF16)


def _dsa(far, sqn, skn, vt, proj, iw, bias, batch, seq):
    n = sqn.shape[0]
    nq = seq // TQ
    top_k = min(IDX_TOPK_MAX, seq // 4)
    pos_bits = int(seq).bit_length()
    iq_block = 4
    ik_block = (5 * 512 + 2 * LANES) // LANES
    return pl.pallas_call(
        functools.partial(_dsa_body, top_k=top_k, pos_bits=pos_bits),
        grid=(batch, nq),
        in_specs=[
            pl.BlockSpec(memory_space=pltpu.SMEM),
            pl.BlockSpec((TQ, DSA_HEADS * LANES), lambda b, q: (b * nq + q, 0)),
            pl.BlockSpec((seq, LANES), lambda b, q: (b, 0)),
            pl.BlockSpec((None, None, seq // TK, LANES, TK), lambda b, q: (b, DIFF_HEADS, 0, 0, 0)),
            pl.BlockSpec((TQ, IDX_HEADS * IDX_DIM), lambda b, q: (b * nq + q, iq_block)),
            pl.BlockSpec((seq, LANES), lambda b, q: (b, ik_block)),
            pl.BlockSpec((TQ, LANES), lambda b, q: (b * nq + q, 0)),
            pl.BlockSpec((DSA_HEADS, 2 * TK, TQ), lambda b, q: (1, 0, 0)),
        ],
        out_specs=pl.BlockSpec((TQ, DSA_HEADS * LANES), lambda b, q: (b * nq + q, 0)),
        out_shape=jax.ShapeDtypeStruct((n, DSA_HEADS * LANES), BF16),
        scratch_shapes=[
            pltpu.VMEM((LANES, DSA_HEADS * TQ), BF16),
            pltpu.VMEM((LANES, IDX_HEADS * TQ), BF16),
            pltpu.VMEM((IDX_HEADS, TQ), F32),
            pltpu.VMEM((seq // TK, TK, TQ), I32),
            pltpu.VMEM((1, TQ), I32),
            pltpu.VMEM((1, DSA_HEADS * TQ), F32),
            pltpu.VMEM((1, DSA_HEADS * TQ), F32),
            pltpu.VMEM((LANES, DSA_HEADS * TQ), F32),
        ],
        compiler_params=_cparams(2),
        name="dsa",
    )(far, sqn, skn, vt, proj, proj, iw, bias)


def _rel_bucket(dist):
    exact = REL_BUCKETS // 2
    n = jnp.maximum(dist, 0)
    nf = jnp.maximum(n, exact).astype(F32)
    far = exact + (jnp.log(nf / exact) / math.log(REL_MAX_DIST / exact) * (REL_BUCKETS - exact)).astype(I32)
    return jnp.where(n < exact, n, jnp.minimum(far, REL_BUCKETS - 1))


def _bias_tiles_body(tab_ref, bucket_ref, o_ref):
    hd = pl.program_id(0)
    nh = DIFF_HEADS + DSA_HEADS
    bucket = bucket_ref[...]
    acc = jnp.zeros(bucket.shape, F32)
    for b in range(REL_BUCKETS):
        acc = jnp.where(bucket == b, tab_ref[b * nh + hd], acc)
    o_ref[...] = acc


def _bias_tables(rel_bias):
    nh = DIFF_HEADS + DSA_HEADS
    far_bucket = REL_BUCKETS - 1
    assert TK + 1 >= REL_MAX_DIST
    j = jnp.arange(2 * TK, dtype=I32)[:, None]
    i = jnp.arange(TQ, dtype=I32)[None, :]
    bucket = _rel_bucket(i - j + TK)
    near = pl.pallas_call(
        _bias_tiles_body,
        grid=(nh,),
        in_specs=[pl.BlockSpec(memory_space=pltpu.SMEM), pl.BlockSpec((2 * TK, TQ), lambda h: (0, 0))],
        out_specs=pl.BlockSpec((None, 2 * TK, TQ), lambda h: (h, 0, 0)),
        out_shape=jax.ShapeDtypeStruct((nh, 2 * TK, TQ), F32),
        compiler_params=_cparams(1),
        name="bias_tiles",
    )(rel_bias.astype(F32).reshape(-1), bucket)
    return near, rel_bias[far_bucket].astype(F32)


def _block_diag_ones(width, group):
    r = np.arange(width)
    return jnp.asarray((r[:, None] // group) == (r[None, :] // group), dtype=BF16)


def _pad_lanes(w, width=LANES):
    return jnp.pad(w, ((0, 0), (0, width - w.shape[1])))


def kernel(x, rel_bias, mix_norm_g, ev_w_in, ev_w_out, gmlp_w_s, gmlp_b_s, gdn_conv_w, gdn_a_log, gdn_dt_bias,
           gdn_norm_g, od_w_in, od_w_out, diff_q_norm_g, diff_k_norm_g, diff_lambda, diff_sub_norm_g,
           dsa_q_norm_g, dsa_k_norm_g, ffn_norm_g, ffn_w_up, ffn_conv_w, ffn_conv_b, ffn_w_down):
    batch, seq, d = x.shape
    n = batch * seq
    depth = mix_norm_g.shape[0]
    assert d == D_MODEL and seq % max(TM_PROJ, TL_PREP, TL_GDN, TQ) == 0
    h = x.reshape(n, d)
    bias_near, bias_far = _bias_tables(rel_bias)
    bd64 = _block_diag_ones(512, DIFF_QK)
    bd128 = _block_diag_ones(512, DSA_DIM)
    gw = GMLP_GROUPS * LANES
    qkv_w = 3 * GDN_HEADS * GDN_DIM

    for layer in range(depth):
        j = layer // 2
        g_mix = mix_norm_g[layer].reshape(1, d)
        if layer % 2 == 0:
            w = ev_w_in[j]
            o_u, o_v, o_qkv = 0, gw, 2 * gw
            o_b = o_qkv + qkv_w
            o_a = o_b + GDN_HEADS
            o_z = o_a + GDN_HEADS
            w_main = jnp.concatenate(
                [w[:, o_u:o_v], w[:, o_v:o_qkv], w[:, o_z:o_z + GDN_HEADS * GDN_DIM], w[:, o_qkv:o_b]],
                axis=1).astype(BF16)
            w_gate = _pad_lanes(w[:, o_b:o_z]).astype(BF16)
            proj, gates = _in_proj(h, g_mix, w_main, w_gate)
            gp = jnp.stack([
                _pad_lanes(jnp.concatenate([jnp.zeros((GDN_HEADS,), F32), gdn_a_log[j]])[None])[0],
                _pad_lanes(jnp.concatenate([jnp.zeros((GDN_HEADS,), F32), gdn_dt_bias[j]])[None])[0]])
            y_a, qkvn, gates2 = _even_prep(proj, gates, gmlp_w_s[j], gmlp_b_s[j].T, gdn_conv_w[j], gp, batch, seq)
            y_b = _gdn(qkvn, gates2, proj, gdn_norm_g[j].reshape(1, GDN_DIM), batch, seq)
            h = _out_proj(h, y_a, y_b, ev_w_out[j].astype(BF16))
        else:
            lambda_init = 0.8 - 0.6 * math.exp(-0.3 * layer)
            w = od_w_in[j]
            c = np.cumsum([0, 512, 512, 512, 512, 128, 128, 512, 64, 8])
            dq, dk, dv, sq, sk, sv, iq, ik, iw = [w[:, c[i]:c[i + 1]] for i in range(9)]
            w_main = jnp.concatenate([dq, dk, dv, sq, iq, sk, sv, ik, ik], axis=1).astype(BF16)
            w_gate = _pad_lanes(iw).astype(BF16)
            proj, iw_out = _in_proj(h, g_mix, w_main, w_gate)
            dqn, dkn, sqn, skn, vt = _odd_prep(
                proj,
                jnp.tile(diff_q_norm_g[j], 2 * DIFF_HEADS)[None], jnp.tile(diff_k_norm_g[j], 2 * DIFF_HEADS)[None],
                jnp.tile(dsa_q_norm_g[j], DSA_HEADS)[None], dsa_k_norm_g[j][None], bd64, bd128, batch, seq)
            y_c = _diff_attn(bias_far, dqn, dkn, vt, bias_near, diff_lambda[j], diff_sub_norm_g[j][None],
                             batch, seq, lambda_init)
            y_d = _dsa(bias_far, sqn, skn, vt, proj, iw_out, bias_near, batch, seq)
            h = _out_proj(h, y_c, y_d, od_w_out[j].astype(BF16))
        h = _ffn(h, ffn_norm_g[layer].reshape(1, d), ffn_w_up[layer].astype(BF16), ffn_conv_w[layer],
                 ffn_conv_b[layer].reshape(1, 2 * D_FF), ffn_w_down[layer].astype(BF16), seq)
    return h.reshape(batch, seq, d)
```

```python
import functools
import math

import numpy as np
import jax
import jax.numpy as jnp
from jax import lax
from jax.experimental import pallas as pl
from jax.experimental.pallas import tpu as pltpu

F32 = jnp.float32
BF16 = jnp.bfloat16
I32 = jnp.int32
HIGHEST = lax.Precision.HIGHEST

D_MODEL = 1024
GMLP_GROUPS = 4
GMLP_CHUNK = 128
GDN_HEADS = 4
GDN_DIM = 128
GDN_CHUNK = 64
GDN_CONV = 4
DIFF_HEADS = 4
DIFF_QK = 64
DSA_HEADS = 4
DSA_DIM = 128
IDX_HEADS = 8
IDX_DIM = 64
IDX_TOPK_MAX = 256
REL_BUCKETS = 32
REL_MAX_DIST = 128
D_FF = 2816
FFN_CONV = 3
EPS = 1e-6

LANES = 128
SUBLANES_F32 = 8
SUBLANES_BF16 = 16
VMEM_LIMIT = 56 * 1024 * 1024

TM_PROJ = 512
TL_PREP = 512
TL_GDN = 512
GDN_PREP_CHUNKS = 2
TQ = 256
TK = 256
FF_CHUNK = 256
NEG = -1e30
LOG2E = 1.0 / math.log(2.0)
INT_MIN = -2 ** 31


def _cparams(n_axes):
    return pltpu.CompilerParams(dimension_semantics=("arbitrary",) * n_axes, vmem_limit_bytes=VMEM_LIMIT)


def _dot(a, b, **kw):
    return jnp.dot(a, b, preferred_element_type=F32, **kw)


def _dot_nt(a, b, **kw):
    return lax.dot_general(a, b, (((1,), (1,)), ((), ())), preferred_element_type=F32, **kw)


def _dot_tn(a, b, **kw):
    return lax.dot_general(a, b, (((0,), (0,)), ((), ())), preferred_element_type=F32, **kw)


def _rms(x, g):
    return x * lax.rsqrt(jnp.mean(x * x, axis=-1, keepdims=True) + EPS) * g


def _sigmoid(x):
    return 1.0 / (1.0 + jnp.exp(-x))


def _silu(x):
    return x * _sigmoid(x)


def _gelu_tanh(x):
    return 0.5 * x * (1.0 + jnp.tanh(math.sqrt(2.0 / math.pi) * (x + 0.044715 * (x * x * x))))


def _softplus(x):
    return jnp.maximum(x, 0.0) + jnp.log(1.0 + jnp.exp(-jnp.abs(x)))


def _in_proj_body(h_ref, g_ref, w_ref, wg_ref, o_ref, og_ref, *, nout):
    xn = _rms(h_ref[...], g_ref[...]).astype(BF16)
    for c in range(0, nout, 512):
        e = min(c + 512, nout)
        o_ref[:, c:e] = _dot(xn, w_ref[:, c:e]).astype(BF16)
    og_ref[...] = _dot(xn, wg_ref[...])


def _in_proj(h, g, w, wg):
    n, d = h.shape
    nout = w.shape[1]
    return pl.pallas_call(
        functools.partial(_in_proj_body, nout=nout),
        grid=(n // TM_PROJ,),
        in_specs=[
            pl.BlockSpec((TM_PROJ, d), lambda i: (i, 0)),
            pl.BlockSpec((1, d), lambda i: (0, 0)),
            pl.BlockSpec((d, nout), lambda i: (0, 0)),
            pl.BlockSpec((d, LANES), lambda i: (0, 0)),
        ],
        out_specs=[
            pl.BlockSpec((TM_PROJ, nout), lambda i: (i, 0)),
            pl.BlockSpec((TM_PROJ, LANES), lambda i: (i, 0)),
        ],
        out_shape=[jax.ShapeDtypeStruct((n, nout), BF16), jax.ShapeDtypeStruct((n, LANES), F32)],
        compiler_params=_cparams(1),
        name="in_proj",
    )(h, g, w, wg)


def _out_proj_body(h_ref, y1_ref, y2_ref, w_ref, o_ref):
    half = y1_ref.shape[1]
    o_ref[...] = h_ref[...] + _dot(y1_ref[...], w_ref[:half, :]) + _dot(y2_ref[...], w_ref[half:, :])


def _out_proj(h, y1, y2, w):
    n, d = h.shape
    half = y1.shape[1]
    return pl.pallas_call(
        _out_proj_body,
        grid=(n // TM_PROJ,),
        in_specs=[
            pl.BlockSpec((TM_PROJ, d), lambda i: (i, 0)),
            pl.BlockSpec((TM_PROJ, half), lambda i: (i, 0)),
            pl.BlockSpec((TM_PROJ, half), lambda i: (i, 0)),
            pl.BlockSpec((2 * half, d), lambda i: (0, 0)),
        ],
        out_specs=pl.BlockSpec((TM_PROJ, d), lambda i: (i, 0)),
        out_shape=jax.ShapeDtypeStruct((n, d), F32),
        compiler_params=_cparams(1),
        name="out_proj",
    )(h, y1, y2, w)


def _ffn_body(h_ref, halo_ref, g_ref, wup_ref, cw_ref, cb_ref, wdn_ref, o_ref, xn_ref, *, tiles_per_seq):
    tm = h_ref.shape[0]
    i = pl.program_id(0)
    h = h_ref[...]
    g = g_ref[...]
    hb = SUBLANES_BF16
    xn_ref[hb:, :] = _rms(h, g).astype(BF16)
    keep = jnp.where(i % tiles_per_seq == 0, 0.0, 1.0)
    xn_ref[:hb, :] = (_rms(halo_ref[...], g) * keep).astype(BF16)
    xn = xn_ref[...]
    acc = h
    for c in range(0, D_FF, FF_CHUNK):
        parts = []
        for base in (c, D_FF + c):
            up = _dot(xn, wup_ref[:, base:base + FF_CHUNK])
            cw = cw_ref[:, base:base + FF_CHUNK]
            y = (cw[0:1] * up[hb - 2:tm + hb - 2]
                 + cw[1:2] * up[hb - 1:tm + hb - 1]
                 + cw[2:3] * up[hb:tm + hb]
                 + cb_ref[:, base:base + FF_CHUNK])
            parts.append(y)
        act = (_silu(parts[0]) * parts[1]).astype(BF16)
        acc = acc + _dot(act, wdn_ref[c:c + FF_CHUNK, :])
    o_ref[...] = acc


def _ffn(h, g, wup, cw, cb, wdn, seq):
    n, d = h.shape
    tiles_per_seq = seq // TM_PROJ
    halo_blocks = TM_PROJ // SUBLANES_BF16
    const = dict(pipeline_mode=pl.Buffered(1))
    return pl.pallas_call(
        functools.partial(_ffn_body, tiles_per_seq=tiles_per_seq),
        grid=(n // TM_PROJ,),
        in_specs=[
            pl.BlockSpec((TM_PROJ, d), lambda i: (i, 0)),
            pl.BlockSpec((SUBLANES_BF16, d), lambda i: (jnp.maximum(i * halo_blocks - 1, 0), 0)),
            pl.BlockSpec((1, d), lambda i: (0, 0)),
            pl.BlockSpec((d, 2 * D_FF), lambda i: (0, 0), **const),
            pl.BlockSpec((FFN_CONV, 2 * D_FF), lambda i: (0, 0)),
            pl.BlockSpec((1, 2 * D_FF), lambda i: (0, 0)),
            pl.BlockSpec((D_FF, d), lambda i: (0, 0), **const),
        ],
        out_specs=pl.BlockSpec((TM_PROJ, d), lambda i: (i, 0)),
        out_shape=jax.ShapeDtypeStruct((n, d), F32),
        scratch_shapes=[pltpu.VMEM((TM_PROJ + SUBLANES_BF16, d), BF16)],
        compiler_params=_cparams(1),
        name="ffn",
    )(h, h, g, wup, cw, cb, wdn)


def _even_prep_body(uv_ref, qkv_ref, halo_ref, gate_ref, ws_ref, bs_ref, cw_ref, gp_ref,
                    ya_ref, qkvo_ref, gout_ref):
    tl = uv_ref.shape[0]
    t = pl.program_id(1)
    gw = LANES
    row = lax.broadcasted_iota(I32, (GMLP_CHUNK, GMLP_CHUNK), 0)
    col = lax.broadcasted_iota(I32, (GMLP_CHUNK, GMLP_CHUNK), 1)
    for gi in range(GMLP_GROUPS):
        u = _gelu_tanh(uv_ref[:, gi * gw:(gi + 1) * gw].astype(F32))
        v = _gelu_tanh(uv_ref[:, (GMLP_GROUPS + gi) * gw:(GMLP_GROUPS + gi + 1) * gw].astype(F32))
        vc = v - jnp.mean(v, axis=-1, keepdims=True)
        vn = (vc * lax.rsqrt(jnp.mean(vc * vc, axis=-1, keepdims=True) + EPS)).astype(BF16)
        w = jnp.where(col <= row, ws_ref[gi], 0.0).astype(BF16)
        b = bs_ref[:, gi:gi + 1]
        for c in range(tl // GMLP_CHUNK):
            r0 = c * GMLP_CHUNK
            mixed = _dot(w, vn[r0:r0 + GMLP_CHUNK]) + b
            ya_ref[r0:r0 + GMLP_CHUNK, gi * gw:(gi + 1) * gw] = (u[r0:r0 + GMLP_CHUNK] * mixed).astype(BF16)
    keep = jnp.where(t == 0, 0.0, 1.0)
    hb = SUBLANES_BF16
    for j in range(3 * GDN_HEADS):
        sl = slice(j * gw, (j + 1) * gw)
        x = jnp.concatenate([halo_ref[:, sl].astype(F32) * keep, qkv_ref[:, sl].astype(F32)], axis=0)
        cw = cw_ref[:, sl]
        y = cw[0:1] * x[hb - 3:tl + hb - 3]
        for k in range(1, GDN_CONV):
            y = y + cw[k:k + 1] * x[hb - 3 + k:tl + hb - 3 + k]
        y = _silu(y)
        if j < 2 * GDN_HEADS:
            y = y * lax.rsqrt(jnp.sum(y * y, axis=-1, keepdims=True) + EPS)
        if j < GDN_HEADS:
            y = y * (GDN_DIM ** -0.5)
        qkvo_ref[:, sl] = y.astype(BF16)
    x = gate_ref[...]
    lane = lax.broadcasted_iota(I32, x.shape, 1)
    beta = _sigmoid(x)
    gdec = -jnp.exp(gp_ref[0:1, :]) * _softplus(x + gp_ref[1:2, :])
    gout_ref[...] = jnp.where(lane < GDN_HEADS, beta, gdec)


def _even_prep(proj, gates, ws, bs_t, cw, gp, batch, seq):
    n = proj.shape[0]
    nt = seq // TL_PREP
    halo_blocks = TL_PREP // SUBLANES_BF16
    qkv_w = 3 * GDN_HEADS * GDN_DIM
    row = lambda b, t: b * nt + t
    return pl.pallas_call(
        _even_prep_body,
        grid=(batch, nt),
        in_specs=[
            pl.BlockSpec((TL_PREP, 2 * GMLP_GROUPS * LANES), lambda b, t: (row(b, t), 0)),
            pl.BlockSpec((TL_PREP, qkv_w), lambda b, t: (row(b, t), 1)),
            pl.BlockSpec((SUBLANES_BF16, qkv_w), lambda b, t: (jnp.maximum(row(b, t) * halo_blocks - 1, 0), 1)),
            pl.BlockSpec((TL_PREP, LANES), lambda b, t: (row(b, t), 0)),
            pl.BlockSpec((GMLP_GROUPS, GMLP_CHUNK, GMLP_CHUNK), lambda b, t: (0, 0, 0)),
            pl.BlockSpec((GMLP_CHUNK, GMLP_GROUPS), lambda b, t: (0, 0)),
            pl.BlockSpec((GDN_CONV, qkv_w), lambda b, t: (0, 0)),
            pl.BlockSpec((2, LANES), lambda b, t: (0, 0)),
        ],
        out_specs=[
            pl.BlockSpec((TL_PREP, GMLP_GROUPS * LANES), lambda b, t: (row(b, t), 0)),
            pl.BlockSpec((TL_PREP, qkv_w), lambda b, t: (row(b, t), 0)),
            pl.BlockSpec((TL_PREP, LANES), lambda b, t: (row(b, t), 0)),
        ],
        out_shape=[
            jax.ShapeDtypeStruct((n, GMLP_GROUPS * LANES), BF16),
            jax.ShapeDtypeStruct((n, qkv_w), BF16),
            jax.ShapeDtypeStruct((n, LANES), F32),
        ],
        compiler_params=_cparams(2),
        name="even_prep",
    )(proj, proj, proj, gates, ws, bs_t, cw, gp)


def _split_bf16(x):
    hi = x.astype(BF16)
    return hi, (x - hi.astype(F32)).astype(BF16)


def _mm3(a, b):
    ah, al = _split_bf16(a)
    bh, bl = _split_bf16(b)
    return _dot(ah, bh) + (_dot(ah, bl) + _dot(al, bh))


def _unit_lower_inverses(lows):
    c = lows[0].shape[0]
    row = lax.broadcasted_iota(I32, (c, c), 0)
    col = lax.broadcasted_iota(I32, (c, c), 1)
    eye = jnp.where(row == col, 1.0, 0.0)
    same16 = jnp.right_shift(row, 4) == jnp.right_shift(col, 4)
    same32 = jnp.right_shift(row, 5) == jnp.right_shift(col, 5)
    ps = [jnp.where(same16, -low, 0.0) for low in lows]
    xs = [eye + p for p in ps]
    for _ in range(3):
        ps = [_mm3(p, p) for p in ps]
        xs = [x + _mm3(x, p) for x, p in zip(xs, ps)]
    mid = jnp.logical_and(same32, jnp.logical_not(same16))
    mm1 = lambda a, b: _dot(a.astype(BF16), b.astype(BF16))
    ys = [mm1(jnp.where(mid, low, 0.0), x) for low, x in zip(lows, xs)]
    xs = [x - mm1(x, y) for x, y in zip(xs, ys)]
    ys = [mm1(jnp.where(same32, 0.0, low), x) for low, x in zip(lows, xs)]
    return [x - mm1(x, y) for x, y in zip(xs, ys)]


def _gdn_body(qkv_ref, gate_ref, z_ref, gn_ref, o_ref,
              s_ref, u_ref, w_ref, qg_ref, kdt_ref, intra_ref, egl_ref):
    tl = qkv_ref.shape[0]
    c = GDN_CHUNK
    d = GDN_DIM
    heads = range(GDN_HEADS)
    t = pl.program_id(1)

    @pl.when(t == 0)
    def _():
        s_ref[...] = jnp.zeros_like(s_ref)

    row = lax.broadcasted_iota(I32, (c, c), 0)
    col = lax.broadcasted_iota(I32, (c, c), 1)
    incl = col <= row
    strict = col < row
    lmat = jnp.where(incl, 1.0, 0.0)
    gn = gn_ref[...]

    def prepare(it, carry):
        cis = [it * GDN_PREP_CHUNKS + j for j in range(GDN_PREP_CHUNKS)]
        rows = [pl.ds(pl.multiple_of(ci * c, c), c) for ci in cis]
        gates = [gate_ref[r, :] for r in rows]
        gcum = [_dot(lmat, g, precision=HIGHEST) for g in gates]
        pairs = [(j, hd) for j in range(GDN_PREP_CHUNKS) for hd in heads]
        qs = [qkv_ref[rows[j], hd * d:(hd + 1) * d] for j, hd in pairs]
        ks = [qkv_ref[rows[j], (GDN_HEADS + hd) * d:(GDN_HEADS + hd + 1) * d] for j, hd in pairs]
        vs = [qkv_ref[rows[j], (2 * GDN_HEADS + hd) * d:(2 * GDN_HEADS + hd + 1) * d] for j, hd in pairs]
        beta = [jnp.broadcast_to(gates[j][:, hd:hd + 1], (c, d)) for j, hd in pairs]
        gc = [jnp.broadcast_to(gcum[j][:, GDN_HEADS + hd:GDN_HEADS + hd + 1], (c, d)) for j, hd in pairs]
        decay = [jnp.where(incl, jnp.exp(jnp.where(incl, g[:, :c] - g.T[:c, :], 0.0)), 0.0) for g in gc]
        kb = [k.astype(F32) * b for k, b in zip(ks, beta)]
        lows = [jnp.where(strict, _dot_nt(x.astype(BF16), k) * dc, 0.0) for x, k, dc in zip(kb, ks, decay)]
        tinv = [x.astype(BF16) for x in _unit_lower_inverses(lows)]
        egc = [jnp.exp(g) for g in gc]
        rhs = [jnp.concatenate([(v.astype(F32) * b).astype(BF16), (x * e).astype(BF16)], axis=1)
               for v, b, x, e in zip(vs, beta, kb, egc)]
        uw = [_dot(ti, r) for ti, r in zip(tinv, rhs)]
        qk = [_dot_nt(q, k) for q, k in zip(qs, ks)]
        for n, (j, hd) in enumerate(pairs):
            ci = cis[j]
            u_ref[ci, hd] = uw[n][:, :d]
            w_ref[ci, hd] = uw[n][:, d:].astype(BF16)
            intra_ref[ci, hd] = jnp.where(incl, qk[n] * decay[n], 0.0).astype(BF16)
            qg_ref[ci, hd] = (qs[n].astype(F32) * egc[n]).astype(BF16)
            g_last = gc[n][c - 1:c, :]
            kdt_ref[ci, hd] = (ks[n].astype(F32) * jnp.exp(g_last - gc[n])).T.astype(BF16)
            egl_ref[ci, hd] = jnp.exp(g_last)
        return carry

    lax.fori_loop(0, tl // c // GDN_PREP_CHUNKS, prepare, 0)

    def scan(ci, carry):
        r0 = pl.multiple_of(ci * c, c)
        states = [s_ref[hd] for hd in heads]
        sb = [s.astype(BF16) for s in states]
        ws = [_dot(w_ref[ci, hd], sb[hd]) for hd in heads]
        qsd = [_dot(qg_ref[ci, hd], sb[hd]) for hd in heads]
        vnb = [(u_ref[ci, hd] - ws[hd]).astype(BF16) for hd in heads]
        for hd in heads:
            s_ref[hd] = states[hd] * egl_ref[ci, hd] + _dot(kdt_ref[ci, hd], vnb[hd])
        for hd in heads:
            out = qsd[hd] + _dot(intra_ref[ci, hd], vnb[hd])
            z = z_ref[pl.ds(r0, c), hd * d:(hd + 1) * d].astype(F32)
            o_ref[pl.ds(r0, c), hd * d:(hd + 1) * d] = (_rms(out, gn) * _silu(z)).astype(BF16)
        return carry

    lax.fori_loop(0, tl // c, scan, 0)


def _gdn(qkv, gates, proj, gn, batch, seq):
    n = qkv.shape[0]
    nt = seq // TL_GDN
    nc = TL_GDN // GDN_CHUNK
    width = GDN_HEADS * GDN_DIM
    row = lambda b, t: b * nt + t
    return pl.pallas_call(
        _gdn_body,
        grid=(batch, nt),
        in_specs=[
            pl.BlockSpec((TL_GDN, 3 * width), lambda b, t: (row(b, t), 0)),
            pl.BlockSpec((TL_GDN, LANES), lambda b, t: (row(b, t), 0)),
            pl.BlockSpec((TL_GDN, width), lambda b, t: (row(b, t), 2)),
            pl.BlockSpec((1, GDN_DIM), lambda b, t: (0, 0)),
        ],
        out_specs=pl.BlockSpec((TL_GDN, width), lambda b, t: (row(b, t), 0)),
        out_shape=jax.ShapeDtypeStruct((n, width), BF16),
        scratch_shapes=[
            pltpu.VMEM((GDN_HEADS, GDN_DIM, GDN_DIM), F32),
            pltpu.VMEM((nc, GDN_HEADS, GDN_CHUNK, GDN_DIM), F32),
            pltpu.VMEM((nc, GDN_HEADS, GDN_CHUNK, GDN_DIM), BF16),
            pltpu.VMEM((nc, GDN_HEADS, GDN_CHUNK, GDN_DIM), BF16),
            pltpu.VMEM((nc, GDN_HEADS, GDN_DIM, GDN_CHUNK), BF16),
            pltpu.VMEM((nc, GDN_HEADS, GDN_CHUNK, GDN_CHUNK), BF16),
            pltpu.VMEM((nc, GDN_HEADS, 1, GDN_DIM), F32),
        ],
        compiler_params=_cparams(2),
        name="gdn",
    )(qkv, gates, proj, gn)


def _group_mean_sq(x, ones_bd, group):
    x2 = x * x
    hi = x2.astype(BF16)
    lo = (x2 - hi.astype(F32)).astype(BF16)
    return (_dot(hi, ones_bd) + _dot(lo, ones_bd)) * (1.0 / group)


def _odd_prep_body(dq_ref, dk_ref, dv_ref, sq_ref, sk_ref, sv_ref, gq_ref, gk_ref, gsq_ref, gsk_ref, bd64_ref,
                   bd128_ref, dqo_ref, dko_ref, sqo_ref, sko_ref, vt_ref):
    for j in range(DIFF_HEADS + 1):
        src = sv_ref[...] if j == DIFF_HEADS else dv_ref[:, j * LANES:(j + 1) * LANES]
        xt = src.astype(F32).T
        for c in range(xt.shape[1] // TK):
            vt_ref[j, c] = xt[:, c * TK:(c + 1) * TK].astype(BF16)
    bd64 = bd64_ref[...]
    x = dq_ref[...].astype(F32)
    dqo_ref[...] = (x * lax.rsqrt(_group_mean_sq(x, bd64, DIFF_QK) + EPS) * gq_ref[...]
                    * (DIFF_QK ** -0.5 * LOG2E)).astype(BF16)
    x = dk_ref[...].astype(F32)
    dko_ref[...] = (x * lax.rsqrt(_group_mean_sq(x, bd64, DIFF_QK) + EPS) * gk_ref[...]).astype(BF16)
    x = sq_ref[...].astype(F32)
    sqo_ref[...] = (x * lax.rsqrt(_group_mean_sq(x, bd128_ref[...], DSA_DIM) + EPS) * gsq_ref[...]
                    * (DSA_DIM ** -0.5 * LOG2E)).astype(BF16)
    x = sk_ref[...].astype(F32)
    sko_ref[...] = _rms(x, gsk_ref[...]).astype(BF16)


def _odd_prep(proj, gq, gk, gsq, gsk, bd64, bd128, batch, seq):
    n = proj.shape[0]
    w = 512
    nt = seq // TL_PREP
    kt = TL_PREP // TK
    full = lambda shape: pl.BlockSpec(shape, lambda b, t: (0,) * len(shape))
    row = lambda b, t: b * nt + t
    return pl.pallas_call(
        _odd_prep_body,
        grid=(batch, nt),
        in_specs=[
            pl.BlockSpec((TL_PREP, w), lambda b, t: (row(b, t), 0)),
            pl.BlockSpec((TL_PREP, w), lambda b, t: (row(b, t), 1)),
            pl.BlockSpec((TL_PREP, w), lambda b, t: (row(b, t), 2)),
            pl.BlockSpec((TL_PREP, w), lambda b, t: (row(b, t), 3)),
            pl.BlockSpec((TL_PREP, LANES), lambda b, t: (row(b, t), 5 * w // LANES)),
            pl.BlockSpec((TL_PREP, LANES), lambda b, t: (row(b, t), 5 * w // LANES + 1)),
            full((1, w)), full((1, w)), full((1, w)), full((1, LANES)), full((w, w)), full((w, w)),
        ],
        out_specs=[
            pl.BlockSpec((TL_PREP, w), lambda b, t: (row(b, t), 0)),
            pl.BlockSpec((TL_PREP, w), lambda b, t: (row(b, t), 0)),
            pl.BlockSpec((TL_PREP, w), lambda b, t: (row(b, t), 0)),
            pl.BlockSpec((TL_PREP, LANES), lambda b, t: (row(b, t), 0)),
            pl.BlockSpec((None, DIFF_HEADS + 1, kt, LANES, TK), lambda b, t: (b, 0, t, 0, 0)),
        ],
        out_shape=[
            jax.ShapeDtypeStruct((n, w), BF16), jax.ShapeDtypeStruct((n, w), BF16),
            jax.ShapeDtypeStruct((n, w), BF16), jax.ShapeDtypeStruct((n, LANES), BF16),
            jax.ShapeDtypeStruct((batch, DIFF_HEADS + 1, seq // TK, LANES, TK), BF16),
        ],
        compiler_params=_cparams(2),
        name="odd_prep",
    )(proj, proj, proj, proj, proj, proj, gq, gk, gsq, gsk, bd64, bd128)


def _bias_tile_index(kb, qb):
    return jnp.clip(kb - (qb - 2), 0, 2)


def _softmax_step_t(s_t, vt, m_ref, l_ref, acc_ref):
    for g in range(s_t.shape[1] // LANES):
        sl = slice(g * LANES, (g + 1) * LANES)
        s = s_t[:, sl]
        m_prev = m_ref[:, sl]
        m_new = jnp.maximum(m_prev, jnp.max(s, axis=0, keepdims=True))
        alpha = jnp.exp2(m_prev - m_new)
        p = jnp.exp2(s - m_new)
        l_ref[:, sl] = alpha * l_ref[:, sl] + jnp.sum(p, axis=0, keepdims=True)
        acc_ref[:, sl] = alpha * acc_ref[:, sl] + _dot(vt, p.astype(BF16))
        m_ref[:, sl] = m_new


def _diff_attn_body(q_ref, k_ref, vt_ref, bias_ref, lam_ref, sg_ref, o_ref,
                    qt_ref, m_ref, l_ref, acc_ref, *, lambda_init):
    qi = pl.program_id(2)
    nblk = qi + 1
    qt = q_ref[...].astype(F32).T
    dim = lax.broadcasted_iota(I32, qt.shape, 0)
    qt_ref[:, :TQ] = jnp.where(dim < DIFF_QK, qt, 0.0).astype(BF16)
    qt_ref[:, TQ:] = jnp.where(dim >= DIFF_QK, qt, 0.0).astype(BF16)
    m_ref[...] = jnp.full_like(m_ref, NEG)
    l_ref[...] = jnp.zeros_like(l_ref)
    acc_ref[...] = jnp.zeros_like(acc_ref)

    def logits(kb):
        r0 = pl.multiple_of(kb * TK, TK)
        b = bias_ref[_bias_tile_index(kb, qi)]
        return _dot(k_ref[pl.ds(r0, TK), :], qt_ref[...]) + jnp.concatenate([b, b], axis=1)

    def block(kb, s_cur):
        s_next = logits(jnp.minimum(kb + 1, nblk - 1))
        _softmax_step_t(s_cur, vt_ref[kb], m_ref, l_ref, acc_ref)
        return s_next

    lax.fori_loop(0, nblk, block, logits(0))

    lf = lam_ref[...]
    lam = (jnp.exp(jnp.sum(lf[0:1] * lf[1:2], axis=-1, keepdims=True))
           - jnp.exp(jnp.sum(lf[2:3] * lf[3:4], axis=-1, keepdims=True)) + lambda_init)
    ot = acc_ref[:, :TQ] / l_ref[:, :TQ] - lam * (acc_ref[:, TQ:] / l_ref[:, TQ:])
    ot = ot * lax.rsqrt(jnp.mean(ot * ot, axis=0, keepdims=True) + EPS)
    o_ref[...] = (ot.T * sg_ref[...] * (1.0 - lambda_init)).astype(BF16)


def _diff_attn(dqn, dkn, vt, bias, lam_p, sub_g, batch, seq, lambda_init):
    n = dqn.shape[0]
    nq = seq // TQ
    return pl.pallas_call(
        functools.partial(_diff_attn_body, lambda_init=lambda_init),
        grid=(batch, DIFF_HEADS, nq),
        in_specs=[
            pl.BlockSpec((TQ, LANES), lambda b, h, q: (b * nq + q, h)),
            pl.BlockSpec((seq, LANES), lambda b, h, q: (b, h)),
            pl.BlockSpec((None, None, seq // TK, LANES, TK), lambda b, h, q: (b, h, 0, 0, 0)),
            pl.BlockSpec((None, 3, TK, TQ), lambda b, h, q: (h, 0, 0, 0)),
            pl.BlockSpec((4, DIFF_QK), lambda b, h, q: (0, 0)),
            pl.BlockSpec((1, LANES), lambda b, h, q: (0, 0)),
        ],
        out_specs=pl.BlockSpec((TQ, LANES), lambda b, h, q: (b * nq + q, h)),
        out_shape=jax.ShapeDtypeStruct((n, DIFF_HEADS * LANES), BF16),
        scratch_shapes=[
            pltpu.VMEM((LANES, 2 * TQ), BF16),
            pltpu.VMEM((1, 2 * TQ), F32),
            pltpu.VMEM((1, 2 * TQ), F32),
            pltpu.VMEM((LANES, 2 * TQ), F32),
        ],
        compiler_params=_cparams(3),
        name="diff_attn",
    )(dqn, dkn, vt, bias, lam_p, sub_g)


def _dsa_body(q_ref, k_ref, vt_ref, iq_ref, ik_ref, iw_ref, bias_ref, o_ref,
              qt_ref, qit_ref, wt_ref, keys_ref, t_ref, m_ref, l_ref, acc_ref, *, top_k, pos_bits):
    qb = pl.program_id(1)
    nkb = qb + 1

    for p in range(IDX_HEADS // 2):
        gt = iq_ref[:, p * LANES:(p + 1) * LANES].astype(F32).T
        dim = lax.broadcasted_iota(I32, gt.shape, 0)
        qit_ref[:, (2 * p) * TQ:(2 * p + 1) * TQ] = jnp.where(dim < IDX_DIM, gt, 0.0).astype(BF16)
        qit_ref[:, (2 * p + 1) * TQ:(2 * p + 2) * TQ] = jnp.where(dim >= IDX_DIM, gt, 0.0).astype(BF16)
    for hd in range(DSA_HEADS):
        qt_ref[:, hd * TQ:(hd + 1) * TQ] = q_ref[:, hd * LANES:(hd + 1) * LANES].astype(F32).T.astype(BF16)
    wscale = (IDX_HEADS ** -0.5) * (IDX_DIM ** -0.5)
    wt_ref[...] = (iw_ref[...] * wscale).T[:IDX_HEADS, :]

    key_i = lax.broadcasted_iota(I32, (TK, TQ), 0)
    qry_i = lax.broadcasted_iota(I32, (TK, TQ), 1)
    causal = key_i <= qry_i

    def index_block(kb, masked):
        r0 = pl.multiple_of(kb * TK, TK)
        ik = ik_ref[pl.ds(r0, TK), :]
        idx = jnp.zeros((TK, TQ), F32)
        for hi in range(IDX_HEADS):
            s = _dot(ik, qit_ref[:, hi * TQ:(hi + 1) * TQ])
            idx = idx + jnp.maximum(s, 0.0) * wt_ref[hi:hi + 1, :]
        idx = jnp.where(idx == 0.0, 0.0, idx)
        bits = pltpu.bitcast(idx, I32)
        key = jnp.where(bits < 0, bits ^ jnp.int32(0x7FFFFFFF), bits)
        if masked:
            key = jnp.where(causal, key, jnp.int32(INT_MIN))
        keys_ref[kb] = key

    def index_loop(kb, carry):
        index_block(kb, False)
        return carry

    lax.fori_loop(0, nkb - 1, index_loop, 0)
    index_block(nkb - 1, True)

    def count(pred):
        def body(kb, acc):
            x = jnp.where(pred(keys_ref[kb], key_i + kb * TK), 1.0, 0.0)
            parts = [x[i * SUBLANES_F32:(i + 1) * SUBLANES_F32] for i in range(TK // SUBLANES_F32)]
            while len(parts) > 1:
                parts = [parts[i] + parts[i + 1] for i in range(0, len(parts), 2)]
            return acc + parts[0]
        acc = lax.fori_loop(0, nkb, body, jnp.zeros((SUBLANES_F32, TQ), F32))
        return jnp.sum(acc, axis=0, keepdims=True)

    kf = float(top_k)

    def count_ge(cand):
        return count(lambda kk, pos: kk >= cand)

    total = (nkb * TK).astype(F32) * jnp.ones((1, TQ), F32)
    c0 = count_ge(jnp.zeros((1, TQ), I32))
    acc0 = c0 >= kf
    t0 = jnp.where(acc0, jnp.int32(0), jnp.int32(INT_MIN))
    cge0 = jnp.where(acc0, c0, total)

    def bit_step(i, carry):
        t, cge = carry
        cand = t | jnp.left_shift(jnp.int32(1), 30 - i)
        cnt = count_ge(cand)
        ok = cnt >= kf
        return jnp.where(ok, cand, t), jnp.where(ok, cnt, cge)

    t, cge = lax.fori_loop(0, 31, bit_step, (t0, cge0))
    t_ref[...] = t

    @pl.when(jnp.max(cge) > kf)
    def _():
        tt = t_ref[...]
        r = kf - count(lambda kk, pos: kk > tt)

        def pos_step(i, pcut):
            cand = pcut | jnp.left_shift(jnp.int32(1), pos_bits - 1 - i)
            tied_before = count(lambda kk, pos: jnp.logical_and(kk == tt, pos < cand))
            return jnp.where(tied_before <= r, cand, pcut)

        pcut = lax.fori_loop(0, pos_bits, pos_step, jnp.zeros((1, TQ), I32))

        def demote(kb, carry):
            kk = keys_ref[kb]
            drop = jnp.logical_and(kk == tt, key_i + kb * TK >= pcut)
            keys_ref[kb] = jnp.where(drop, kk - 1, kk)
            return carry

        lax.fori_loop(0, nkb, demote, 0)

    m_ref[...] = jnp.full_like(m_ref, NEG)
    l_ref[...] = jnp.zeros_like(l_ref)
    acc_ref[...] = jnp.zeros_like(acc_ref)

    def block(kb, carry):
        r0 = pl.multiple_of(kb * TK, TK)
        typ = _bias_tile_index(kb, qb)
        sel = keys_ref[kb] >= t_ref[...]
        k_blk = k_ref[pl.ds(r0, TK), :]
        vt = vt_ref[kb]
        for hd in range(DSA_HEADS):
            sl = slice(hd * TQ, (hd + 1) * TQ)
            s = jnp.where(sel, _dot(k_blk, qt_ref[:, sl]) + bias_ref[hd, typ], NEG)
            _softmax_step_t(s, vt, m_ref.at[:, sl], l_ref.at[:, sl], acc_ref.at[:, sl])
        return carry

    lax.fori_loop(0, nkb, block, 0)

    for hd in range(DSA_HEADS):
        sl = slice(hd * TQ, (hd + 1) * TQ)
        o_ref[:, hd * LANES:(hd + 1) * LANES] = (acc_ref[:, sl] / l_ref[:, sl]).T.astype(BF16)


def _dsa(sqn, skn, vt, proj, iw, bias, batch, seq):
    n = sqn.shape[0]
    nq = seq // TQ
    top_k = min(IDX_TOPK_MAX, seq // 4)
    pos_bits = int(seq).bit_length()
    iq_block = 4
    ik_block = (5 * 512 + 2 * LANES) // LANES
    return pl.pallas_call(
        functools.partial(_dsa_body, top_k=top_k, pos_bits=pos_bits),
        grid=(batch, nq),
        in_specs=[
            pl.BlockSpec((TQ, DSA_HEADS * LANES), lambda b, q: (b * nq + q, 0)),
            pl.BlockSpec((seq, LANES), lambda b, q: (b, 0)),
            pl.BlockSpec((None, None, seq // TK, LANES, TK), lambda b, q: (b, DIFF_HEADS, 0, 0, 0)),
            pl.BlockSpec((TQ, IDX_HEADS * IDX_DIM), lambda b, q: (b * nq + q, iq_block)),
            pl.BlockSpec((seq, LANES), lambda b, q: (b, ik_block)),
            pl.BlockSpec((TQ, LANES), lambda b, q: (b * nq + q, 0)),
            pl.BlockSpec((DSA_HEADS, 3, TK, TQ), lambda b, q: (1, 0, 0, 0)),
        ],
        out_specs=pl.BlockSpec((TQ, DSA_HEADS * LANES), lambda b, q: (b * nq + q, 0)),
        out_shape=jax.ShapeDtypeStruct((n, DSA_HEADS * LANES), BF16),
        scratch_shapes=[
            pltpu.VMEM((LANES, DSA_HEADS * TQ), BF16),
            pltpu.VMEM((LANES, IDX_HEADS * TQ), BF16),
            pltpu.VMEM((IDX_HEADS, TQ), F32),
            pltpu.VMEM((seq // TK, TK, TQ), I32),
            pltpu.VMEM((1, TQ), I32),
            pltpu.VMEM((1, DSA_HEADS * TQ), F32),
            pltpu.VMEM((1, DSA_HEADS * TQ), F32),
            pltpu.VMEM((LANES, DSA_HEADS * TQ), F32),
        ],
        compiler_params=_cparams(2),
        name="dsa",
    )(sqn, skn, vt, proj, proj, iw, bias)


def _rel_bucket(dist):
    exact = REL_BUCKETS // 2
    n = jnp.maximum(dist, 0)
    nf = jnp.maximum(n, exact).astype(F32)
    far = exact + (jnp.log(nf / exact) / math.log(REL_MAX_DIST / exact) * (REL_BUCKETS - exact)).astype(I32)
    return jnp.where(n < exact, n, jnp.minimum(far, REL_BUCKETS - 1))


def _bias_tiles_body(tab_ref, bucket_ref, o_ref):
    hd = pl.program_id(0)
    nh = DIFF_HEADS + DSA_HEADS
    key = lax.broadcasted_iota(I32, (TK, TQ), 0)
    qry = lax.broadcasted_iota(I32, (TK, TQ), 1)
    for tile in range(3):
        bucket = bucket_ref[tile]
        acc = jnp.zeros(bucket.shape, F32)
        for b in range(REL_BUCKETS):
            acc = jnp.where(bucket == b, tab_ref[b * nh + hd], acc)
        if tile == 2:
            acc = jnp.where(key <= qry, acc, NEG)
        o_ref[tile] = acc * LOG2E


def _bias_tables(rel_bias):
    nh = DIFF_HEADS + DSA_HEADS
    assert TK + 1 >= REL_MAX_DIST
    j = jnp.arange(TK, dtype=I32)[:, None]
    i = jnp.arange(TQ, dtype=I32)[None, :]
    dist = jnp.stack([jnp.full((TK, TQ), 2 * TK, I32), i - j + TK, i - j])
    return pl.pallas_call(
        _bias_tiles_body,
        grid=(nh,),
        in_specs=[pl.BlockSpec(memory_space=pltpu.SMEM), pl.BlockSpec((3, TK, TQ), lambda h: (0, 0, 0))],
        out_specs=pl.BlockSpec((None, 3, TK, TQ), lambda h: (h, 0, 0, 0)),
        out_shape=jax.ShapeDtypeStruct((nh, 3, TK, TQ), F32),
        compiler_params=_cparams(1),
        name="bias_tiles",
    )(rel_bias.astype(F32).reshape(-1), _rel_bucket(dist))


def _block_diag_ones(width, group):
    r = np.arange(width)
    return jnp.asarray((r[:, None] // group) == (r[None, :] // group), dtype=BF16)


def _pad_lanes(w, width=LANES):
    return jnp.pad(w, ((0, 0), (0, width - w.shape[1])))


def kernel(x, rel_bias, mix_norm_g, ev_w_in, ev_w_out, gmlp_w_s, gmlp_b_s, gdn_conv_w, gdn_a_log, gdn_dt_bias,
           gdn_norm_g, od_w_in, od_w_out, diff_q_norm_g, diff_k_norm_g, diff_lambda, diff_sub_norm_g,
           dsa_q_norm_g, dsa_k_norm_g, ffn_norm_g, ffn_w_up, ffn_conv_w, ffn_conv_b, ffn_w_down):
    batch, seq, d = x.shape
    n = batch * seq
    depth = mix_norm_g.shape[0]
    assert d == D_MODEL and seq % max(TM_PROJ, TL_PREP, TL_GDN, TQ) == 0
    h = x.reshape(n, d)
    bias_tiles = _bias_tables(rel_bias)
    bd64 = _block_diag_ones(512, DIFF_QK)
    bd128 = _block_diag_ones(512, DSA_DIM)
    gw = GMLP_GROUPS * LANES
    qkv_w = 3 * GDN_HEADS * GDN_DIM

    for layer in range(depth):
        j = layer // 2
        g_mix = mix_norm_g[layer].reshape(1, d)
        if layer % 2 == 0:
            w = ev_w_in[j]
            o_u, o_v, o_qkv = 0, gw, 2 * gw
            o_b = o_qkv + qkv_w
            o_a = o_b + GDN_HEADS
            o_z = o_a + GDN_HEADS
            w_main = jnp.concatenate(
                [w[:, o_u:o_v], w[:, o_v:o_qkv], w[:, o_z:o_z + GDN_HEADS * GDN_DIM], w[:, o_qkv:o_b]],
                axis=1).astype(BF16)
            w_gate = _pad_lanes(w[:, o_b:o_z]).astype(BF16)
            proj, gates = _in_proj(h, g_mix, w_main, w_gate)
            gp = jnp.stack([
                _pad_lanes(jnp.concatenate([jnp.zeros((GDN_HEADS,), F32), gdn_a_log[j]])[None])[0],
                _pad_lanes(jnp.concatenate([jnp.zeros((GDN_HEADS,), F32), gdn_dt_bias[j]])[None])[0]])
            y_a, qkvn, gates2 = _even_prep(proj, gates, gmlp_w_s[j], gmlp_b_s[j].T, gdn_conv_w[j], gp, batch, seq)
            y_b = _gdn(qkvn, gates2, proj, gdn_norm_g[j].reshape(1, GDN_DIM), batch, seq)
            h = _out_proj(h, y_a, y_b, ev_w_out[j].astype(BF16))
        else:
            lambda_init = 0.8 - 0.6 * math.exp(-0.3 * layer)
            w = od_w_in[j]
            c = np.cumsum([0, 512, 512, 512, 512, 128, 128, 512, 64, 8])
            dq, dk, dv, sq, sk, sv, iq, ik, iw = [w[:, c[i]:c[i + 1]] for i in range(9)]
            w_main = jnp.concatenate([dq, dk, dv, sq, iq, sk, sv, ik, ik], axis=1).astype(BF16)
            w_gate = _pad_lanes(iw).astype(BF16)
            proj, iw_out = _in_proj(h, g_mix, w_main, w_gate)
            dqn, dkn, sqn, skn, vt = _odd_prep(
                proj,
                jnp.tile(diff_q_norm_g[j], 2 * DIFF_HEADS)[None], jnp.tile(diff_k_norm_g[j], 2 * DIFF_HEADS)[None],
                jnp.tile(dsa_q_norm_g[j], DSA_HEADS)[None], dsa_k_norm_g[j][None], bd64, bd128, batch, seq)
            y_c = _diff_attn(dqn, dkn, vt, bias_tiles, diff_lambda[j], diff_sub_norm_g[j][None],
                             batch, seq, lambda_init)
            y_d = _dsa(sqn, skn, vt, proj, iw_out, bias_tiles, batch, seq)
            h = _out_proj(h, y_c, y_d, od_w_out[j].astype(BF16))
        h = _ffn(h, ffn_norm_g[layer].reshape(1, d), ffn_w_up[layer].astype(BF16), ffn_conv_w[layer],
                 ffn_conv_b[layer].reshape(1, 2 * D_FF), ffn_w_down[layer].astype(BF16), seq)
    return h.reshape(batch, seq, d)
```

```python
import functools
import math

import numpy as np
import jax
import jax.numpy as jnp
from jax import lax
from jax.experimental import pallas as pl
from jax.experimental.pallas import tpu as pltpu

F32 = jnp.float32
BF16 = jnp.bfloat16
I32 = jnp.int32
HIGHEST = lax.Precision.HIGHEST

D_MODEL = 1024
GMLP_GROUPS = 4
GMLP_CHUNK = 128
GDN_HEADS = 4
GDN_DIM = 128
GDN_CHUNK = 64
GDN_CONV = 4
DIFF_HEADS = 4
DIFF_QK = 64
DSA_HEADS = 4
DSA_DIM = 128
IDX_HEADS = 8
IDX_DIM = 64
IDX_TOPK_MAX = 256
REL_BUCKETS = 32
REL_MAX_DIST = 128
D_FF = 2816
FFN_CONV = 3
EPS = 1e-6

LANES = 128
SUBLANES_F32 = 8
SUBLANES_BF16 = 16
VMEM_LIMIT = 56 * 1024 * 1024
VT_ROWS = LANES + SUBLANES_BF16

TM_PROJ = 512
TL_PREP = 512
TL_GDN = 512
GDN_PREP_CHUNKS = 2
TQ = 256
TK = 256
FF_CHUNK = 256
NEG = -1e30
LOG2E = 1.0 / math.log(2.0)
INT_MIN = -2 ** 31


def _cparams(n_axes):
    return pltpu.CompilerParams(dimension_semantics=("arbitrary",) * n_axes, vmem_limit_bytes=VMEM_LIMIT)


def _dot(a, b, **kw):
    return jnp.dot(a, b, preferred_element_type=F32, **kw)


def _dot_nt(a, b, **kw):
    return lax.dot_general(a, b, (((1,), (1,)), ((), ())), preferred_element_type=F32, **kw)


def _dot_tn(a, b, **kw):
    return lax.dot_general(a, b, (((0,), (0,)), ((), ())), preferred_element_type=F32, **kw)


def _rms(x, g):
    return x * lax.rsqrt(jnp.mean(x * x, axis=-1, keepdims=True) + EPS) * g


def _sigmoid(x):
    return 1.0 / (1.0 + jnp.exp(-x))


def _silu(x):
    return x * _sigmoid(x)


def _gelu_tanh(x):
    return 0.5 * x * (1.0 + jnp.tanh(math.sqrt(2.0 / math.pi) * (x + 0.044715 * (x * x * x))))


def _softplus(x):
    return jnp.maximum(x, 0.0) + jnp.log(1.0 + jnp.exp(-jnp.abs(x)))


def _in_proj_body(h_ref, g_ref, w_ref, wg_ref, o_ref, og_ref, *, nout):
    xn = _rms(h_ref[...], g_ref[...]).astype(BF16)
    for c in range(0, nout, 512):
        e = min(c + 512, nout)
        o_ref[:, c:e] = _dot(xn, w_ref[:, c:e]).astype(BF16)
    og_ref[...] = _dot(xn, wg_ref[...])


def _in_proj(h, g, w, wg):
    n, d = h.shape
    nout = w.shape[1]
    return pl.pallas_call(
        functools.partial(_in_proj_body, nout=nout),
        grid=(n // TM_PROJ,),
        in_specs=[
            pl.BlockSpec((TM_PROJ, d), lambda i: (i, 0)),
            pl.BlockSpec((1, d), lambda i: (0, 0)),
            pl.BlockSpec((d, nout), lambda i: (0, 0)),
            pl.BlockSpec((d, LANES), lambda i: (0, 0)),
        ],
        out_specs=[
            pl.BlockSpec((TM_PROJ, nout), lambda i: (i, 0)),
            pl.BlockSpec((TM_PROJ, LANES), lambda i: (i, 0)),
        ],
        out_shape=[jax.ShapeDtypeStruct((n, nout), BF16), jax.ShapeDtypeStruct((n, LANES), F32)],
        compiler_params=_cparams(1),
        name="in_proj",
    )(h, g, w, wg)


def _out_proj_body(h_ref, y1_ref, y2_ref, w_ref, o_ref):
    half = y1_ref.shape[1]
    o_ref[...] = h_ref[...] + _dot(y1_ref[...], w_ref[:half, :]) + _dot(y2_ref[...], w_ref[half:, :])


def _out_proj(h, y1, y2, w):
    n, d = h.shape
    half = y1.shape[1]
    return pl.pallas_call(
        _out_proj_body,
        grid=(n // TM_PROJ,),
        in_specs=[
            pl.BlockSpec((TM_PROJ, d), lambda i: (i, 0)),
            pl.BlockSpec((TM_PROJ, half), lambda i: (i, 0)),
            pl.BlockSpec((TM_PROJ, half), lambda i: (i, 0)),
            pl.BlockSpec((2 * half, d), lambda i: (0, 0)),
        ],
        out_specs=pl.BlockSpec((TM_PROJ, d), lambda i: (i, 0)),
        out_shape=jax.ShapeDtypeStruct((n, d), F32),
        compiler_params=_cparams(1),
        name="out_proj",
    )(h, y1, y2, w)


def _ffn_body(h_ref, halo_ref, g_ref, wup_ref, cw_ref, cb_ref, wdn_ref, o_ref, xn_ref, act_ref, *, tiles_per_seq):
    tm = h_ref.shape[0]
    i = pl.program_id(0)
    h = h_ref[...]
    g = g_ref[...]
    hb = SUBLANES_BF16
    xn_ref[hb:, :] = _rms(h, g).astype(BF16)
    keep = jnp.where(i % tiles_per_seq == 0, 0.0, 1.0)
    xn_ref[:hb, :] = (_rms(halo_ref[...], g) * keep).astype(BF16)
    xn = xn_ref[...]

    def up_pair(c):
        return [_dot(xn, wup_ref[:, base:base + FF_CHUNK]) for base in (c, D_FF + c)]

    def conv(up, base):
        cw = cw_ref[:, base:base + FF_CHUNK]
        return (cw[0:1] * up[hb - 2:tm + hb - 2] + cw[1:2] * up[hb - 1:tm + hb - 1] + cw[2:3] * up[hb:tm + hb]
                + cb_ref[:, base:base + FF_CHUNK])

    chunks = list(range(0, D_FF, FF_CHUNK))
    ups = up_pair(chunks[0])
    for n, c in enumerate(chunks):
        nxt = up_pair(chunks[n + 1]) if n + 1 < len(chunks) else None
        act_ref[:, c:c + FF_CHUNK] = (_silu(conv(ups[0], c)) * conv(ups[1], D_FF + c)).astype(BF16)
        ups = nxt
    o_ref[...] = h + _dot(act_ref[...], wdn_ref[...])


def _ffn(h, g, wup, cw, cb, wdn, seq):
    n, d = h.shape
    tiles_per_seq = seq // TM_PROJ
    halo_blocks = TM_PROJ // SUBLANES_BF16
    const = dict(pipeline_mode=pl.Buffered(1))
    return pl.pallas_call(
        functools.partial(_ffn_body, tiles_per_seq=tiles_per_seq),
        grid=(n // TM_PROJ,),
        in_specs=[
            pl.BlockSpec((TM_PROJ, d), lambda i: (i, 0)),
            pl.BlockSpec((SUBLANES_BF16, d), lambda i: (jnp.maximum(i * halo_blocks - 1, 0), 0)),
            pl.BlockSpec((1, d), lambda i: (0, 0)),
            pl.BlockSpec((d, 2 * D_FF), lambda i: (0, 0), **const),
            pl.BlockSpec((FFN_CONV, 2 * D_FF), lambda i: (0, 0)),
            pl.BlockSpec((1, 2 * D_FF), lambda i: (0, 0)),
            pl.BlockSpec((D_FF, d), lambda i: (0, 0), **const),
        ],
        out_specs=pl.BlockSpec((TM_PROJ, d), lambda i: (i, 0)),
        out_shape=jax.ShapeDtypeStruct((n, d), F32),
        scratch_shapes=[pltpu.VMEM((TM_PROJ + SUBLANES_BF16, d), BF16), pltpu.VMEM((TM_PROJ, D_FF), BF16)],
        compiler_params=_cparams(1),
        name="ffn",
    )(h, h, g, wup, cw, cb, wdn)


def _even_prep_body(uv_ref, qkv_ref, halo_ref, gate_ref, ws_ref, bs_ref, cw_ref, gp_ref,
                    ya_ref, qkvo_ref, gout_ref):
    tl = uv_ref.shape[0]
    t = pl.program_id(1)
    gw = LANES
    row = lax.broadcasted_iota(I32, (GMLP_CHUNK, GMLP_CHUNK), 0)
    col = lax.broadcasted_iota(I32, (GMLP_CHUNK, GMLP_CHUNK), 1)
    for gi in range(GMLP_GROUPS):
        u = _gelu_tanh(uv_ref[:, gi * gw:(gi + 1) * gw].astype(F32))
        v = _gelu_tanh(uv_ref[:, (GMLP_GROUPS + gi) * gw:(GMLP_GROUPS + gi + 1) * gw].astype(F32))
        vc = v - jnp.mean(v, axis=-1, keepdims=True)
        vn = (vc * lax.rsqrt(jnp.mean(vc * vc, axis=-1, keepdims=True) + EPS)).astype(BF16)
        w = jnp.where(col <= row, ws_ref[gi], 0.0).astype(BF16)
        b = bs_ref[:, gi:gi + 1]
        for c in range(tl // GMLP_CHUNK):
            r0 = c * GMLP_CHUNK
            mixed = _dot(w, vn[r0:r0 + GMLP_CHUNK]) + b
            ya_ref[r0:r0 + GMLP_CHUNK, gi * gw:(gi + 1) * gw] = (u[r0:r0 + GMLP_CHUNK] * mixed).astype(BF16)
    keep = jnp.where(t == 0, 0.0, 1.0)
    hb = SUBLANES_BF16
    for j in range(3 * GDN_HEADS):
        sl = slice(j * gw, (j + 1) * gw)
        x = jnp.concatenate([halo_ref[:, sl].astype(F32) * keep, qkv_ref[:, sl].astype(F32)], axis=0)
        cw = cw_ref[:, sl]
        y = cw[0:1] * x[hb - 3:tl + hb - 3]
        for k in range(1, GDN_CONV):
            y = y + cw[k:k + 1] * x[hb - 3 + k:tl + hb - 3 + k]
        y = _silu(y)
        if j < 2 * GDN_HEADS:
            y = y * lax.rsqrt(jnp.sum(y * y, axis=-1, keepdims=True) + EPS)
        if j < GDN_HEADS:
            y = y * (GDN_DIM ** -0.5)
        qkvo_ref[:, sl] = y.astype(BF16)
    x = gate_ref[...]
    lane = lax.broadcasted_iota(I32, x.shape, 1)
    beta = _sigmoid(x)
    gdec = -jnp.exp(gp_ref[0:1, :]) * _softplus(x + gp_ref[1:2, :])
    gout_ref[...] = jnp.where(lane < GDN_HEADS, beta, gdec)


def _even_prep(proj, gates, ws, bs_t, cw, gp, batch, seq):
    n = proj.shape[0]
    nt = seq // TL_PREP
    halo_blocks = TL_PREP // SUBLANES_BF16
    qkv_w = 3 * GDN_HEADS * GDN_DIM
    row = lambda b, t: b * nt + t
    return pl.pallas_call(
        _even_prep_body,
        grid=(batch, nt),
        in_specs=[
            pl.BlockSpec((TL_PREP, 2 * GMLP_GROUPS * LANES), lambda b, t: (row(b, t), 0)),
            pl.BlockSpec((TL_PREP, qkv_w), lambda b, t: (row(b, t), 1)),
            pl.BlockSpec((SUBLANES_BF16, qkv_w), lambda b, t: (jnp.maximum(row(b, t) * halo_blocks - 1, 0), 1)),
            pl.BlockSpec((TL_PREP, LANES), lambda b, t: (row(b, t), 0)),
            pl.BlockSpec((GMLP_GROUPS, GMLP_CHUNK, GMLP_CHUNK), lambda b, t: (0, 0, 0)),
            pl.BlockSpec((GMLP_CHUNK, GMLP_GROUPS), lambda b, t: (0, 0)),
            pl.BlockSpec((GDN_CONV, qkv_w), lambda b, t: (0, 0)),
            pl.BlockSpec((2, LANES), lambda b, t: (0, 0)),
        ],
        out_specs=[
            pl.BlockSpec((TL_PREP, GMLP_GROUPS * LANES), lambda b, t: (row(b, t), 0)),
            pl.BlockSpec((TL_PREP, qkv_w), lambda b, t: (row(b, t), 0)),
            pl.BlockSpec((TL_PREP, LANES), lambda b, t: (row(b, t), 0)),
        ],
        out_shape=[
            jax.ShapeDtypeStruct((n, GMLP_GROUPS * LANES), BF16),
            jax.ShapeDtypeStruct((n, qkv_w), BF16),
            jax.ShapeDtypeStruct((n, LANES), F32),
        ],
        compiler_params=_cparams(2),
        name="even_prep",
    )(proj, proj, proj, gates, ws, bs_t, cw, gp)


def _split_bf16(x):
    hi = x.astype(BF16)
    return hi, (x - hi.astype(F32)).astype(BF16)


def _mm3(a, b):
    ah, al = _split_bf16(a)
    bh, bl = _split_bf16(b)
    return _dot(ah, bh) + (_dot(ah, bl) + _dot(al, bh))


def _unit_lower_inverses(lows):
    c = lows[0].shape[0]
    row = lax.broadcasted_iota(I32, (c, c), 0)
    col = lax.broadcasted_iota(I32, (c, c), 1)
    eye = jnp.where(row == col, 1.0, 0.0)
    same16 = jnp.right_shift(row, 4) == jnp.right_shift(col, 4)
    same32 = jnp.right_shift(row, 5) == jnp.right_shift(col, 5)
    ps = [jnp.where(same16, -low, 0.0) for low in lows]
    xs = [eye + p for p in ps]
    for _ in range(3):
        ps = [_mm3(p, p) for p in ps]
        xs = [x + _mm3(x, p) for x, p in zip(xs, ps)]
    mid = jnp.logical_and(same32, jnp.logical_not(same16))
    mm1 = lambda a, b: _dot(a.astype(BF16), b.astype(BF16))
    ys = [mm1(jnp.where(mid, low, 0.0), x) for low, x in zip(lows, xs)]
    xs = [x - mm1(x, y) for x, y in zip(xs, ys)]
    ys = [mm1(jnp.where(same32, 0.0, low), x) for low, x in zip(lows, xs)]
    return [x - mm1(x, y) for x, y in zip(xs, ys)]


def _gdn_body(qkv_ref, gate_ref, z_ref, gn_ref, o_ref,
              s_ref, u_ref, w_ref, qg_ref, kdt_ref, intra_ref, egl_ref):
    tl = qkv_ref.shape[0]
    c = GDN_CHUNK
    d = GDN_DIM
    heads = range(GDN_HEADS)
    t = pl.program_id(1)

    @pl.when(t == 0)
    def _():
        s_ref[...] = jnp.zeros_like(s_ref)

    row = lax.broadcasted_iota(I32, (c, c), 0)
    col = lax.broadcasted_iota(I32, (c, c), 1)
    incl = col <= row
    strict = col < row
    lmat = jnp.where(incl, 1.0, 0.0)
    gn = gn_ref[...]

    def prepare(it, carry):
        cis = [it * GDN_PREP_CHUNKS + j for j in range(GDN_PREP_CHUNKS)]
        rows = [pl.ds(pl.multiple_of(ci * c, c), c) for ci in cis]
        gates = [gate_ref[r, :] for r in rows]
        gcum = [_dot(lmat, g, precision=HIGHEST) for g in gates]
        pairs = [(j, hd) for j in range(GDN_PREP_CHUNKS) for hd in heads]
        qs = [qkv_ref[rows[j], hd * d:(hd + 1) * d] for j, hd in pairs]
        ks = [qkv_ref[rows[j], (GDN_HEADS + hd) * d:(GDN_HEADS + hd + 1) * d] for j, hd in pairs]
        vs = [qkv_ref[rows[j], (2 * GDN_HEADS + hd) * d:(2 * GDN_HEADS + hd + 1) * d] for j, hd in pairs]
        beta = [jnp.broadcast_to(gates[j][:, hd:hd + 1], (c, d)) for j, hd in pairs]
        gc = [jnp.broadcast_to(gcum[j][:, GDN_HEADS + hd:GDN_HEADS + hd + 1], (c, d)) for j, hd in pairs]
        decay = [jnp.where(incl, jnp.exp(jnp.where(incl, g[:, :c] - g.T[:c, :], 0.0)), 0.0) for g in gc]
        kb = [k.astype(F32) * b for k, b in zip(ks, beta)]
        lows = [jnp.where(strict, _dot_nt(x.astype(BF16), k) * dc, 0.0) for x, k, dc in zip(kb, ks, decay)]
        tinv = [x.astype(BF16) for x in _unit_lower_inverses(lows)]
        egc = [jnp.exp(g) for g in gc]
        rhs = [jnp.concatenate([(v.astype(F32) * b).astype(BF16), (x * e).astype(BF16)], axis=1)
               for v, b, x, e in zip(vs, beta, kb, egc)]
        uw = [_dot(ti, r) for ti, r in zip(tinv, rhs)]
        qk = [_dot_nt(q, k) for q, k in zip(qs, ks)]
        for n, (j, hd) in enumerate(pairs):
            ci = cis[j]
            u_ref[ci, hd] = uw[n][:, :d]
            w_ref[ci, hd] = uw[n][:, d:].astype(BF16)
            intra_ref[ci, hd] = jnp.where(incl, qk[n] * decay[n], 0.0).astype(BF16)
            qg_ref[ci, hd] = (qs[n].astype(F32) * egc[n]).astype(BF16)
            g_last = gc[n][c - 1:c, :]
            kdt_ref[ci, hd] = (ks[n].astype(F32) * jnp.exp(g_last - gc[n])).T.astype(BF16)
            egl_ref[ci, hd] = jnp.exp(g_last)
        return carry

    lax.fori_loop(0, tl // c // GDN_PREP_CHUNKS, prepare, 0)

    def scan(ci, carry):
        r0 = pl.multiple_of(ci * c, c)
        states = [s_ref[hd] for hd in heads]
        sb = [s.astype(BF16) for s in states]
        ws = [_dot(w_ref[ci, hd], sb[hd]) for hd in heads]
        qsd = [_dot(qg_ref[ci, hd], sb[hd]) for hd in heads]
        vnb = [(u_ref[ci, hd] - ws[hd]).astype(BF16) for hd in heads]
        for hd in heads:
            s_ref[hd] = states[hd] * egl_ref[ci, hd] + _dot(kdt_ref[ci, hd], vnb[hd])
        for hd in heads:
            out = qsd[hd] + _dot(intra_ref[ci, hd], vnb[hd])
            z = z_ref[pl.ds(r0, c), hd * d:(hd + 1) * d].astype(F32)
            o_ref[pl.ds(r0, c), hd * d:(hd + 1) * d] = (_rms(out, gn) * _silu(z)).astype(BF16)
        return carry

    lax.fori_loop(0, tl // c, scan, 0)


def _gdn(qkv, gates, proj, gn, batch, seq):
    n = qkv.shape[0]
    nt = seq // TL_GDN
    nc = TL_GDN // GDN_CHUNK
    width = GDN_HEADS * GDN_DIM
    row = lambda b, t: b * nt + t
    return pl.pallas_call(
        _gdn_body,
        grid=(batch, nt),
        in_specs=[
            pl.BlockSpec((TL_GDN, 3 * width), lambda b, t: (row(b, t), 0)),
            pl.BlockSpec((TL_GDN, LANES), lambda b, t: (row(b, t), 0)),
            pl.BlockSpec((TL_GDN, width), lambda b, t: (row(b, t), 2)),
            pl.BlockSpec((1, GDN_DIM), lambda b, t: (0, 0)),
        ],
        out_specs=pl.BlockSpec((TL_GDN, width), lambda b, t: (row(b, t), 0)),
        out_shape=jax.ShapeDtypeStruct((n, width), BF16),
        scratch_shapes=[
            pltpu.VMEM((GDN_HEADS, GDN_DIM, GDN_DIM), F32),
            pltpu.VMEM((nc, GDN_HEADS, GDN_CHUNK, GDN_DIM), F32),
            pltpu.VMEM((nc, GDN_HEADS, GDN_CHUNK, GDN_DIM), BF16),
            pltpu.VMEM((nc, GDN_HEADS, GDN_CHUNK, GDN_DIM), BF16),
            pltpu.VMEM((nc, GDN_HEADS, GDN_DIM, GDN_CHUNK), BF16),
            pltpu.VMEM((nc, GDN_HEADS, GDN_CHUNK, GDN_CHUNK), BF16),
            pltpu.VMEM((nc, GDN_HEADS, 1, GDN_DIM), F32),
        ],
        compiler_params=_cparams(2),
        name="gdn",
    )(qkv, gates, proj, gn)


def _group_mean_sq(x, ones_bd, group):
    x2 = x * x
    hi = x2.astype(BF16)
    lo = (x2 - hi.astype(F32)).astype(BF16)
    return (_dot(hi, ones_bd) + _dot(lo, ones_bd)) * (1.0 / group)


def _odd_prep_body(dq_ref, dk_ref, dv_ref, sq_ref, sk_ref, sv_ref, gq_ref, gk_ref, gsq_ref, gsk_ref, bd64_ref,
                   bd128_ref, dqo_ref, dko_ref, sqo_ref, sko_ref, vt_ref):
    ones = jnp.ones((VT_ROWS - LANES, TK), BF16)
    for j in range(DIFF_HEADS + 1):
        src = sv_ref[...] if j == DIFF_HEADS else dv_ref[:, j * LANES:(j + 1) * LANES]
        xt = src.astype(F32).T
        for c in range(xt.shape[1] // TK):
            vt_ref[j, c, :LANES, :] = xt[:, c * TK:(c + 1) * TK].astype(BF16)
            vt_ref[j, c, LANES:, :] = ones
    bd64 = bd64_ref[...]
    x = dq_ref[...].astype(F32)
    dqo_ref[...] = (x * lax.rsqrt(_group_mean_sq(x, bd64, DIFF_QK) + EPS) * gq_ref[...]
                    * (DIFF_QK ** -0.5 * LOG2E)).astype(BF16)
    x = dk_ref[...].astype(F32)
    dko_ref[...] = (x * lax.rsqrt(_group_mean_sq(x, bd64, DIFF_QK) + EPS) * gk_ref[...]).astype(BF16)
    x = sq_ref[...].astype(F32)
    sqo_ref[...] = (x * lax.rsqrt(_group_mean_sq(x, bd128_ref[...], DSA_DIM) + EPS) * gsq_ref[...]
                    * (DSA_DIM ** -0.5 * LOG2E)).astype(BF16)
    x = sk_ref[...].astype(F32)
    sko_ref[...] = _rms(x, gsk_ref[...]).astype(BF16)


def _odd_prep(proj, gq, gk, gsq, gsk, bd64, bd128, batch, seq):
    n = proj.shape[0]
    w = 512
    nt = seq // TL_PREP
    kt = TL_PREP // TK
    full = lambda shape: pl.BlockSpec(shape, lambda b, t: (0,) * len(shape))
    row = lambda b, t: b * nt + t
    return pl.pallas_call(
        _odd_prep_body,
        grid=(batch, nt),
        in_specs=[
            pl.BlockSpec((TL_PREP, w), lambda b, t: (row(b, t), 0)),
            pl.BlockSpec((TL_PREP, w), lambda b, t: (row(b, t), 1)),
            pl.BlockSpec((TL_PREP, w), lambda b, t: (row(b, t), 2)),
            pl.BlockSpec((TL_PREP, w), lambda b, t: (row(b, t), 3)),
            pl.BlockSpec((TL_PREP, LANES), lambda b, t: (row(b, t), 5 * w // LANES)),
            pl.BlockSpec((TL_PREP, LANES), lambda b, t: (row(b, t), 5 * w // LANES + 1)),
            full((1, w)), full((1, w)), full((1, w)), full((1, LANES)), full((w, w)), full((w, w)),
        ],
        out_specs=[
            pl.BlockSpec((TL_PREP, w), lambda b, t: (row(b, t), 0)),
            pl.BlockSpec((TL_PREP, w), lambda b, t: (row(b, t), 0)),
            pl.BlockSpec((TL_PREP, w), lambda b, t: (row(b, t), 0)),
            pl.BlockSpec((TL_PREP, LANES), lambda b, t: (row(b, t), 0)),
            pl.BlockSpec((None, DIFF_HEADS + 1, kt, VT_ROWS, TK), lambda b, t: (b, 0, t, 0, 0)),
        ],
        out_shape=[
            jax.ShapeDtypeStruct((n, w), BF16), jax.ShapeDtypeStruct((n, w), BF16),
            jax.ShapeDtypeStruct((n, w), BF16), jax.ShapeDtypeStruct((n, LANES), BF16),
            jax.ShapeDtypeStruct((batch, DIFF_HEADS + 1, seq // TK, VT_ROWS, TK), BF16),
        ],
        compiler_params=_cparams(2),
        name="odd_prep",
    )(proj, proj, proj, proj, proj, proj, gq, gk, gsq, gsk, bd64, bd128)


def _bias_tile_index(kb, qb):
    return jnp.clip(kb - (qb - 2), 0, 2)


def _softmax_step_t(s_t, vt, m_ref, acc_ref):
    for g in range(s_t.shape[1] // LANES):
        sl = slice(g * LANES, (g + 1) * LANES)
        s = s_t[:, sl]
        m_prev = m_ref[:, sl]
        m_new = jnp.maximum(m_prev, jnp.max(s, axis=0, keepdims=True))
        alpha = jnp.exp2(m_prev - m_new)
        p = jnp.exp2(s - m_new)
        acc_ref[:, sl] = alpha * acc_ref[:, sl] + _dot(vt, p.astype(BF16))
        m_ref[:, sl] = m_new


def _attend_tiles(logits, vt_ref, nblk, s_ref, m_ref, acc_ref):
    m_ref[...] = jnp.full_like(m_ref, NEG)
    acc_ref[...] = jnp.zeros_like(acc_ref)
    s_ref[0] = logits(0)

    def pair(i, carry):
        kb = 2 * i
        s_ref[1] = logits(kb + 1)
        _softmax_step_t(s_ref[0], vt_ref[kb], m_ref, acc_ref)
        s_ref[0] = logits(jnp.minimum(kb + 2, nblk - 1))
        _softmax_step_t(s_ref[1], vt_ref[kb + 1], m_ref, acc_ref)
        return carry

    lax.fori_loop(0, nblk // 2, pair, 0)

    @pl.when(nblk % 2 == 1)
    def _():
        _softmax_step_t(s_ref[0], vt_ref[nblk - 1], m_ref, acc_ref)


def _diff_attn_body(q_ref, k_ref, vt_ref, bias_ref, lam_ref, sg_ref, o_ref,
                    qt_ref, m_ref, acc_ref, s_ref, *, lambda_init):
    qi = pl.program_id(2)
    nblk = qi + 1
    qt = q_ref[...].astype(F32).T
    dim = lax.broadcasted_iota(I32, qt.shape, 0)
    qt_ref[:, :TQ] = jnp.where(dim < DIFF_QK, qt, 0.0).astype(BF16)
    qt_ref[:, TQ:] = jnp.where(dim >= DIFF_QK, qt, 0.0).astype(BF16)

    def logits(kb):
        r0 = pl.multiple_of(kb * TK, TK)
        b = bias_ref[_bias_tile_index(kb, qi)]
        return _dot(k_ref[pl.ds(r0, TK), :], qt_ref[...]) + jnp.concatenate([b, b], axis=1)

    _attend_tiles(logits, vt_ref, nblk, s_ref, m_ref, acc_ref)

    lf = lam_ref[...]
    lam = (jnp.exp(jnp.sum(lf[0:1] * lf[1:2], axis=-1, keepdims=True))
           - jnp.exp(jnp.sum(lf[2:3] * lf[3:4], axis=-1, keepdims=True)) + lambda_init)
    den = acc_ref[LANES:LANES + 1, :]
    ot = acc_ref[:LANES, :TQ] / den[:, :TQ] - lam * (acc_ref[:LANES, TQ:] / den[:, TQ:])
    ot = ot * lax.rsqrt(jnp.mean(ot * ot, axis=0, keepdims=True) + EPS)
    o_ref[...] = (ot.T * sg_ref[...] * (1.0 - lambda_init)).astype(BF16)


def _diff_attn(dqn, dkn, vt, bias, lam_p, sub_g, batch, seq, lambda_init):
    n = dqn.shape[0]
    nq = seq // TQ
    return pl.pallas_call(
        functools.partial(_diff_attn_body, lambda_init=lambda_init),
        grid=(batch, DIFF_HEADS, nq),
        in_specs=[
            pl.BlockSpec((TQ, LANES), lambda b, h, q: (b * nq + q, h)),
            pl.BlockSpec((seq, LANES), lambda b, h, q: (b, h)),
            pl.BlockSpec((None, None, seq // TK, VT_ROWS, TK), lambda b, h, q: (b, h, 0, 0, 0)),
            pl.BlockSpec((None, 3, TK, TQ), lambda b, h, q: (h, 0, 0, 0)),
            pl.BlockSpec((4, DIFF_QK), lambda b, h, q: (0, 0)),
            pl.BlockSpec((1, LANES), lambda b, h, q: (0, 0)),
        ],
        out_specs=pl.BlockSpec((TQ, LANES), lambda b, h, q: (b * nq + q, h)),
        out_shape=jax.ShapeDtypeStruct((n, DIFF_HEADS * LANES), BF16),
        scratch_shapes=[
            pltpu.VMEM((LANES, 2 * TQ), BF16),
            pltpu.VMEM((1, 2 * TQ), F32),
            pltpu.VMEM((VT_ROWS, 2 * TQ), F32),
            pltpu.VMEM((2, TK, 2 * TQ), F32),
        ],
        compiler_params=_cparams(3),
        name="diff_attn",
    )(dqn, dkn, vt, bias, lam_p, sub_g)


def _dsa_body(q_ref, k_ref, vt_ref, iq_ref, ik_ref, iw_ref, bias_ref, o_ref,
              qt_ref, qit_ref, wt_ref, keys_ref, t_ref, m_ref, acc_ref, s_ref, *, top_k, pos_bits):
    qb = pl.program_id(1)
    nkb = qb + 1

    for p in range(IDX_HEADS // 2):
        gt = iq_ref[:, p * LANES:(p + 1) * LANES].astype(F32).T.astype(BF16)
        qit_ref[:, (2 * p) * TQ:(2 * p + 1) * TQ] = gt[:IDX_DIM]
        qit_ref[:, (2 * p + 1) * TQ:(2 * p + 2) * TQ] = gt[IDX_DIM:]
    for hd in range(DSA_HEADS):
        qt_ref[:, hd * TQ:(hd + 1) * TQ] = q_ref[:, hd * LANES:(hd + 1) * LANES].astype(F32).T.astype(BF16)
    wscale = (IDX_HEADS ** -0.5) * (IDX_DIM ** -0.5)
    wt_ref[...] = (iw_ref[...] * wscale).T[:IDX_HEADS, :]

    key_i = lax.broadcasted_iota(I32, (TK, TQ), 0)
    qry_i = lax.broadcasted_iota(I32, (TK, TQ), 1)
    causal = key_i <= qry_i

    def index_block(kb, masked):
        r0 = pl.multiple_of(kb * TK, TK)
        ik = ik_ref[pl.ds(r0, TK), :IDX_DIM]
        idx = jnp.zeros((TK, TQ), F32)
        for hi in range(IDX_HEADS):
            s = _dot(ik, qit_ref[:, hi * TQ:(hi + 1) * TQ])
            idx = idx + jnp.maximum(s, 0.0) * wt_ref[hi:hi + 1, :]
        idx = jnp.where(idx == 0.0, 0.0, idx)
        bits = pltpu.bitcast(idx, I32)
        key = jnp.where(bits < 0, bits ^ jnp.int32(0x7FFFFFFF), bits)
        if masked:
            key = jnp.where(causal, key, jnp.int32(INT_MIN))
        keys_ref[kb] = key

    def index_loop(kb, carry):
        index_block(kb, False)
        return carry

    lax.fori_loop(0, nkb - 1, index_loop, 0)
    index_block(nkb - 1, True)

    def count(pred):
        def body(kb, acc):
            x = jnp.where(pred(keys_ref[kb], key_i + kb * TK), 1.0, 0.0)
            parts = [x[i * SUBLANES_F32:(i + 1) * SUBLANES_F32] for i in range(TK // SUBLANES_F32)]
            while len(parts) > 1:
                parts = [parts[i] + parts[i + 1] for i in range(0, len(parts), 2)]
            return acc + parts[0]
        acc = lax.fori_loop(0, nkb, body, jnp.zeros((SUBLANES_F32, TQ), F32))
        return jnp.sum(acc, axis=0, keepdims=True)

    kf = float(top_k)

    def count_ge(cand):
        return count(lambda kk, pos: kk >= cand)

    total = (nkb * TK).astype(F32) * jnp.ones((1, TQ), F32)
    c0 = count_ge(jnp.zeros((1, TQ), I32))
    acc0 = c0 >= kf
    t0 = jnp.where(acc0, jnp.int32(0), jnp.int32(INT_MIN))
    cge0 = jnp.where(acc0, c0, total)

    def bit_step(i, carry):
        t, cge = carry
        cand = t | jnp.left_shift(jnp.int32(1), 30 - i)
        cnt = count_ge(cand)
        ok = cnt >= kf
        return jnp.where(ok, cand, t), jnp.where(ok, cnt, cge)

    t, cge = lax.fori_loop(0, 31, bit_step, (t0, cge0))
    t_ref[...] = t

    @pl.when(jnp.max(cge) > kf)
    def _():
        tt = t_ref[...]
        r = kf - count(lambda kk, pos: kk > tt)

        def pos_step(i, pcut):
            cand = pcut | jnp.left_shift(jnp.int32(1), pos_bits - 1 - i)
            tied_before = count(lambda kk, pos: jnp.logical_and(kk == tt, pos < cand))
            return jnp.where(tied_before <= r, cand, pcut)

        pcut = lax.fori_loop(0, pos_bits, pos_step, jnp.zeros((1, TQ), I32))

        def demote(kb, carry):
            kk = keys_ref[kb]
            drop = jnp.logical_and(kk == tt, key_i + kb * TK >= pcut)
            keys_ref[kb] = jnp.where(drop, kk - 1, kk)
            return carry

        lax.fori_loop(0, nkb, demote, 0)

    def logits(kb):
        r0 = pl.multiple_of(kb * TK, TK)
        typ = _bias_tile_index(kb, qb)
        sel = keys_ref[kb] >= t_ref[...]
        s = _dot(k_ref[pl.ds(r0, TK), :], qt_ref[...])
        return jnp.concatenate(
            [jnp.where(sel, s[:, hd * TQ:(hd + 1) * TQ] + bias_ref[hd, typ], NEG) for hd in range(DSA_HEADS)], axis=1)

    _attend_tiles(logits, vt_ref, nkb, s_ref, m_ref, acc_ref)

    for hd in range(DSA_HEADS):
        sl = slice(hd * TQ, (hd + 1) * TQ)
        o_ref[:, hd * LANES:(hd + 1) * LANES] = (acc_ref[:LANES, sl] / acc_ref[LANES:LANES + 1, sl]).T.astype(BF16)


def _dsa(sqn, skn, vt, proj, iw, bias, batch, seq):
    n = sqn.shape[0]
    nq = seq // TQ
    top_k = min(IDX_TOPK_MAX, seq // 4)
    pos_bits = int(seq).bit_length()
    iq_block = 4
    ik_block = (5 * 512 + 2 * LANES) // LANES
    return pl.pallas_call(
        functools.partial(_dsa_body, top_k=top_k, pos_bits=pos_bits),
        grid=(batch, nq),
        in_specs=[
            pl.BlockSpec((TQ, DSA_HEADS * LANES), lambda b, q: (b * nq + q, 0)),
            pl.BlockSpec((seq, LANES), lambda b, q: (b, 0)),
            pl.BlockSpec((None, None, seq // TK, VT_ROWS, TK), lambda b, q: (b, DIFF_HEADS, 0, 0, 0)),
            pl.BlockSpec((TQ, IDX_HEADS * IDX_DIM), lambda b, q: (b * nq + q, iq_block)),
            pl.BlockSpec((seq, LANES), lambda b, q: (b, ik_block)),
            pl.BlockSpec((TQ, LANES), lambda b, q: (b * nq + q, 0)),
            pl.BlockSpec((DSA_HEADS, 3, TK, TQ), lambda b, q: (1, 0, 0, 0)),
        ],
        out_specs=pl.BlockSpec((TQ, DSA_HEADS * LANES), lambda b, q: (b * nq + q, 0)),
        out_shape=jax.ShapeDtypeStruct((n, DSA_HEADS * LANES), BF16),
        scratch_shapes=[
            pltpu.VMEM((LANES, DSA_HEADS * TQ), BF16),
            pltpu.VMEM((IDX_DIM, IDX_HEADS * TQ), BF16),
            pltpu.VMEM((IDX_HEADS, TQ), F32),
            pltpu.VMEM((seq // TK, TK, TQ), I32),
            pltpu.VMEM((1, TQ), I32),
            pltpu.VMEM((1, DSA_HEADS * TQ), F32),
            pltpu.VMEM((VT_ROWS, DSA_HEADS * TQ), F32),
            pltpu.VMEM((2, TK, DSA_HEADS * TQ), F32),
        ],
        compiler_params=_cparams(2),
        name="dsa",
    )(sqn, skn, vt, proj, proj, iw, bias)


def _rel_bucket(dist):
    exact = REL_BUCKETS // 2
    n = jnp.maximum(dist, 0)
    nf = jnp.maximum(n, exact).astype(F32)
    far = exact + (jnp.log(nf / exact) / math.log(REL_MAX_DIST / exact) * (REL_BUCKETS - exact)).astype(I32)
    return jnp.where(n < exact, n, jnp.minimum(far, REL_BUCKETS - 1))


def _bias_tiles_body(tab_ref, bucket_ref, o_ref):
    hd = pl.program_id(0)
    nh = DIFF_HEADS + DSA_HEADS
    key = lax.broadcasted_iota(I32, (TK, TQ), 0)
    qry = lax.broadcasted_iota(I32, (TK, TQ), 1)
    for tile in range(3):
        bucket = bucket_ref[tile]
        acc = jnp.zeros(bucket.shape, F32)
        for b in range(REL_BUCKETS):
            acc = jnp.where(bucket == b, tab_ref[b * nh + hd], acc)
        if tile == 2:
            acc = jnp.where(key <= qry, acc, NEG)
        o_ref[tile] = acc * LOG2E


def _bias_tables(rel_bias):
    nh = DIFF_HEADS + DSA_HEADS
    assert TK + 1 >= REL_MAX_DIST
    j = jnp.arange(TK, dtype=I32)[:, None]
    i = jnp.arange(TQ, dtype=I32)[None, :]
    dist = jnp.stack([jnp.full((TK, TQ), 2 * TK, I32), i - j + TK, i - j])
    return pl.pallas_call(
        _bias_tiles_body,
        grid=(nh,),
        in_specs=[pl.BlockSpec(memory_space=pltpu.SMEM), pl.BlockSpec((3, TK, TQ), lambda h: (0, 0, 0))],
        out_specs=pl.BlockSpec((None, 3, TK, TQ), lambda h: (h, 0, 0, 0)),
        out_shape=jax.ShapeDtypeStruct((nh, 3, TK, TQ), F32),
        compiler_params=_cparams(1),
        name="bias_tiles",
    )(rel_bias.astype(F32).reshape(-1), _rel_bucket(dist))


def _block_diag_ones(width, group):
    r = np.arange(width)
    return jnp.asarray((r[:, None] // group) == (r[None, :] // group), dtype=BF16)


def _pad_lanes(w, width=LANES):
    return jnp.pad(w, ((0, 0), (0, width - w.shape[1])))


def kernel(x, rel_bias, mix_norm_g, ev_w_in, ev_w_out, gmlp_w_s, gmlp_b_s, gdn_conv_w, gdn_a_log, gdn_dt_bias,
           gdn_norm_g, od_w_in, od_w_out, diff_q_norm_g, diff_k_norm_g, diff_lambda, diff_sub_norm_g,
           dsa_q_norm_g, dsa_k_norm_g, ffn_norm_g, ffn_w_up, ffn_conv_w, ffn_conv_b, ffn_w_down):
    batch, seq, d = x.shape
    n = batch * seq
    depth = mix_norm_g.shape[0]
    assert d == D_MODEL and seq % max(TM_PROJ, TL_PREP, TL_GDN, TQ) == 0
    h = x.reshape(n, d)
    bias_tiles = _bias_tables(rel_bias)
    bd64 = _block_diag_ones(512, DIFF_QK)
    bd128 = _block_diag_ones(512, DSA_DIM)
    gw = GMLP_GROUPS * LANES
    qkv_w = 3 * GDN_HEADS * GDN_DIM

    for layer in range(depth):
        j = layer // 2
        g_mix = mix_norm_g[layer].reshape(1, d)
        if layer % 2 == 0:
            w = ev_w_in[j]
            o_u, o_v, o_qkv = 0, gw, 2 * gw
            o_b = o_qkv + qkv_w
            o_a = o_b + GDN_HEADS
            o_z = o_a + GDN_HEADS
            w_main = jnp.concatenate(
                [w[:, o_u:o_v], w[:, o_v:o_qkv], w[:, o_z:o_z + GDN_HEADS * GDN_DIM], w[:, o_qkv:o_b]],
                axis=1).astype(BF16)
            w_gate = _pad_lanes(w[:, o_b:o_z]).astype(BF16)
            proj, gates = _in_proj(h, g_mix, w_main, w_gate)
            gp = jnp.stack([
                _pad_lanes(jnp.concatenate([jnp.zeros((GDN_HEADS,), F32), gdn_a_log[j]])[None])[0],
                _pad_lanes(jnp.concatenate([jnp.zeros((GDN_HEADS,), F32), gdn_dt_bias[j]])[None])[0]])
            y_a, qkvn, gates2 = _even_prep(proj, gates, gmlp_w_s[j], gmlp_b_s[j].T, gdn_conv_w[j], gp, batch, seq)
            y_b = _gdn(qkvn, gates2, proj, gdn_norm_g[j].reshape(1, GDN_DIM), batch, seq)
            h = _out_proj(h, y_a, y_b, ev_w_out[j].astype(BF16))
        else:
            lambda_init = 0.8 - 0.6 * math.exp(-0.3 * layer)
            w = od_w_in[j]
            c = np.cumsum([0, 512, 512, 512, 512, 128, 128, 512, 64, 8])
            dq, dk, dv, sq, sk, sv, iq, ik, iw = [w[:, c[i]:c[i + 1]] for i in range(9)]
            w_main = jnp.concatenate([dq, dk, dv, sq, iq, sk, sv, ik, ik], axis=1).astype(BF16)
            w_gate = _pad_lanes(iw).astype(BF16)
            proj, iw_out = _in_proj(h, g_mix, w_main, w_gate)
            dqn, dkn, sqn, skn, vt = _odd_prep(
                proj,
                jnp.tile(diff_q_norm_g[j], 2 * DIFF_HEADS)[None], jnp.tile(diff_k_norm_g[j], 2 * DIFF_HEADS)[None],
                jnp.tile(dsa_q_norm_g[j], DSA_HEADS)[None], dsa_k_norm_g[j][None], bd64, bd128, batch, seq)
            y_c = _diff_attn(dqn, dkn, vt, bias_tiles, diff_lambda[j], diff_sub_norm_g[j][None],
                             batch, seq, lambda_init)
            y_d = _dsa(sqn, skn, vt, proj, iw_out, bias_tiles, batch, seq)
            h = _out_proj(h, y_c, y_d, od_w_out[j].astype(BF16))
        h = _ffn(h, ffn_norm_g[layer].reshape(1, d), ffn_w_up[layer].astype(BF16), ffn_conv_w[layer],
                 ffn_conv_b[layer].reshape(1, 2 * D_FF), ffn_w_down[layer].astype(BF16), seq)
    return h.reshape(batch, seq, d)
```

```python
import functools
import math

import numpy as np
import jax
import jax.numpy as jnp
from jax import lax
from jax.experimental import pallas as pl
from jax.experimental.pallas import tpu as pltpu

F32 = jnp.float32
BF16 = jnp.bfloat16
I32 = jnp.int32
I16 = jnp.int16
HIGHEST = lax.Precision.HIGHEST

D_MODEL = 1024
GMLP_GROUPS = 4
GMLP_CHUNK = 128
GDN_HEADS = 4
GDN_DIM = 128
GDN_CHUNK = 64
GDN_CONV = 4
DIFF_HEADS = 4
DIFF_QK = 64
DSA_HEADS = 4
DSA_DIM = 128
IDX_HEADS = 8
IDX_DIM = 64
IDX_TOPK_MAX = 256
REL_BUCKETS = 32
REL_MAX_DIST = 128
D_FF = 2816
FFN_CONV = 3
EPS = 1e-6

LANES = 128
SUBLANES_F32 = 8
SUBLANES_BF16 = 16
VMEM_LIMIT = 56 * 1024 * 1024
VT_ROWS = LANES + SUBLANES_BF16

TM_PROJ = 512
TL_PREP = 512
TL_GDN = 512
GDN_PREP_CHUNKS = 4
TQ = 256
TK = 256
FF_CHUNK = 256
NEG = -1e30
LOG2E = 1.0 / math.log(2.0)
INT_MIN = -2 ** 31


def _cparams(n_axes):
    return pltpu.CompilerParams(dimension_semantics=("arbitrary",) * n_axes, vmem_limit_bytes=VMEM_LIMIT)


def _dot(a, b, **kw):
    return jnp.dot(a, b, preferred_element_type=F32, **kw)


def _dot_nt(a, b, **kw):
    return lax.dot_general(a, b, (((1,), (1,)), ((), ())), preferred_element_type=F32, **kw)


def _dot_tn(a, b, **kw):
    return lax.dot_general(a, b, (((0,), (0,)), ((), ())), preferred_element_type=F32, **kw)


def _rms(x, g):
    return x * lax.rsqrt(jnp.mean(x * x, axis=-1, keepdims=True) + EPS) * g


def _sigmoid(x):
    return 1.0 / (1.0 + jnp.exp(-x))


def _silu(x):
    return x * _sigmoid(x)


def _gelu_tanh(x):
    return 0.5 * x * (1.0 + jnp.tanh(math.sqrt(2.0 / math.pi) * (x + 0.044715 * (x * x * x))))


def _softplus(x):
    return jnp.maximum(x, 0.0) + jnp.log(1.0 + jnp.exp(-jnp.abs(x)))


def _in_proj_body(h_ref, g_ref, w_ref, wg_ref, o_ref, og_ref, *, nout):
    xn = _rms(h_ref[...], g_ref[...]).astype(BF16)
    for c in range(0, nout, 512):
        e = min(c + 512, nout)
        o_ref[:, c:e] = _dot(xn, w_ref[:, c:e]).astype(BF16)
    og_ref[...] = _dot(xn, wg_ref[...])


def _in_proj(h, g, w, wg):
    n, d = h.shape
    nout = w.shape[1]
    return pl.pallas_call(
        functools.partial(_in_proj_body, nout=nout),
        grid=(n // TM_PROJ,),
        in_specs=[
            pl.BlockSpec((TM_PROJ, d), lambda i: (i, 0)),
            pl.BlockSpec((1, d), lambda i: (0, 0)),
            pl.BlockSpec((d, nout), lambda i: (0, 0)),
            pl.BlockSpec((d, LANES), lambda i: (0, 0)),
        ],
        out_specs=[
            pl.BlockSpec((TM_PROJ, nout), lambda i: (i, 0)),
            pl.BlockSpec((TM_PROJ, LANES), lambda i: (i, 0)),
        ],
        out_shape=[jax.ShapeDtypeStruct((n, nout), BF16), jax.ShapeDtypeStruct((n, LANES), F32)],
        compiler_params=_cparams(1),
        name="in_proj",
    )(h, g, w, wg)


def _out_proj_body(h_ref, y1_ref, y2_ref, w_ref, o_ref):
    half = y1_ref.shape[1]
    o_ref[...] = h_ref[...] + _dot(y1_ref[...], w_ref[:half, :]) + _dot(y2_ref[...], w_ref[half:, :])


def _out_proj(h, y1, y2, w):
    n, d = h.shape
    half = y1.shape[1]
    return pl.pallas_call(
        _out_proj_body,
        grid=(n // TM_PROJ,),
        in_specs=[
            pl.BlockSpec((TM_PROJ, d), lambda i: (i, 0)),
            pl.BlockSpec((TM_PROJ, half), lambda i: (i, 0)),
            pl.BlockSpec((TM_PROJ, half), lambda i: (i, 0)),
            pl.BlockSpec((2 * half, d), lambda i: (0, 0)),
        ],
        out_specs=pl.BlockSpec((TM_PROJ, d), lambda i: (i, 0)),
        out_shape=jax.ShapeDtypeStruct((n, d), F32),
        compiler_params=_cparams(1),
        name="out_proj",
    )(h, y1, y2, w)


def _ffn_body(h_ref, halo_ref, g_ref, wup_ref, cw_ref, cb_ref, wdn_ref, o_ref, xn_ref, act_ref, *, tiles_per_seq):
    tm = h_ref.shape[0]
    i = pl.program_id(0)
    h = h_ref[...]
    g = g_ref[...]
    hb = SUBLANES_BF16
    xn_ref[hb:, :] = _rms(h, g).astype(BF16)
    keep = jnp.where(i % tiles_per_seq == 0, 0.0, 1.0)
    xn_ref[:hb, :] = (_rms(halo_ref[...], g) * keep).astype(BF16)
    xn = xn_ref[...]

    def up_pair(c):
        return [_dot(xn, wup_ref[:, base:base + FF_CHUNK]) for base in (c, D_FF + c)]

    def conv(up, base):
        cw = cw_ref[:, base:base + FF_CHUNK]
        return (cw[0:1] * up[hb - 2:tm + hb - 2] + cw[1:2] * up[hb - 1:tm + hb - 1] + cw[2:3] * up[hb:tm + hb]
                + cb_ref[:, base:base + FF_CHUNK])

    chunks = list(range(0, D_FF, FF_CHUNK))
    ups = up_pair(chunks[0])
    for n, c in enumerate(chunks):
        nxt = up_pair(chunks[n + 1]) if n + 1 < len(chunks) else None
        act_ref[:, c:c + FF_CHUNK] = (_silu(conv(ups[0], c)) * conv(ups[1], D_FF + c)).astype(BF16)
        ups = nxt
    o_ref[...] = h + _dot(act_ref[...], wdn_ref[...])


def _ffn(h, g, wup, cw, cb, wdn, seq):
    n, d = h.shape
    tiles_per_seq = seq // TM_PROJ
    halo_blocks = TM_PROJ // SUBLANES_BF16
    const = dict(pipeline_mode=pl.Buffered(1))
    return pl.pallas_call(
        functools.partial(_ffn_body, tiles_per_seq=tiles_per_seq),
        grid=(n // TM_PROJ,),
        in_specs=[
            pl.BlockSpec((TM_PROJ, d), lambda i: (i, 0)),
            pl.BlockSpec((SUBLANES_BF16, d), lambda i: (jnp.maximum(i * halo_blocks - 1, 0), 0)),
            pl.BlockSpec((1, d), lambda i: (0, 0)),
            pl.BlockSpec((d, 2 * D_FF), lambda i: (0, 0), **const),
            pl.BlockSpec((FFN_CONV, 2 * D_FF), lambda i: (0, 0)),
            pl.BlockSpec((1, 2 * D_FF), lambda i: (0, 0)),
            pl.BlockSpec((D_FF, d), lambda i: (0, 0), **const),
        ],
        out_specs=pl.BlockSpec((TM_PROJ, d), lambda i: (i, 0)),
        out_shape=jax.ShapeDtypeStruct((n, d), F32),
        scratch_shapes=[pltpu.VMEM((TM_PROJ + SUBLANES_BF16, d), BF16), pltpu.VMEM((TM_PROJ, D_FF), BF16)],
        compiler_params=_cparams(1),
        name="ffn",
    )(h, h, g, wup, cw, cb, wdn)


def _even_prep_body(uv_ref, qkv_ref, halo_ref, gate_ref, ws_ref, bs_ref, cw_ref, gp_ref,
                    ya_ref, qkvo_ref, gout_ref):
    tl = uv_ref.shape[0]
    t = pl.program_id(1)
    gw = LANES
    row = lax.broadcasted_iota(I32, (GMLP_CHUNK, GMLP_CHUNK), 0)
    col = lax.broadcasted_iota(I32, (GMLP_CHUNK, GMLP_CHUNK), 1)
    for gi in range(GMLP_GROUPS):
        u = _gelu_tanh(uv_ref[:, gi * gw:(gi + 1) * gw].astype(F32))
        v = _gelu_tanh(uv_ref[:, (GMLP_GROUPS + gi) * gw:(GMLP_GROUPS + gi + 1) * gw].astype(F32))
        vc = v - jnp.mean(v, axis=-1, keepdims=True)
        vn = (vc * lax.rsqrt(jnp.mean(vc * vc, axis=-1, keepdims=True) + EPS)).astype(BF16)
        w = jnp.where(col <= row, ws_ref[gi], 0.0).astype(BF16)
        b = bs_ref[:, gi:gi + 1]
        for c in range(tl // GMLP_CHUNK):
            r0 = c * GMLP_CHUNK
            mixed = _dot(w, vn[r0:r0 + GMLP_CHUNK]) + b
            ya_ref[r0:r0 + GMLP_CHUNK, gi * gw:(gi + 1) * gw] = (u[r0:r0 + GMLP_CHUNK] * mixed).astype(BF16)
    keep = jnp.where(t == 0, 0.0, 1.0)
    hb = SUBLANES_BF16
    for j in range(3 * GDN_HEADS):
        sl = slice(j * gw, (j + 1) * gw)
        x = jnp.concatenate([halo_ref[:, sl].astype(F32) * keep, qkv_ref[:, sl].astype(F32)], axis=0)
        cw = cw_ref[:, sl]
        y = cw[0:1] * x[hb - 3:tl + hb - 3]
        for k in range(1, GDN_CONV):
            y = y + cw[k:k + 1] * x[hb - 3 + k:tl + hb - 3 + k]
        y = _silu(y)
        if j < 2 * GDN_HEADS:
            y = y * lax.rsqrt(jnp.sum(y * y, axis=-1, keepdims=True) + EPS)
        if j < GDN_HEADS:
            y = y * (GDN_DIM ** -0.5)
        qkvo_ref[:, sl] = y.astype(BF16)
    x = gate_ref[...]
    lane = lax.broadcasted_iota(I32, x.shape, 1)
    beta = _sigmoid(x)
    gdec = -jnp.exp(gp_ref[0:1, :]) * _softplus(x + gp_ref[1:2, :])
    gout_ref[...] = jnp.where(lane < GDN_HEADS, beta, gdec)


def _even_prep(proj, gates, ws, bs_t, cw, gp, batch, seq):
    n = proj.shape[0]
    nt = seq // TL_PREP
    halo_blocks = TL_PREP // SUBLANES_BF16
    qkv_w = 3 * GDN_HEADS * GDN_DIM
    row = lambda b, t: b * nt + t
    return pl.pallas_call(
        _even_prep_body,
        grid=(batch, nt),
        in_specs=[
            pl.BlockSpec((TL_PREP, 2 * GMLP_GROUPS * LANES), lambda b, t: (row(b, t), 0)),
            pl.BlockSpec((TL_PREP, qkv_w), lambda b, t: (row(b, t), 1)),
            pl.BlockSpec((SUBLANES_BF16, qkv_w), lambda b, t: (jnp.maximum(row(b, t) * halo_blocks - 1, 0), 1)),
            pl.BlockSpec((TL_PREP, LANES), lambda b, t: (row(b, t), 0)),
            pl.BlockSpec((GMLP_GROUPS, GMLP_CHUNK, GMLP_CHUNK), lambda b, t: (0, 0, 0)),
            pl.BlockSpec((GMLP_CHUNK, GMLP_GROUPS), lambda b, t: (0, 0)),
            pl.BlockSpec((GDN_CONV, qkv_w), lambda b, t: (0, 0)),
            pl.BlockSpec((2, LANES), lambda b, t: (0, 0)),
        ],
        out_specs=[
            pl.BlockSpec((TL_PREP, GMLP_GROUPS * LANES), lambda b, t: (row(b, t), 0)),
            pl.BlockSpec((TL_PREP, qkv_w), lambda b, t: (row(b, t), 0)),
            pl.BlockSpec((TL_PREP, LANES), lambda b, t: (row(b, t), 0)),
        ],
        out_shape=[
            jax.ShapeDtypeStruct((n, GMLP_GROUPS * LANES), BF16),
            jax.ShapeDtypeStruct((n, qkv_w), BF16),
            jax.ShapeDtypeStruct((n, LANES), F32),
        ],
        compiler_params=_cparams(2),
        name="even_prep",
    )(proj, proj, proj, gates, ws, bs_t, cw, gp)


def _split_bf16(x):
    hi = x.astype(BF16)
    return hi, (x - hi.astype(F32)).astype(BF16)


def _mm3(a, b):
    ah, al = _split_bf16(a)
    bh, bl = _split_bf16(b)
    return _dot(ah, bh) + (_dot(ah, bl) + _dot(al, bh))


def _unit_lower_inverses(lows):
    c = lows[0].shape[0]
    row = lax.broadcasted_iota(I32, (c, c), 0)
    col = lax.broadcasted_iota(I32, (c, c), 1)
    eye = jnp.where(row == col, 1.0, 0.0)
    same16 = jnp.right_shift(row, 4) == jnp.right_shift(col, 4)
    same32 = jnp.right_shift(row, 5) == jnp.right_shift(col, 5)
    ps = [jnp.where(same16, -low, 0.0) for low in lows]
    xs = [eye + p for p in ps]
    for _ in range(3):
        ps = [_mm3(p, p) for p in ps]
        xs = [x + _mm3(x, p) for x, p in zip(xs, ps)]
    mid = jnp.logical_and(same32, jnp.logical_not(same16))
    mm1 = lambda a, b: _dot(a.astype(BF16), b.astype(BF16))
    ys = [mm1(jnp.where(mid, low, 0.0), x) for low, x in zip(lows, xs)]
    xs = [x - mm1(x, y) for x, y in zip(xs, ys)]
    ys = [mm1(jnp.where(same32, 0.0, low), x) for low, x in zip(lows, xs)]
    return [x - mm1(x, y) for x, y in zip(xs, ys)]


def _gdn_body(qkv_ref, gate_ref, z_ref, gn_ref, o_ref,
              s_ref, u_ref, w_ref, qg_ref, kdt_ref, intra_ref, egl_ref):
    tl = qkv_ref.shape[0]
    c = GDN_CHUNK
    d = GDN_DIM
    heads = range(GDN_HEADS)
    t = pl.program_id(1)

    @pl.when(t == 0)
    def _():
        s_ref[...] = jnp.zeros_like(s_ref)

    row = lax.broadcasted_iota(I32, (c, c), 0)
    col = lax.broadcasted_iota(I32, (c, c), 1)
    incl = col <= row
    strict = col < row
    lmat = jnp.where(incl, 1.0, 0.0)
    gn = gn_ref[...]

    def prepare(it, carry):
        cis = [it * GDN_PREP_CHUNKS + j for j in range(GDN_PREP_CHUNKS)]
        rows = [pl.ds(pl.multiple_of(ci * c, c), c) for ci in cis]
        gates = [gate_ref[r, :] for r in rows]
        gcum = [_dot(lmat, g, precision=HIGHEST) for g in gates]
        pairs = [(j, hd) for j in range(GDN_PREP_CHUNKS) for hd in heads]
        qs = [qkv_ref[rows[j], hd * d:(hd + 1) * d] for j, hd in pairs]
        ks = [qkv_ref[rows[j], (GDN_HEADS + hd) * d:(GDN_HEADS + hd + 1) * d] for j, hd in pairs]
        vs = [qkv_ref[rows[j], (2 * GDN_HEADS + hd) * d:(2 * GDN_HEADS + hd + 1) * d] for j, hd in pairs]
        beta = [jnp.broadcast_to(gates[j][:, hd:hd + 1], (c, d)) for j, hd in pairs]
        gc = [jnp.broadcast_to(gcum[j][:, GDN_HEADS + hd:GDN_HEADS + hd + 1], (c, d)) for j, hd in pairs]
        decay = [jnp.where(incl, jnp.exp(jnp.where(incl, g[:, :c] - g.T[:c, :], 0.0)), 0.0) for g in gc]
        kb = [k.astype(F32) * b for k, b in zip(ks, beta)]
        lows = [jnp.where(strict, _dot_nt(x.astype(BF16), k) * dc, 0.0) for x, k, dc in zip(kb, ks, decay)]
        tinv = [x.astype(BF16) for x in _unit_lower_inverses(lows)]
        egc = [jnp.exp(g) for g in gc]
        rhs = [jnp.concatenate([(v.astype(F32) * b).astype(BF16), (x * e).astype(BF16)], axis=1)
               for v, b, x, e in zip(vs, beta, kb, egc)]
        uw = [_dot(ti, r) for ti, r in zip(tinv, rhs)]
        qk = [_dot_nt(q, k) for q, k in zip(qs, ks)]
        for n, (j, hd) in enumerate(pairs):
            ci = cis[j]
            u_ref[ci, hd] = uw[n][:, :d]
            w_ref[ci, hd] = uw[n][:, d:].astype(BF16)
            intra_ref[ci, hd] = jnp.where(incl, qk[n] * decay[n], 0.0).astype(BF16)
            qg_ref[ci, hd] = (qs[n].astype(F32) * egc[n]).astype(BF16)
            g_last = gc[n][c - 1:c, :]
            kdt_ref[ci, hd] = (ks[n].astype(F32) * jnp.exp(g_last - gc[n])).T.astype(BF16)
            egl_ref[ci, hd] = jnp.exp(g_last)
        return carry

    lax.fori_loop(0, tl // c // GDN_PREP_CHUNKS, prepare, 0)

    def scan(ci, carry):
        r0 = pl.multiple_of(ci * c, c)
        states = [s_ref[hd] for hd in heads]
        sb = [s.astype(BF16) for s in states]
        ws = [_dot(w_ref[ci, hd], sb[hd]) for hd in heads]
        qsd = [_dot(qg_ref[ci, hd], sb[hd]) for hd in heads]
        vnb = [(u_ref[ci, hd] - ws[hd]).astype(BF16) for hd in heads]
        for hd in heads:
            s_ref[hd] = states[hd] * egl_ref[ci, hd] + _dot(kdt_ref[ci, hd], vnb[hd])
        for hd in heads:
            out = qsd[hd] + _dot(intra_ref[ci, hd], vnb[hd])
            z = z_ref[pl.ds(r0, c), hd * d:(hd + 1) * d].astype(F32)
            o_ref[pl.ds(r0, c), hd * d:(hd + 1) * d] = (_rms(out, gn) * _silu(z)).astype(BF16)
        return carry

    lax.fori_loop(0, tl // c, scan, 0)


def _gdn(qkv, gates, proj, gn, batch, seq):
    n = qkv.shape[0]
    nt = seq // TL_GDN
    nc = TL_GDN // GDN_CHUNK
    width = GDN_HEADS * GDN_DIM
    row = lambda b, t: b * nt + t
    return pl.pallas_call(
        _gdn_body,
        grid=(batch, nt),
        in_specs=[
            pl.BlockSpec((TL_GDN, 3 * width), lambda b, t: (row(b, t), 0)),
            pl.BlockSpec((TL_GDN, LANES), lambda b, t: (row(b, t), 0)),
            pl.BlockSpec((TL_GDN, width), lambda b, t: (row(b, t), 2)),
            pl.BlockSpec((1, GDN_DIM), lambda b, t: (0, 0)),
        ],
        out_specs=pl.BlockSpec((TL_GDN, width), lambda b, t: (row(b, t), 0)),
        out_shape=jax.ShapeDtypeStruct((n, width), BF16),
        scratch_shapes=[
            pltpu.VMEM((GDN_HEADS, GDN_DIM, GDN_DIM), F32),
            pltpu.VMEM((nc, GDN_HEADS, GDN_CHUNK, GDN_DIM), F32),
            pltpu.VMEM((nc, GDN_HEADS, GDN_CHUNK, GDN_DIM), BF16),
            pltpu.VMEM((nc, GDN_HEADS, GDN_CHUNK, GDN_DIM), BF16),
            pltpu.VMEM((nc, GDN_HEADS, GDN_DIM, GDN_CHUNK), BF16),
            pltpu.VMEM((nc, GDN_HEADS, GDN_CHUNK, GDN_CHUNK), BF16),
            pltpu.VMEM((nc, GDN_HEADS, 1, GDN_DIM), F32),
        ],
        compiler_params=_cparams(2),
        name="gdn",
    )(qkv, gates, proj, gn)


def _group_mean_sq(x, ones_bd, group):
    x2 = x * x
    hi = x2.astype(BF16)
    lo = (x2 - hi.astype(F32)).astype(BF16)
    return (_dot(hi, ones_bd) + _dot(lo, ones_bd)) * (1.0 / group)


def _odd_prep_body(dq_ref, dk_ref, dv_ref, sq_ref, sk_ref, sv_ref, gq_ref, gk_ref, gsq_ref, gsk_ref, bd64_ref,
                   bd128_ref, dqo_ref, dko_ref, sqo_ref, sko_ref, vt_ref):
    ones = jnp.ones((VT_ROWS - LANES, TK), BF16)
    for j in range(DIFF_HEADS + 1):
        src = sv_ref[...] if j == DIFF_HEADS else dv_ref[:, j * LANES:(j + 1) * LANES]
        xt = src.astype(F32).T
        for c in range(xt.shape[1] // TK):
            vt_ref[j, c, :LANES, :] = xt[:, c * TK:(c + 1) * TK].astype(BF16)
            vt_ref[j, c, LANES:, :] = ones
    bd64 = bd64_ref[...]
    x = dq_ref[...].astype(F32)
    dqo_ref[...] = (x * lax.rsqrt(_group_mean_sq(x, bd64, DIFF_QK) + EPS) * gq_ref[...]
                    * (DIFF_QK ** -0.5 * LOG2E)).astype(BF16)
    x = dk_ref[...].astype(F32)
    dko_ref[...] = (x * lax.rsqrt(_group_mean_sq(x, bd64, DIFF_QK) + EPS) * gk_ref[...]).astype(BF16)
    x = sq_ref[...].astype(F32)
    sqo_ref[...] = (x * lax.rsqrt(_group_mean_sq(x, bd128_ref[...], DSA_DIM) + EPS) * gsq_ref[...]
                    * (DSA_DIM ** -0.5 * LOG2E)).astype(BF16)
    x = sk_ref[...].astype(F32)
    sko_ref[...] = _rms(x, gsk_ref[...]).astype(BF16)


def _odd_prep(proj, gq, gk, gsq, gsk, bd64, bd128, batch, seq):
    n = proj.shape[0]
    w = 512
    nt = seq // TL_PREP
    kt = TL_PREP // TK
    full = lambda shape: pl.BlockSpec(shape, lambda b, t: (0,) * len(shape))
    row = lambda b, t: b * nt + t
    return pl.pallas_call(
        _odd_prep_body,
        grid=(batch, nt),
        in_specs=[
            pl.BlockSpec((TL_PREP, w), lambda b, t: (row(b, t), 0)),
            pl.BlockSpec((TL_PREP, w), lambda b, t: (row(b, t), 1)),
            pl.BlockSpec((TL_PREP, w), lambda b, t: (row(b, t), 2)),
            pl.BlockSpec((TL_PREP, w), lambda b, t: (row(b, t), 3)),
            pl.BlockSpec((TL_PREP, LANES), lambda b, t: (row(b, t), 5 * w // LANES)),
            pl.BlockSpec((TL_PREP, LANES), lambda b, t: (row(b, t), 5 * w // LANES + 1)),
            full((1, w)), full((1, w)), full((1, w)), full((1, LANES)), full((w, w)), full((w, w)),
        ],
        out_specs=[
            pl.BlockSpec((TL_PREP, w), lambda b, t: (row(b, t), 0)),
            pl.BlockSpec((TL_PREP, w), lambda b, t: (row(b, t), 0)),
            pl.BlockSpec((TL_PREP, w), lambda b, t: (row(b, t), 0)),
            pl.BlockSpec((TL_PREP, LANES), lambda b, t: (row(b, t), 0)),
            pl.BlockSpec((None, DIFF_HEADS + 1, kt, VT_ROWS, TK), lambda b, t: (b, 0, t, 0, 0)),
        ],
        out_shape=[
            jax.ShapeDtypeStruct((n, w), BF16), jax.ShapeDtypeStruct((n, w), BF16),
            jax.ShapeDtypeStruct((n, w), BF16), jax.ShapeDtypeStruct((n, LANES), BF16),
            jax.ShapeDtypeStruct((batch, DIFF_HEADS + 1, seq // TK, VT_ROWS, TK), BF16),
        ],
        compiler_params=_cparams(2),
        name="odd_prep",
    )(proj, proj, proj, proj, proj, proj, gq, gk, gsq, gsk, bd64, bd128)


def _bias_tile_index(kb, qb):
    return jnp.clip(kb - (qb - 2), 0, 2)


def _softmax_step_t(s_t, vt, m_ref, acc_ref):
    for g in range(s_t.shape[1] // LANES):
        sl = slice(g * LANES, (g + 1) * LANES)
        s = s_t[:, sl]
        m_prev = m_ref[:, sl]
        m_new = jnp.maximum(m_prev, jnp.max(s, axis=0, keepdims=True))
        alpha = jnp.exp2(m_prev - m_new)
        p = jnp.exp2(s - m_new)
        acc_ref[:, sl] = alpha * acc_ref[:, sl] + _dot(vt, p.astype(BF16))
        m_ref[:, sl] = m_new


def _attend_tiles(logits, values, nblk, s_ref, m_ref, acc_ref):
    streams = range(len(logits))
    m_ref[...] = jnp.full_like(m_ref, NEG)
    acc_ref[...] = jnp.zeros_like(acc_ref)
    for i in streams:
        s_ref[i, 0] = logits[i](0)

    def pair(it, carry):
        kb = 2 * it
        for i in streams:
            s_ref[i, 1] = logits[i](kb + 1)
        for i in streams:
            _softmax_step_t(s_ref[i, 0], values[i](kb), m_ref.at[i], acc_ref.at[i])
        for i in streams:
            s_ref[i, 0] = logits[i](jnp.minimum(kb + 2, nblk - 1))
        for i in streams:
            _softmax_step_t(s_ref[i, 1], values[i](kb + 1), m_ref.at[i], acc_ref.at[i])
        return carry

    lax.fori_loop(0, nblk // 2, pair, 0)

    @pl.when(nblk % 2 == 1)
    def _():
        for i in streams:
            _softmax_step_t(s_ref[i, 0], values[i](nblk - 1), m_ref.at[i], acc_ref.at[i])


def _diff_attn_body(q_ref, k_ref, vt_ref, bias_ref, lam_ref, sg_ref, o_ref,
                    qt_ref, m_ref, acc_ref, s_ref, *, lambda_init):
    qi = pl.program_id(1)
    nblk = qi + 1
    heads = range(DIFF_HEADS)
    for hd in heads:
        qt = q_ref[:, hd * LANES:(hd + 1) * LANES].astype(F32).T
        dim = lax.broadcasted_iota(I32, qt.shape, 0)
        qt_ref[hd, :, :TQ] = jnp.where(dim < DIFF_QK, qt, 0.0).astype(BF16)
        qt_ref[hd, :, TQ:] = jnp.where(dim >= DIFF_QK, qt, 0.0).astype(BF16)

    def logits_of(hd):
        def logits(kb):
            r0 = pl.multiple_of(kb * TK, TK)
            b = bias_ref[hd, _bias_tile_index(kb, qi)]
            s = _dot(k_ref[pl.ds(r0, TK), hd * LANES:(hd + 1) * LANES], qt_ref[hd])
            return s + jnp.concatenate([b, b], axis=1)
        return logits

    _attend_tiles([logits_of(hd) for hd in heads], [lambda kb, hd=hd: vt_ref[hd, kb] for hd in heads],
                  nblk, s_ref, m_ref, acc_ref)

    lf = lam_ref[...]
    lam = (jnp.exp(jnp.sum(lf[0:1] * lf[1:2], axis=-1, keepdims=True))
           - jnp.exp(jnp.sum(lf[2:3] * lf[3:4], axis=-1, keepdims=True)) + lambda_init)
    for hd in heads:
        den = acc_ref[hd, LANES:LANES + 1, :]
        ot = (acc_ref[hd, :LANES, :TQ] / den[:, :TQ] - lam * (acc_ref[hd, :LANES, TQ:] / den[:, TQ:]))
        ot = ot * lax.rsqrt(jnp.mean(ot * ot, axis=0, keepdims=True) + EPS)
        o_ref[:, hd * LANES:(hd + 1) * LANES] = (ot.T * sg_ref[...] * (1.0 - lambda_init)).astype(BF16)


def _diff_attn(dqn, dkn, vt, bias, lam_p, sub_g, batch, seq, lambda_init):
    n = dqn.shape[0]
    nq = seq // TQ
    width = DIFF_HEADS * LANES
    return pl.pallas_call(
        functools.partial(_diff_attn_body, lambda_init=lambda_init),
        grid=(batch, nq),
        in_specs=[
            pl.BlockSpec((TQ, width), lambda b, q: (b * nq + q, 0)),
            pl.BlockSpec((seq, width), lambda b, q: (b, 0)),
            pl.BlockSpec((None, DIFF_HEADS + 1, seq // TK, VT_ROWS, TK), lambda b, q: (b, 0, 0, 0, 0)),
            pl.BlockSpec((DIFF_HEADS, 3, TK, TQ), lambda b, q: (0, 0, 0, 0)),
            pl.BlockSpec((4, DIFF_QK), lambda b, q: (0, 0)),
            pl.BlockSpec((1, LANES), lambda b, q: (0, 0)),
        ],
        out_specs=pl.BlockSpec((TQ, width), lambda b, q: (b * nq + q, 0)),
        out_shape=jax.ShapeDtypeStruct((n, width), BF16),
        scratch_shapes=[
            pltpu.VMEM((DIFF_HEADS, LANES, 2 * TQ), BF16),
            pltpu.VMEM((DIFF_HEADS, 1, 2 * TQ), F32),
            pltpu.VMEM((DIFF_HEADS, VT_ROWS, 2 * TQ), F32),
            pltpu.VMEM((DIFF_HEADS, 2, TK, 2 * TQ), F32),
        ],
        compiler_params=_cparams(2),
        name="diff_attn",
    )(dqn, dkn, vt, bias, lam_p, sub_g)


def _dsa_body(q_ref, k_ref, vt_ref, iq_ref, ik_ref, iw_ref, bias_ref, o_ref,
              qt_ref, qit_ref, wt_ref, keys_ref, hi_ref, lo_ref, t_ref, m_ref, acc_ref, s_ref, *, top_k, pos_bits):
    qb = pl.program_id(1)
    nkb = qb + 1

    for p in range(IDX_HEADS // 2):
        gt = iq_ref[:, p * LANES:(p + 1) * LANES].astype(F32).T.astype(BF16)
        qit_ref[:, (2 * p) * TQ:(2 * p + 1) * TQ] = gt[:IDX_DIM]
        qit_ref[:, (2 * p + 1) * TQ:(2 * p + 2) * TQ] = gt[IDX_DIM:]
    for hd in range(DSA_HEADS):
        qt_ref[:, hd * TQ:(hd + 1) * TQ] = q_ref[:, hd * LANES:(hd + 1) * LANES].astype(F32).T.astype(BF16)
    wscale = (IDX_HEADS ** -0.5) * (IDX_DIM ** -0.5)
    wt_ref[...] = (iw_ref[...] * wscale).T[:IDX_HEADS, :]

    key_i = lax.broadcasted_iota(I32, (TK, TQ), 0)
    qry_i = lax.broadcasted_iota(I32, (TK, TQ), 1)
    causal = key_i <= qry_i

    def index_block(kb, masked):
        r0 = pl.multiple_of(kb * TK, TK)
        ik = ik_ref[pl.ds(r0, TK), :IDX_DIM]
        idx = jnp.zeros((TK, TQ), F32)
        for hi in range(IDX_HEADS):
            s = _dot(ik, qit_ref[:, hi * TQ:(hi + 1) * TQ])
            idx = idx + jnp.maximum(s, 0.0) * wt_ref[hi:hi + 1, :]
        idx = jnp.where(idx == 0.0, 0.0, idx)
        bits = pltpu.bitcast(idx, I32)
        key = jnp.where(bits < 0, bits ^ jnp.int32(0x7FFFFFFF), bits)
        if masked:
            key = jnp.where(causal, key, jnp.int32(INT_MIN))
        keys_ref[kb] = key
        hi_ref[kb] = jnp.right_shift(key, 16).astype(I16)
        lo_ref[kb] = ((key & 0xFFFF) - 0x8000).astype(I16)

    def index_loop(kb, carry):
        index_block(kb, False)
        return carry

    lax.fori_loop(0, nkb - 1, index_loop, 0)
    index_block(nkb - 1, True)

    def count(pred):
        def body(kb, acc):
            x = jnp.where(pred(keys_ref[kb], key_i + kb * TK), 1.0, 0.0)
            parts = [x[i * SUBLANES_F32:(i + 1) * SUBLANES_F32] for i in range(TK // SUBLANES_F32)]
            while len(parts) > 1:
                parts = [parts[i] + parts[i + 1] for i in range(0, len(parts), 2)]
            return acc + parts[0]
        acc = lax.fori_loop(0, nkb, body, jnp.zeros((SUBLANES_F32, TQ), F32))
        return jnp.sum(acc, axis=0, keepdims=True)

    kf = float(top_k)
    rows16 = SUBLANES_BF16
    i16_min = -2 ** 15

    def count16(ref, cand, strict=False):
        c16 = jnp.broadcast_to(cand, (rows16, TQ)).astype(I16)

        def body(kb, acc):
            x = ref[kb]
            parts = []
            for i in range(TK // rows16):
                slab = x[i * rows16:(i + 1) * rows16]
                hit = slab > c16 if strict else slab >= c16
                parts.append(jnp.where(hit, jnp.int16(1), jnp.int16(0)))
            while len(parts) > 1:
                parts = [parts[i] + parts[i + 1] for i in range(0, len(parts), 2)]
            return acc + parts[0]

        acc = lax.fori_loop(0, nkb, body, jnp.zeros((rows16, TQ), I16))
        return jnp.sum(acc.astype(F32), axis=0, keepdims=True)

    def digit_select(ref, need, fallback):
        c0 = count16(ref, jnp.zeros((1, TQ), I32))
        ok0 = c0 >= need
        d0 = jnp.where(ok0, jnp.int32(0), jnp.int32(i16_min))
        n0 = jnp.where(ok0, c0, fallback)

        def bit_step(i, carry):
            d, n = carry
            cand = d | jnp.left_shift(jnp.int32(1), 14 - i)
            cnt = count16(ref, cand)
            ok = cnt >= need
            return jnp.where(ok, cand, d), jnp.where(ok, cnt, n)

        return lax.fori_loop(0, 15, bit_step, (d0, n0))

    total = (nkb * TK).astype(F32) * jnp.ones((1, TQ), F32)
    t_hi, n_hi = digit_select(hi_ref, kf, total)
    n_above = count16(hi_ref, t_hi, strict=True)
    t_hi16 = jnp.broadcast_to(t_hi, (rows16, TQ)).astype(I16)

    def restrict(kb, carry):
        for i in range(TK // rows16):
            sl = slice(i * rows16, (i + 1) * rows16)
            lo_ref[kb, sl, :] = jnp.where(hi_ref[kb, sl, :] == t_hi16, lo_ref[kb, sl, :], jnp.int16(i16_min))
        return carry

    lax.fori_loop(0, nkb, restrict, 0)
    t_lo, n_lo = digit_select(lo_ref, kf - n_above, n_hi - n_above)
    t = jnp.left_shift(t_hi, 16) + (t_lo + 0x8000)
    cge = n_above + n_lo
    t_ref[...] = t

    @pl.when(jnp.max(cge) > kf)
    def _():
        tt = t_ref[...]
        r = kf - count(lambda kk, pos: kk > tt)

        def pos_step(i, pcut):
            cand = pcut | jnp.left_shift(jnp.int32(1), pos_bits - 1 - i)
            tied_before = count(lambda kk, pos: jnp.logical_and(kk == tt, pos < cand))
            return jnp.where(tied_before <= r, cand, pcut)

        pcut = lax.fori_loop(0, pos_bits, pos_step, jnp.zeros((1, TQ), I32))

        def demote(kb, carry):
            kk = keys_ref[kb]
            drop = jnp.logical_and(kk == tt, key_i + kb * TK >= pcut)
            keys_ref[kb] = jnp.where(drop, kk - 1, kk)
            return carry

        lax.fori_loop(0, nkb, demote, 0)

    def logits_of(hd):
        def logits(kb):
            r0 = pl.multiple_of(kb * TK, TK)
            sel = keys_ref[kb] >= t_ref[...]
            s = _dot(k_ref[pl.ds(r0, TK), :], qt_ref[:, hd * TQ:(hd + 1) * TQ])
            return jnp.where(sel, s + bias_ref[hd, _bias_tile_index(kb, qb)], NEG)
        return logits

    heads = range(DSA_HEADS)
    _attend_tiles([logits_of(hd) for hd in heads], [lambda kb: vt_ref[kb]] * DSA_HEADS, nkb, s_ref, m_ref, acc_ref)

    for hd in heads:
        o_ref[:, hd * LANES:(hd + 1) * LANES] = (acc_ref[hd, :LANES, :] / acc_ref[hd, LANES:LANES + 1, :]).T.astype(BF16)


def _dsa(sqn, skn, vt, proj, iw, bias, batch, seq):
    n = sqn.shape[0]
    nq = seq // TQ
    top_k = min(IDX_TOPK_MAX, seq // 4)
    pos_bits = int(seq).bit_length()
    iq_block = 4
    ik_block = (5 * 512 + 2 * LANES) // LANES
    return pl.pallas_call(
        functools.partial(_dsa_body, top_k=top_k, pos_bits=pos_bits),
        grid=(batch, nq),
        in_specs=[
            pl.BlockSpec((TQ, DSA_HEADS * LANES), lambda b, q: (b * nq + q, 0)),
            pl.BlockSpec((seq, LANES), lambda b, q: (b, 0)),
            pl.BlockSpec((None, None, seq // TK, VT_ROWS, TK), lambda b, q: (b, DIFF_HEADS, 0, 0, 0)),
            pl.BlockSpec((TQ, IDX_HEADS * IDX_DIM), lambda b, q: (b * nq + q, iq_block)),
            pl.BlockSpec((seq, LANES), lambda b, q: (b, ik_block)),
            pl.BlockSpec((TQ, LANES), lambda b, q: (b * nq + q, 0)),
            pl.BlockSpec((DSA_HEADS, 3, TK, TQ), lambda b, q: (1, 0, 0, 0)),
        ],
        out_specs=pl.BlockSpec((TQ, DSA_HEADS * LANES), lambda b, q: (b * nq + q, 0)),
        out_shape=jax.ShapeDtypeStruct((n, DSA_HEADS * LANES), BF16),
        scratch_shapes=[
            pltpu.VMEM((LANES, DSA_HEADS * TQ), BF16),
            pltpu.VMEM((IDX_DIM, IDX_HEADS * TQ), BF16),
            pltpu.VMEM((IDX_HEADS, TQ), F32),
            pltpu.VMEM((seq // TK, TK, TQ), I32),
            pltpu.VMEM((seq // TK, TK, TQ), I16),
            pltpu.VMEM((seq // TK, TK, TQ), I16),
            pltpu.VMEM((1, TQ), I32),
            pltpu.VMEM((DSA_HEADS, 1, TQ), F32),
            pltpu.VMEM((DSA_HEADS, VT_ROWS, TQ), F32),
            pltpu.VMEM((DSA_HEADS, 2, TK, TQ), F32),
        ],
        compiler_params=_cparams(2),
        name="dsa",
    )(sqn, skn, vt, proj, proj, iw, bias)


def _rel_bucket(dist):
    exact = REL_BUCKETS // 2
    n = jnp.maximum(dist, 0)
    nf = jnp.maximum(n, exact).astype(F32)
    far = exact + (jnp.log(nf / exact) / math.log(REL_MAX_DIST / exact) * (REL_BUCKETS - exact)).astype(I32)
    return jnp.where(n < exact, n, jnp.minimum(far, REL_BUCKETS - 1))


def _bias_tiles_body(tab_ref, bucket_ref, o_ref):
    hd = pl.program_id(0)
    nh = DIFF_HEADS + DSA_HEADS
    key = lax.broadcasted_iota(I32, (TK, TQ), 0)
    qry = lax.broadcasted_iota(I32, (TK, TQ), 1)
    for tile in range(3):
        bucket = bucket_ref[tile]
        acc = jnp.zeros(bucket.shape, F32)
        for b in range(REL_BUCKETS):
            acc = jnp.where(bucket == b, tab_ref[b * nh + hd], acc)
        if tile == 2:
            acc = jnp.where(key <= qry, acc, NEG)
        o_ref[tile] = acc * LOG2E


def _bias_tables(rel_bias):
    nh = DIFF_HEADS + DSA_HEADS
    assert TK + 1 >= REL_MAX_DIST
    j = jnp.arange(TK, dtype=I32)[:, None]
    i = jnp.arange(TQ, dtype=I32)[None, :]
    dist = jnp.stack([jnp.full((TK, TQ), 2 * TK, I32), i - j + TK, i - j])
    return pl.pallas_call(
        _bias_tiles_body,
        grid=(nh,),
        in_specs=[pl.BlockSpec(memory_space=pltpu.SMEM), pl.BlockSpec((3, TK, TQ), lambda h: (0, 0, 0))],
        out_specs=pl.BlockSpec((None, 3, TK, TQ), lambda h: (h, 0, 0, 0)),
        out_shape=jax.ShapeDtypeStruct((nh, 3, TK, TQ), F32),
        compiler_params=_cparams(1),
        name="bias_tiles",
    )(rel_bias.astype(F32).reshape(-1), _rel_bucket(dist))


def _block_diag_ones(width, group):
    r = np.arange(width)
    return jnp.asarray((r[:, None] // group) == (r[None, :] // group), dtype=BF16)


def _pad_lanes(w, width=LANES):
    return jnp.pad(w, ((0, 0), (0, width - w.shape[1])))


def kernel(x, rel_bias, mix_norm_g, ev_w_in, ev_w_out, gmlp_w_s, gmlp_b_s, gdn_conv_w, gdn_a_log, gdn_dt_bias,
           gdn_norm_g, od_w_in, od_w_out, diff_q_norm_g, diff_k_norm_g, diff_lambda, diff_sub_norm_g,
           dsa_q_norm_g, dsa_k_norm_g, ffn_norm_g, ffn_w_up, ffn_conv_w, ffn_conv_b, ffn_w_down):
    batch, seq, d = x.shape
    n = batch * seq
    depth = mix_norm_g.shape[0]
    assert d == D_MODEL and seq % max(TM_PROJ, TL_PREP, TL_GDN, TQ) == 0
    h = x.reshape(n, d)
    bias_tiles = _bias_tables(rel_bias)
    bd64 = _block_diag_ones(512, DIFF_QK)
    bd128 = _block_diag_ones(512, DSA_DIM)
    gw = GMLP_GROUPS * LANES
    qkv_w = 3 * GDN_HEADS * GDN_DIM

    for layer in range(depth):
        j = layer // 2
        g_mix = mix_norm_g[layer].reshape(1, d)
        if layer % 2 == 0:
            w = ev_w_in[j]
            o_u, o_v, o_qkv = 0, gw, 2 * gw
            o_b = o_qkv + qkv_w
            o_a = o_b + GDN_HEADS
            o_z = o_a + GDN_HEADS
            w_main = jnp.concatenate(
                [w[:, o_u:o_v], w[:, o_v:o_qkv], w[:, o_z:o_z + GDN_HEADS * GDN_DIM], w[:, o_qkv:o_b]],
                axis=1).astype(BF16)
            w_gate = _pad_lanes(w[:, o_b:o_z]).astype(BF16)
            proj, gates = _in_proj(h, g_mix, w_main, w_gate)
            gp = jnp.stack([
                _pad_lanes(jnp.concatenate([jnp.zeros((GDN_HEADS,), F32), gdn_a_log[j]])[None])[0],
                _pad_lanes(jnp.concatenate([jnp.zeros((GDN_HEADS,), F32), gdn_dt_bias[j]])[None])[0]])
            y_a, qkvn, gates2 = _even_prep(proj, gates, gmlp_w_s[j], gmlp_b_s[j].T, gdn_conv_w[j], gp, batch, seq)
            y_b = _gdn(qkvn, gates2, proj, gdn_norm_g[j].reshape(1, GDN_DIM), batch, seq)
            h = _out_proj(h, y_a, y_b, ev_w_out[j].astype(BF16))
        else:
            lambda_init = 0.8 - 0.6 * math.exp(-0.3 * layer)
            w = od_w_in[j]
            c = np.cumsum([0, 512, 512, 512, 512, 128, 128, 512, 64, 8])
            dq, dk, dv, sq, sk, sv, iq, ik, iw = [w[:, c[i]:c[i + 1]] for i in range(9)]
            w_main = jnp.concatenate([dq, dk, dv, sq, iq, sk, sv, ik, ik], axis=1).astype(BF16)
            w_gate = _pad_lanes(iw).astype(BF16)
            proj, iw_out = _in_proj(h, g_mix, w_main, w_gate)
            dqn, dkn, sqn, skn, vt = _odd_prep(
                proj,
                jnp.tile(diff_q_norm_g[j], 2 * DIFF_HEADS)[None], jnp.tile(diff_k_norm_g[j], 2 * DIFF_HEADS)[None],
                jnp.tile(dsa_q_norm_g[j], DSA_HEADS)[None], dsa_k_norm_g[j][None], bd64, bd128, batch, seq)
            y_c = _diff_attn(dqn, dkn, vt, bias_tiles, diff_lambda[j], diff_sub_norm_g[j][None],
                             batch, seq, lambda_init)
            y_d = _dsa(sqn, skn, vt, proj, iw_out, bias_tiles, batch, seq)
            h = _out_proj(h, y_c, y_d, od_w_out[j].astype(BF16))
        h = _ffn(h, ffn_norm_g[layer].reshape(1, d), ffn_w_up[layer].astype(BF16), ffn_conv_w[layer],
                 ffn_conv_b[layer].reshape(1, 2 * D_FF), ffn_w_down[layer].astype(BF16), seq)
    return h.reshape(batch, seq, d)
```

```python
import functools
import math

import numpy as np
import jax
import jax.numpy as jnp
from jax import lax
from jax.experimental import pallas as pl
from jax.experimental.pallas import tpu as pltpu

F32 = jnp.float32
BF16 = jnp.bfloat16
I32 = jnp.int32
I16 = jnp.int16
HIGHEST = lax.Precision.HIGHEST

D_MODEL = 1024
GMLP_GROUPS = 4
GMLP_CHUNK = 128
GDN_HEADS = 4
GDN_DIM = 128
GDN_CHUNK = 64
GDN_CONV = 4
DIFF_HEADS = 4
DIFF_QK = 64
DSA_HEADS = 4
DSA_DIM = 128
IDX_HEADS = 8
IDX_DIM = 64
IDX_TOPK_MAX = 256
REL_BUCKETS = 32
REL_MAX_DIST = 128
D_FF = 2816
FFN_CONV = 3
EPS = 1e-6

LANES = 128
SUBLANES_F32 = 8
SUBLANES_BF16 = 16
VMEM_LIMIT = 56 * 1024 * 1024
VT_ROWS = LANES + SUBLANES_BF16

TM_PROJ = 512
TM_FFN = 512
TL_PREP = 512
TL_GDN = 512
GDN_SEQS_PER_STEP = 2
GDN_PREP_PROBLEMS = 16
TQ = 256
TK = 256
FF_CHUNK = 256
NEG = -1e30
LOG2E = 1.0 / math.log(2.0)
INT_MIN = -2 ** 31


def _cparams(n_axes):
    return pltpu.CompilerParams(dimension_semantics=("arbitrary",) * n_axes, vmem_limit_bytes=VMEM_LIMIT)


def _dot(a, b, **kw):
    return jnp.dot(a, b, preferred_element_type=F32, **kw)


def _dot_nt(a, b, **kw):
    return lax.dot_general(a, b, (((1,), (1,)), ((), ())), preferred_element_type=F32, **kw)


def _dot_tn(a, b, **kw):
    return lax.dot_general(a, b, (((0,), (0,)), ((), ())), preferred_element_type=F32, **kw)


def _rms(x, g):
    return x * lax.rsqrt(jnp.mean(x * x, axis=-1, keepdims=True) + EPS) * g


def _sigmoid(x):
    return 1.0 / (1.0 + jnp.exp(-x))


def _silu(x):
    return x * _sigmoid(x)


def _gelu_tanh(x):
    return 0.5 * x * (1.0 + jnp.tanh(math.sqrt(2.0 / math.pi) * (x + 0.044715 * (x * x * x))))


def _softplus(x):
    return jnp.maximum(x, 0.0) + jnp.log(1.0 + jnp.exp(-jnp.abs(x)))


def _in_proj_body(h_ref, g_ref, w_ref, wg_ref, o_ref, og_ref, *, nout):
    xn = _rms(h_ref[...], g_ref[...]).astype(BF16)
    for c in range(0, nout, 512):
        e = min(c + 512, nout)
        o_ref[:, c:e] = _dot(xn, w_ref[:, c:e]).astype(BF16)
    og_ref[...] = _dot(xn, wg_ref[...])


def _in_proj(h, g, w, wg):
    n, d = h.shape
    nout = w.shape[1]
    return pl.pallas_call(
        functools.partial(_in_proj_body, nout=nout),
        grid=(n // TM_PROJ,),
        in_specs=[
            pl.BlockSpec((TM_PROJ, d), lambda i: (i, 0)),
            pl.BlockSpec((1, d), lambda i: (0, 0)),
            pl.BlockSpec((d, nout), lambda i: (0, 0)),
            pl.BlockSpec((d, LANES), lambda i: (0, 0)),
        ],
        out_specs=[
            pl.BlockSpec((TM_PROJ, nout), lambda i: (i, 0)),
            pl.BlockSpec((TM_PROJ, LANES), lambda i: (i, 0)),
        ],
        out_shape=[jax.ShapeDtypeStruct((n, nout), BF16), jax.ShapeDtypeStruct((n, LANES), F32)],
        compiler_params=_cparams(1),
        name="in_proj",
    )(h, g, w, wg)


def _out_proj_body(h_ref, y1_ref, y2_ref, w_ref, o_ref):
    half = y1_ref.shape[1]
    o_ref[...] = h_ref[...] + _dot(y1_ref[...], w_ref[:half, :]) + _dot(y2_ref[...], w_ref[half:, :])


def _out_proj(h, y1, y2, w):
    n, d = h.shape
    half = y1.shape[1]
    return pl.pallas_call(
        _out_proj_body,
        grid=(n // TM_PROJ,),
        in_specs=[
            pl.BlockSpec((TM_PROJ, d), lambda i: (i, 0)),
            pl.BlockSpec((TM_PROJ, half), lambda i: (i, 0)),
            pl.BlockSpec((TM_PROJ, half), lambda i: (i, 0)),
            pl.BlockSpec((2 * half, d), lambda i: (0, 0)),
        ],
        out_specs=pl.BlockSpec((TM_PROJ, d), lambda i: (i, 0)),
        out_shape=jax.ShapeDtypeStruct((n, d), F32),
        compiler_params=_cparams(1),
        name="out_proj",
    )(h, y1, y2, w)


def _ffn_body(h_ref, halo_ref, g_ref, wup_ref, cw_ref, cb_ref, wdn_ref, o_ref, xn_ref, act_ref, *, tiles_per_seq):
    tm = h_ref.shape[0]
    i = pl.program_id(0)
    h = h_ref[...]
    g = g_ref[...]
    hb = SUBLANES_BF16
    xn_ref[hb:, :] = _rms(h, g).astype(BF16)
    keep = jnp.where(i % tiles_per_seq == 0, 0.0, 1.0)
    xn_ref[:hb, :] = (_rms(halo_ref[...], g) * keep).astype(BF16)
    xn = xn_ref[...]

    def up_pair(c):
        return [_dot(xn, wup_ref[:, base:base + FF_CHUNK]) for base in (c, D_FF + c)]

    def conv(up, base):
        cw = cw_ref[:, base:base + FF_CHUNK]
        return (cw[0:1] * up[hb - 2:tm + hb - 2] + cw[1:2] * up[hb - 1:tm + hb - 1] + cw[2:3] * up[hb:tm + hb]
                + cb_ref[:, base:base + FF_CHUNK])

    chunks = list(range(0, D_FF, FF_CHUNK))
    ups = up_pair(chunks[0])
    for n, c in enumerate(chunks):
        nxt = up_pair(chunks[n + 1]) if n + 1 < len(chunks) else None
        act_ref[:, c:c + FF_CHUNK] = (_silu(conv(ups[0], c)) * conv(ups[1], D_FF + c)).astype(BF16)
        ups = nxt
    o_ref[...] = h + _dot(act_ref[...], wdn_ref[...])


def _ffn(h, g, wup, cw, cb, wdn, seq):
    n, d = h.shape
    tiles_per_seq = seq // TM_FFN
    halo_blocks = TM_FFN // SUBLANES_BF16
    const = dict(pipeline_mode=pl.Buffered(1))
    return pl.pallas_call(
        functools.partial(_ffn_body, tiles_per_seq=tiles_per_seq),
        grid=(n // TM_FFN,),
        in_specs=[
            pl.BlockSpec((TM_FFN, d), lambda i: (i, 0)),
            pl.BlockSpec((SUBLANES_BF16, d), lambda i: (jnp.maximum(i * halo_blocks - 1, 0), 0)),
            pl.BlockSpec((1, d), lambda i: (0, 0)),
            pl.BlockSpec((d, 2 * D_FF), lambda i: (0, 0), **const),
            pl.BlockSpec((FFN_CONV, 2 * D_FF), lambda i: (0, 0)),
            pl.BlockSpec((1, 2 * D_FF), lambda i: (0, 0)),
            pl.BlockSpec((D_FF, d), lambda i: (0, 0), **const),
        ],
        out_specs=pl.BlockSpec((TM_FFN, d), lambda i: (i, 0)),
        out_shape=jax.ShapeDtypeStruct((n, d), F32),
        scratch_shapes=[pltpu.VMEM((TM_FFN + SUBLANES_BF16, d), BF16), pltpu.VMEM((TM_FFN, D_FF), BF16)],
        compiler_params=_cparams(1),
        name="ffn",
    )(h, h, g, wup, cw, cb, wdn)


def _even_prep_body(uv_ref, qkv_ref, halo_ref, gate_ref, ws_ref, bs_ref, cw_ref, gp_ref,
                    ya_ref, qkvo_ref, gout_ref):
    tl = uv_ref.shape[0]
    t = pl.program_id(1)
    gw = LANES
    row = lax.broadcasted_iota(I32, (GMLP_CHUNK, GMLP_CHUNK), 0)
    col = lax.broadcasted_iota(I32, (GMLP_CHUNK, GMLP_CHUNK), 1)
    for gi in range(GMLP_GROUPS):
        u = _gelu_tanh(uv_ref[:, gi * gw:(gi + 1) * gw].astype(F32))
        v = _gelu_tanh(uv_ref[:, (GMLP_GROUPS + gi) * gw:(GMLP_GROUPS + gi + 1) * gw].astype(F32))
        vc = v - jnp.mean(v, axis=-1, keepdims=True)
        vn = (vc * lax.rsqrt(jnp.mean(vc * vc, axis=-1, keepdims=True) + EPS)).astype(BF16)
        w = jnp.where(col <= row, ws_ref[gi], 0.0).astype(BF16)
        b = bs_ref[:, gi:gi + 1]
        for c in range(tl // GMLP_CHUNK):
            r0 = c * GMLP_CHUNK
            mixed = _dot(w, vn[r0:r0 + GMLP_CHUNK]) + b
            ya_ref[r0:r0 + GMLP_CHUNK, gi * gw:(gi + 1) * gw] = (u[r0:r0 + GMLP_CHUNK] * mixed).astype(BF16)
    keep = jnp.where(t == 0, 0.0, 1.0)
    hb = SUBLANES_BF16
    for j in range(3 * GDN_HEADS):
        sl = slice(j * gw, (j + 1) * gw)
        x = jnp.concatenate([halo_ref[:, sl].astype(F32) * keep, qkv_ref[:, sl].astype(F32)], axis=0)
        cw = cw_ref[:, sl]
        y = cw[0:1] * x[hb - 3:tl + hb - 3]
        for k in range(1, GDN_CONV):
            y = y + cw[k:k + 1] * x[hb - 3 + k:tl + hb - 3 + k]
        y = _silu(y)
        if j < 2 * GDN_HEADS:
            y = y * lax.rsqrt(jnp.sum(y * y, axis=-1, keepdims=True) + EPS)
        if j < GDN_HEADS:
            y = y * (GDN_DIM ** -0.5)
        qkvo_ref[:, sl] = y.astype(BF16)
    x = gate_ref[...]
    lane = lax.broadcasted_iota(I32, x.shape, 1)
    beta = _sigmoid(x)
    gdec = -jnp.exp(gp_ref[0:1, :]) * _softplus(x + gp_ref[1:2, :])
    gout_ref[...] = jnp.where(lane < GDN_HEADS, beta, gdec)


def _even_prep(proj, gates, ws, bs_t, cw, gp, batch, seq):
    n = proj.shape[0]
    nt = seq // TL_PREP
    halo_blocks = TL_PREP // SUBLANES_BF16
    qkv_w = 3 * GDN_HEADS * GDN_DIM
    row = lambda b, t: b * nt + t
    return pl.pallas_call(
        _even_prep_body,
        grid=(batch, nt),
        in_specs=[
            pl.BlockSpec((TL_PREP, 2 * GMLP_GROUPS * LANES), lambda b, t: (row(b, t), 0)),
            pl.BlockSpec((TL_PREP, qkv_w), lambda b, t: (row(b, t), 1)),
            pl.BlockSpec((SUBLANES_BF16, qkv_w), lambda b, t: (jnp.maximum(row(b, t) * halo_blocks - 1, 0), 1)),
            pl.BlockSpec((TL_PREP, LANES), lambda b, t: (row(b, t), 0)),
            pl.BlockSpec((GMLP_GROUPS, GMLP_CHUNK, GMLP_CHUNK), lambda b, t: (0, 0, 0)),
            pl.BlockSpec((GMLP_CHUNK, GMLP_GROUPS), lambda b, t: (0, 0)),
            pl.BlockSpec((GDN_CONV, qkv_w), lambda b, t: (0, 0)),
            pl.BlockSpec((2, LANES), lambda b, t: (0, 0)),
        ],
        out_specs=[
            pl.BlockSpec((TL_PREP, GMLP_GROUPS * LANES), lambda b, t: (row(b, t), 0)),
            pl.BlockSpec((TL_PREP, qkv_w), lambda b, t: (row(b, t), 0)),
            pl.BlockSpec((TL_PREP, LANES), lambda b, t: (row(b, t), 0)),
        ],
        out_shape=[
            jax.ShapeDtypeStruct((n, GMLP_GROUPS * LANES), BF16),
            jax.ShapeDtypeStruct((n, qkv_w), BF16),
            jax.ShapeDtypeStruct((n, LANES), F32),
        ],
        compiler_params=_cparams(2),
        name="even_prep",
    )(proj, proj, proj, gates, ws, bs_t, cw, gp)


def _split_bf16(x):
    hi = x.astype(BF16)
    return hi, (x - hi.astype(F32)).astype(BF16)


def _mm3(a, b):
    ah, al = _split_bf16(a)
    bh, bl = _split_bf16(b)
    return _dot(ah, bh) + (_dot(ah, bl) + _dot(al, bh))


def _unit_lower_inverses(lows):
    c = lows[0].shape[0]
    row = lax.broadcasted_iota(I32, (c, c), 0)
    col = lax.broadcasted_iota(I32, (c, c), 1)
    eye = jnp.where(row == col, 1.0, 0.0)
    same16 = jnp.right_shift(row, 4) == jnp.right_shift(col, 4)
    same32 = jnp.right_shift(row, 5) == jnp.right_shift(col, 5)
    ps = [jnp.where(same16, -low, 0.0) for low in lows]
    xs = [eye + p for p in ps]
    for _ in range(3):
        ps = [_mm3(p, p) for p in ps]
        xs = [x + _mm3(x, p) for x, p in zip(xs, ps)]
    mid = jnp.logical_and(same32, jnp.logical_not(same16))
    mm1 = lambda a, b: _dot(a.astype(BF16), b.astype(BF16))
    ys = [mm1(jnp.where(mid, low, 0.0), x) for low, x in zip(lows, xs)]
    xs = [x - mm1(x, y) for x, y in zip(xs, ys)]
    ys = [mm1(jnp.where(same32, 0.0, low), x) for low, x in zip(lows, xs)]
    return [x - mm1(x, y) for x, y in zip(xs, ys)]


def _gdn_body(qkv_ref, gate_ref, z_ref, gn_ref, o_ref,
              s_ref, u_ref, w_ref, qg_ref, kdt_ref, intra_ref, egl_ref):
    nseq, tl = qkv_ref.shape[0], qkv_ref.shape[1]
    c = GDN_CHUNK
    d = GDN_DIM
    lanes = [(bi, hd) for bi in range(nseq) for hd in range(GDN_HEADS)]
    flat = lambda bi, hd: bi * GDN_HEADS + hd
    prep_chunks = max(1, GDN_PREP_PROBLEMS // len(lanes))
    t = pl.program_id(1)

    @pl.when(t == 0)
    def _():
        s_ref[...] = jnp.zeros_like(s_ref)

    row = lax.broadcasted_iota(I32, (c, c), 0)
    col = lax.broadcasted_iota(I32, (c, c), 1)
    incl = col <= row
    strict = col < row
    lmat = jnp.where(incl, 1.0, 0.0)
    gn = gn_ref[...]

    def prepare(it, carry):
        cis = [it * prep_chunks + j for j in range(prep_chunks)]
        rows = [pl.ds(pl.multiple_of(ci * c, c), c) for ci in cis]
        gates = {(j, bi): gate_ref[bi, rows[j], :] for j in range(prep_chunks) for bi in range(nseq)}
        gcum = {key: _dot(lmat, g, precision=HIGHEST) for key, g in gates.items()}
        pairs = [(j, bi, hd) for j in range(prep_chunks) for bi, hd in lanes]
        qs = [qkv_ref[bi, rows[j], hd * d:(hd + 1) * d] for j, bi, hd in pairs]
        ks = [qkv_ref[bi, rows[j], (GDN_HEADS + hd) * d:(GDN_HEADS + hd + 1) * d] for j, bi, hd in pairs]
        vs = [qkv_ref[bi, rows[j], (2 * GDN_HEADS + hd) * d:(2 * GDN_HEADS + hd + 1) * d] for j, bi, hd in pairs]
        beta = [jnp.broadcast_to(gates[j, bi][:, hd:hd + 1], (c, d)) for j, bi, hd in pairs]
        gc = [jnp.broadcast_to(gcum[j, bi][:, GDN_HEADS + hd:GDN_HEADS + hd + 1], (c, d)) for j, bi, hd in pairs]
        decay = [jnp.where(incl, jnp.exp(jnp.where(incl, g[:, :c] - g.T[:c, :], 0.0)), 0.0) for g in gc]
        kb = [k.astype(F32) * b for k, b in zip(ks, beta)]
        lows = [jnp.where(strict, _dot_nt(x.astype(BF16), k) * dc, 0.0) for x, k, dc in zip(kb, ks, decay)]
        tinv = [x.astype(BF16) for x in _unit_lower_inverses(lows)]
        egc = [jnp.exp(g) for g in gc]
        rhs = [jnp.concatenate([(v.astype(F32) * b).astype(BF16), (x * e).astype(BF16)], axis=1)
               for v, b, x, e in zip(vs, beta, kb, egc)]
        uw = [_dot(ti, r) for ti, r in zip(tinv, rhs)]
        qk = [_dot_nt(q, k) for q, k in zip(qs, ks)]
        for n, (j, bi, hd) in enumerate(pairs):
            ci, f = cis[j], flat(bi, hd)
            u_ref[ci, f] = uw[n][:, :d]
            w_ref[ci, f] = uw[n][:, d:].astype(BF16)
            intra_ref[ci, f] = jnp.where(incl, qk[n] * decay[n], 0.0).astype(BF16)
            qg_ref[ci, f] = (qs[n].astype(F32) * egc[n]).astype(BF16)
            g_last = gc[n][c - 1:c, :]
            kdt_ref[ci, f] = (ks[n].astype(F32) * jnp.exp(g_last - gc[n])).T.astype(BF16)
            egl_ref[ci, f] = jnp.exp(g_last)
        return carry

    lax.fori_loop(0, tl // c // prep_chunks, prepare, 0)

    def scan(ci, carry):
        r0 = pl.multiple_of(ci * c, c)
        fs = [flat(bi, hd) for bi, hd in lanes]
        states = [s_ref[f] for f in fs]
        sb = [s.astype(BF16) for s in states]
        ws = [_dot(w_ref[ci, f], s) for f, s in zip(fs, sb)]
        qsd = [_dot(qg_ref[ci, f], s) for f, s in zip(fs, sb)]
        vnb = [(u_ref[ci, f] - w).astype(BF16) for f, w in zip(fs, ws)]
        for f, s, v in zip(fs, states, vnb):
            s_ref[f] = s * egl_ref[ci, f] + _dot(kdt_ref[ci, f], v)
        for (bi, hd), f, qd, v in zip(lanes, fs, qsd, vnb):
            out = qd + _dot(intra_ref[ci, f], v)
            z = z_ref[bi, pl.ds(r0, c), hd * d:(hd + 1) * d].astype(F32)
            o_ref[bi, pl.ds(r0, c), hd * d:(hd + 1) * d] = (_rms(out, gn) * _silu(z)).astype(BF16)
        return carry

    lax.fori_loop(0, tl // c, scan, 0)


def _gdn(qkv, gates, proj, gn, batch, seq):
    n = qkv.shape[0]
    nt = seq // TL_GDN
    nc = TL_GDN // GDN_CHUNK
    width = GDN_HEADS * GDN_DIM
    nseq = GDN_SEQS_PER_STEP if batch % GDN_SEQS_PER_STEP == 0 else 1
    nl = nseq * GDN_HEADS
    as3d = lambda a: a.reshape(batch, seq, a.shape[-1])
    out = pl.pallas_call(
        _gdn_body,
        grid=(batch // nseq, nt),
        in_specs=[
            pl.BlockSpec((nseq, TL_GDN, 3 * width), lambda b, t: (b, t, 0)),
            pl.BlockSpec((nseq, TL_GDN, LANES), lambda b, t: (b, t, 0)),
            pl.BlockSpec((nseq, TL_GDN, width), lambda b, t: (b, t, 2)),
            pl.BlockSpec((1, GDN_DIM), lambda b, t: (0, 0)),
        ],
        out_specs=pl.BlockSpec((nseq, TL_GDN, width), lambda b, t: (b, t, 0)),
        out_shape=jax.ShapeDtypeStruct((batch, seq, width), BF16),
        scratch_shapes=[
            pltpu.VMEM((nl, GDN_DIM, GDN_DIM), F32),
            pltpu.VMEM((nc, nl, GDN_CHUNK, GDN_DIM), F32),
            pltpu.VMEM((nc, nl, GDN_CHUNK, GDN_DIM), BF16),
            pltpu.VMEM((nc, nl, GDN_CHUNK, GDN_DIM), BF16),
            pltpu.VMEM((nc, nl, GDN_DIM, GDN_CHUNK), BF16),
            pltpu.VMEM((nc, nl, GDN_CHUNK, GDN_CHUNK), BF16),
            pltpu.VMEM((nc, nl, 1, GDN_DIM), F32),
        ],
        compiler_params=_cparams(2),
        name="gdn",
    )(as3d(qkv), as3d(gates), as3d(proj), gn)
    return out.reshape(n, width)


def _group_mean_sq(x, ones_bd, group):
    x2 = x * x
    hi = x2.astype(BF16)
    lo = (x2 - hi.astype(F32)).astype(BF16)
    return (_dot(hi, ones_bd) + _dot(lo, ones_bd)) * (1.0 / group)


def _odd_prep_body(dq_ref, dk_ref, dv_ref, sq_ref, sk_ref, sv_ref, gq_ref, gk_ref, gsq_ref, gsk_ref, bd64_ref,
                   bd128_ref, dqo_ref, dko_ref, sqo_ref, sko_ref, vt_ref):
    ones = jnp.ones((VT_ROWS - LANES, TK), BF16)
    for j in range(DIFF_HEADS + 1):
        src = sv_ref[...] if j == DIFF_HEADS else dv_ref[:, j * LANES:(j + 1) * LANES]
        xt = src.astype(F32).T
        for c in range(xt.shape[1] // TK):
            vt_ref[j, c, :LANES, :] = xt[:, c * TK:(c + 1) * TK].astype(BF16)
            vt_ref[j, c, LANES:, :] = ones
    bd64 = bd64_ref[...]
    x = dq_ref[...].astype(F32)
    dqo_ref[...] = (x * lax.rsqrt(_group_mean_sq(x, bd64, DIFF_QK) + EPS) * gq_ref[...]
                    * (DIFF_QK ** -0.5 * LOG2E)).astype(BF16)
    x = dk_ref[...].astype(F32)
    dko_ref[...] = (x * lax.rsqrt(_group_mean_sq(x, bd64, DIFF_QK) + EPS) * gk_ref[...]).astype(BF16)
    x = sq_ref[...].astype(F32)
    sqo_ref[...] = (x * lax.rsqrt(_group_mean_sq(x, bd128_ref[...], DSA_DIM) + EPS) * gsq_ref[...]
                    * (DSA_DIM ** -0.5 * LOG2E)).astype(BF16)
    x = sk_ref[...].astype(F32)
    sko_ref[...] = _rms(x, gsk_ref[...]).astype(BF16)


def _odd_prep(proj, gq, gk, gsq, gsk, bd64, bd128, batch, seq):
    n = proj.shape[0]
    w = 512
    nt = seq // TL_PREP
    kt = TL_PREP // TK
    full = lambda shape: pl.BlockSpec(shape, lambda b, t: (0,) * len(shape))
    row = lambda b, t: b * nt + t
    return pl.pallas_call(
        _odd_prep_body,
        grid=(batch, nt),
        in_specs=[
            pl.BlockSpec((TL_PREP, w), lambda b, t: (row(b, t), 0)),
            pl.BlockSpec((TL_PREP, w), lambda b, t: (row(b, t), 1)),
            pl.BlockSpec((TL_PREP, w), lambda b, t: (row(b, t), 2)),
            pl.BlockSpec((TL_PREP, w), lambda b, t: (row(b, t), 3)),
            pl.BlockSpec((TL_PREP, LANES), lambda b, t: (row(b, t), 5 * w // LANES)),
            pl.BlockSpec((TL_PREP, LANES), lambda b, t: (row(b, t), 5 * w // LANES + 1)),
            full((1, w)), full((1, w)), full((1, w)), full((1, LANES)), full((w, w)), full((w, w)),
        ],
        out_specs=[
            pl.BlockSpec((TL_PREP, w), lambda b, t: (row(b, t), 0)),
            pl.BlockSpec((TL_PREP, w), lambda b, t: (row(b, t), 0)),
            pl.BlockSpec((TL_PREP, w), lambda b, t: (row(b, t), 0)),
            pl.BlockSpec((TL_PREP, LANES), lambda b, t: (row(b, t), 0)),
            pl.BlockSpec((None, DIFF_HEADS + 1, kt, VT_ROWS, TK), lambda b, t: (b, 0, t, 0, 0)),
        ],
        out_shape=[
            jax.ShapeDtypeStruct((n, w), BF16), jax.ShapeDtypeStruct((n, w), BF16),
            jax.ShapeDtypeStruct((n, w), BF16), jax.ShapeDtypeStruct((n, LANES), BF16),
            jax.ShapeDtypeStruct((batch, DIFF_HEADS + 1, seq // TK, VT_ROWS, TK), BF16),
        ],
        compiler_params=_cparams(2),
        name="odd_prep",
    )(proj, proj, proj, proj, proj, proj, gq, gk, gsq, gsk, bd64, bd128)


def _bias_tile_index(kb, qb):
    return jnp.clip(kb - (qb - 2), 0, 2)


def _softmax_step_t(s_t, vt, m_ref, acc_ref):
    for g in range(s_t.shape[1] // LANES):
        sl = slice(g * LANES, (g + 1) * LANES)
        s = s_t[:, sl]
        m_prev = m_ref[:, sl]
        m_new = jnp.maximum(m_prev, jnp.max(s, axis=0, keepdims=True))
        alpha = jnp.exp2(m_prev - m_new)
        p = jnp.exp2(s - m_new)
        acc_ref[:, sl] = alpha * acc_ref[:, sl] + _dot(vt, p.astype(BF16))
        m_ref[:, sl] = m_new


def _attend_tiles(logits, values, nblk, s_ref, m_ref, acc_ref):
    streams = range(len(logits))
    m_ref[...] = jnp.full_like(m_ref, NEG)
    acc_ref[...] = jnp.zeros_like(acc_ref)
    for i in streams:
        s_ref[i, 0] = logits[i](0)

    def pair(it, carry):
        kb = 2 * it
        for i in streams:
            s_ref[i, 1] = logits[i](kb + 1)
        for i in streams:
            _softmax_step_t(s_ref[i, 0], values[i](kb), m_ref.at[i], acc_ref.at[i])
        for i in streams:
            s_ref[i, 0] = logits[i](jnp.minimum(kb + 2, nblk - 1))
        for i in streams:
            _softmax_step_t(s_ref[i, 1], values[i](kb + 1), m_ref.at[i], acc_ref.at[i])
        return carry

    lax.fori_loop(0, nblk // 2, pair, 0)

    @pl.when(nblk % 2 == 1)
    def _():
        for i in streams:
            _softmax_step_t(s_ref[i, 0], values[i](nblk - 1), m_ref.at[i], acc_ref.at[i])


def _diff_attn_body(q_ref, k_ref, vt_ref, bias_ref, lam_ref, sg_ref, o_ref,
                    qt_ref, m_ref, acc_ref, s_ref, *, lambda_init):
    qi = pl.program_id(1)
    nblk = qi + 1
    heads = range(DIFF_HEADS)
    for hd in heads:
        qt = q_ref[:, hd * LANES:(hd + 1) * LANES].astype(F32).T
        dim = lax.broadcasted_iota(I32, qt.shape, 0)
        qt_ref[hd, :, :TQ] = jnp.where(dim < DIFF_QK, qt, 0.0).astype(BF16)
        qt_ref[hd, :, TQ:] = jnp.where(dim >= DIFF_QK, qt, 0.0).astype(BF16)

    def logits_of(hd):
        def logits(kb):
            r0 = pl.multiple_of(kb * TK, TK)
            b = bias_ref[hd, _bias_tile_index(kb, qi)]
            s = _dot(k_ref[pl.ds(r0, TK), hd * LANES:(hd + 1) * LANES], qt_ref[hd])
            return s + jnp.concatenate([b, b], axis=1)
        return logits

    _attend_tiles([logits_of(hd) for hd in heads], [lambda kb, hd=hd: vt_ref[hd, kb] for hd in heads],
                  nblk, s_ref, m_ref, acc_ref)

    lf = lam_ref[...]
    lam = (jnp.exp(jnp.sum(lf[0:1] * lf[1:2], axis=-1, keepdims=True))
           - jnp.exp(jnp.sum(lf[2:3] * lf[3:4], axis=-1, keepdims=True)) + lambda_init)
    for hd in heads:
        den = acc_ref[hd, LANES:LANES + 1, :]
        ot = (acc_ref[hd, :LANES, :TQ] / den[:, :TQ] - lam * (acc_ref[hd, :LANES, TQ:] / den[:, TQ:]))
        ot = ot * lax.rsqrt(jnp.mean(ot * ot, axis=0, keepdims=True) + EPS)
        o_ref[:, hd * LANES:(hd + 1) * LANES] = (ot.T * sg_ref[...] * (1.0 - lambda_init)).astype(BF16)


def _diff_attn(dqn, dkn, vt, bias, lam_p, sub_g, batch, seq, lambda_init):
    n = dqn.shape[0]
    nq = seq // TQ
    width = DIFF_HEADS * LANES
    return pl.pallas_call(
        functools.partial(_diff_attn_body, lambda_init=lambda_init),
        grid=(batch, nq),
        in_specs=[
            pl.BlockSpec((TQ, width), lambda b, q: (b * nq + q, 0)),
            pl.BlockSpec((seq, width), lambda b, q: (b, 0)),
            pl.BlockSpec((None, DIFF_HEADS + 1, seq // TK, VT_ROWS, TK), lambda b, q: (b, 0, 0, 0, 0)),
            pl.BlockSpec((DIFF_HEADS, 3, TK, TQ), lambda b, q: (0, 0, 0, 0)),
            pl.BlockSpec((4, DIFF_QK), lambda b, q: (0, 0)),
            pl.BlockSpec((1, LANES), lambda b, q: (0, 0)),
        ],
        out_specs=pl.BlockSpec((TQ, width), lambda b, q: (b * nq + q, 0)),
        out_shape=jax.ShapeDtypeStruct((n, width), BF16),
        scratch_shapes=[
            pltpu.VMEM((DIFF_HEADS, LANES, 2 * TQ), BF16),
            pltpu.VMEM((DIFF_HEADS, 1, 2 * TQ), F32),
            pltpu.VMEM((DIFF_HEADS, VT_ROWS, 2 * TQ), F32),
            pltpu.VMEM((DIFF_HEADS, 2, TK, 2 * TQ), F32),
        ],
        compiler_params=_cparams(2),
        name="diff_attn",
    )(dqn, dkn, vt, bias, lam_p, sub_g)


def _dsa_body(q_ref, k_ref, vt_ref, iq_ref, ik_ref, iw_ref, bias_ref, o_ref,
              qt_ref, qit_ref, wt_ref, keys_ref, hi_ref, lo_ref, t_ref, m_ref, acc_ref, s_ref, *, top_k, pos_bits):
    qb = pl.program_id(1)
    nkb = qb + 1

    for p in range(IDX_HEADS // 2):
        gt = iq_ref[:, p * LANES:(p + 1) * LANES].astype(F32).T.astype(BF16)
        qit_ref[:, (2 * p) * TQ:(2 * p + 1) * TQ] = gt[:IDX_DIM]
        qit_ref[:, (2 * p + 1) * TQ:(2 * p + 2) * TQ] = gt[IDX_DIM:]
    for hd in range(DSA_HEADS):
        qt_ref[:, hd * TQ:(hd + 1) * TQ] = q_ref[:, hd * LANES:(hd + 1) * LANES].astype(F32).T.astype(BF16)
    wscale = (IDX_HEADS ** -0.5) * (IDX_DIM ** -0.5)
    wt_ref[...] = (iw_ref[...] * wscale).T[:IDX_HEADS, :]

    key_i = lax.broadcasted_iota(I32, (TK, TQ), 0)
    qry_i = lax.broadcasted_iota(I32, (TK, TQ), 1)
    causal = key_i <= qry_i

    def index_blocks(kbs, masked):
        iks = [ik_ref[pl.ds(pl.multiple_of(kb * TK, TK), TK), :IDX_DIM] for kb in kbs]
        idxs = [jnp.zeros((TK, TQ), F32) for _ in kbs]
        for hi in range(IDX_HEADS):
            ss = [_dot(ik, qit_ref[:, hi * TQ:(hi + 1) * TQ]) for ik in iks]
            idxs = [idx + jnp.maximum(s, 0.0) * wt_ref[hi:hi + 1, :] for idx, s in zip(idxs, ss)]
        for kb, idx in zip(kbs, idxs):
            idx = jnp.where(idx == 0.0, 0.0, idx)
            bits = pltpu.bitcast(idx, I32)
            key = jnp.where(bits < 0, bits ^ jnp.int32(0x7FFFFFFF), bits)
            if masked:
                key = jnp.where(causal, key, jnp.int32(INT_MIN))
            keys_ref[kb] = key
            hi_ref[kb] = jnp.right_shift(key, 16).astype(I16)
            lo_ref[kb] = ((key & 0xFFFF) - 0x8000).astype(I16)

    def index_pair(i, carry):
        index_blocks([2 * i, 2 * i + 1], False)
        return carry

    lax.fori_loop(0, (nkb - 1) // 2, index_pair, 0)

    @pl.when((nkb - 1) % 2 == 1)
    def _():
        index_blocks([nkb - 2], False)

    index_blocks([nkb - 1], True)

    def count(pred):
        def body(kb, acc):
            x = jnp.where(pred(keys_ref[kb], key_i + kb * TK), 1.0, 0.0)
            parts = [x[i * SUBLANES_F32:(i + 1) * SUBLANES_F32] for i in range(TK // SUBLANES_F32)]
            while len(parts) > 1:
                parts = [parts[i] + parts[i + 1] for i in range(0, len(parts), 2)]
            return acc + parts[0]
        acc = lax.fori_loop(0, nkb, body, jnp.zeros((SUBLANES_F32, TQ), F32))
        return jnp.sum(acc, axis=0, keepdims=True)

    kf = float(top_k)
    rows16 = SUBLANES_BF16
    i16_min = -2 ** 15

    def count16(ref, cand, strict=False):
        c16 = jnp.broadcast_to(cand, (rows16, TQ)).astype(I16)

        def body(kb, acc):
            x = ref[kb]
            parts = []
            for i in range(TK // rows16):
                slab = x[i * rows16:(i + 1) * rows16]
                hit = slab > c16 if strict else slab >= c16
                parts.append(jnp.where(hit, jnp.int16(1), jnp.int16(0)))
            while len(parts) > 1:
                parts = [parts[i] + parts[i + 1] for i in range(0, len(parts), 2)]
            return acc + parts[0]

        acc = lax.fori_loop(0, nkb, body, jnp.zeros((rows16, TQ), I16))
        return jnp.sum(acc.astype(F32), axis=0, keepdims=True)

    def digit_select(ref, need, fallback):
        c0 = count16(ref, jnp.zeros((1, TQ), I32))
        ok0 = c0 >= need
        d0 = jnp.where(ok0, jnp.int32(0), jnp.int32(i16_min))
        n0 = jnp.where(ok0, c0, fallback)

        def bit_step(i, carry):
            d, n = carry
            cand = d | jnp.left_shift(jnp.int32(1), 14 - i)
            cnt = count16(ref, cand)
            ok = cnt >= need
            return jnp.where(ok, cand, d), jnp.where(ok, cnt, n)

        return lax.fori_loop(0, 15, bit_step, (d0, n0))

    total = (nkb * TK).astype(F32) * jnp.ones((1, TQ), F32)
    t_hi, n_hi = digit_select(hi_ref, kf, total)
    n_above = count16(hi_ref, t_hi, strict=True)
    t_hi16 = jnp.broadcast_to(t_hi, (rows16, TQ)).astype(I16)

    def restrict(kb, carry):
        for i in range(TK // rows16):
            sl = slice(i * rows16, (i + 1) * rows16)
            lo_ref[kb, sl, :] = jnp.where(hi_ref[kb, sl, :] == t_hi16, lo_ref[kb, sl, :], jnp.int16(i16_min))
        return carry

    lax.fori_loop(0, nkb, restrict, 0)
    t_lo, n_lo = digit_select(lo_ref, kf - n_above, n_hi - n_above)
    t = jnp.left_shift(t_hi, 16) + (t_lo + 0x8000)
    cge = n_above + n_lo
    t_ref[...] = t

    @pl.when(jnp.max(cge) > kf)
    def _():
        tt = t_ref[...]
        r = kf - count(lambda kk, pos: kk > tt)

        def pos_step(i, pcut):
            cand = pcut | jnp.left_shift(jnp.int32(1), pos_bits - 1 - i)
            tied_before = count(lambda kk, pos: jnp.logical_and(kk == tt, pos < cand))
            return jnp.where(tied_before <= r, cand, pcut)

        pcut = lax.fori_loop(0, pos_bits, pos_step, jnp.zeros((1, TQ), I32))

        def demote(kb, carry):
            kk = keys_ref[kb]
            drop = jnp.logical_and(kk == tt, key_i + kb * TK >= pcut)
            keys_ref[kb] = jnp.where(drop, kk - 1, kk)
            return carry

        lax.fori_loop(0, nkb, demote, 0)

    def logits_of(hd):
        def logits(kb):
            r0 = pl.multiple_of(kb * TK, TK)
            sel = keys_ref[kb] >= t_ref[...]
            s = _dot(k_ref[pl.ds(r0, TK), :], qt_ref[:, hd * TQ:(hd + 1) * TQ])
            return jnp.where(sel, s + bias_ref[hd, _bias_tile_index(kb, qb)], NEG)
        return logits

    heads = range(DSA_HEADS)
    _attend_tiles([logits_of(hd) for hd in heads], [lambda kb: vt_ref[kb]] * DSA_HEADS, nkb, s_ref, m_ref, acc_ref)

    for hd in heads:
        o_ref[:, hd * LANES:(hd + 1) * LANES] = (acc_ref[hd, :LANES, :] / acc_ref[hd, LANES:LANES + 1, :]).T.astype(BF16)


def _dsa(sqn, skn, vt, proj, iw, bias, batch, seq):
    n = sqn.shape[0]
    nq = seq // TQ
    top_k = min(IDX_TOPK_MAX, seq // 4)
    pos_bits = int(seq).bit_length()
    iq_block = 4
    ik_block = (5 * 512 + 2 * LANES) // LANES
    return pl.pallas_call(
        functools.partial(_dsa_body, top_k=top_k, pos_bits=pos_bits),
        grid=(batch, nq),
        in_specs=[
            pl.BlockSpec((TQ, DSA_HEADS * LANES), lambda b, q: (b * nq + q, 0)),
            pl.BlockSpec((seq, LANES), lambda b, q: (b, 0)),
            pl.BlockSpec((None, None, seq // TK, VT_ROWS, TK), lambda b, q: (b, DIFF_HEADS, 0, 0, 0)),
            pl.BlockSpec((TQ, IDX_HEADS * IDX_DIM), lambda b, q: (b * nq + q, iq_block)),
            pl.BlockSpec((seq, LANES), lambda b, q: (b, ik_block)),
            pl.BlockSpec((TQ, LANES), lambda b, q: (b * nq + q, 0)),
            pl.BlockSpec((DSA_HEADS, 3, TK, TQ), lambda b, q: (1, 0, 0, 0)),
        ],
        out_specs=pl.BlockSpec((TQ, DSA_HEADS * LANES), lambda b, q: (b * nq + q, 0)),
        out_shape=jax.ShapeDtypeStruct((n, DSA_HEADS * LANES), BF16),
        scratch_shapes=[
            pltpu.VMEM((LANES, DSA_HEADS * TQ), BF16),
            pltpu.VMEM((IDX_DIM, IDX_HEADS * TQ), BF16),
            pltpu.VMEM((IDX_HEADS, TQ), F32),
            pltpu.VMEM((seq // TK, TK, TQ), I32),
            pltpu.VMEM((seq // TK, TK, TQ), I16),
            pltpu.VMEM((seq // TK, TK, TQ), I16),
            pltpu.VMEM((1, TQ), I32),
            pltpu.VMEM((DSA_HEADS, 1, TQ), F32),
            pltpu.VMEM((DSA_HEADS, VT_ROWS, TQ), F32),
            pltpu.VMEM((DSA_HEADS, 2, TK, TQ), F32),
        ],
        compiler_params=_cparams(2),
        name="dsa",
    )(sqn, skn, vt, proj, proj, iw, bias)


def _rel_bucket(dist):
    exact = REL_BUCKETS // 2
    n = jnp.maximum(dist, 0)
    nf = jnp.maximum(n, exact).astype(F32)
    far = exact + (jnp.log(nf / exact) / math.log(REL_MAX_DIST / exact) * (REL_BUCKETS - exact)).astype(I32)
    return jnp.where(n < exact, n, jnp.minimum(far, REL_BUCKETS - 1))


def _bias_tiles_body(tab_ref, bucket_ref, o_ref):
    hd = pl.program_id(0)
    nh = DIFF_HEADS + DSA_HEADS
    key = lax.broadcasted_iota(I32, (TK, TQ), 0)
    qry = lax.broadcasted_iota(I32, (TK, TQ), 1)
    for tile in range(3):
        bucket = bucket_ref[tile]
        acc = jnp.zeros(bucket.shape, F32)
        for b in range(REL_BUCKETS):
            acc = jnp.where(bucket == b, tab_ref[b * nh + hd], acc)
        if tile == 2:
            acc = jnp.where(key <= qry, acc, NEG)
        o_ref[tile] = acc * LOG2E


def _bias_tables(rel_bias):
    nh = DIFF_HEADS + DSA_HEADS
    assert TK + 1 >= REL_MAX_DIST
    j = jnp.arange(TK, dtype=I32)[:, None]
    i = jnp.arange(TQ, dtype=I32)[None, :]
    dist = jnp.stack([jnp.full((TK, TQ), 2 * TK, I32), i - j + TK, i - j])
    return pl.pallas_call(
        _bias_tiles_body,
        grid=(nh,),
        in_specs=[pl.BlockSpec(memory_space=pltpu.SMEM), pl.BlockSpec((3, TK, TQ), lambda h: (0, 0, 0))],
        out_specs=pl.BlockSpec((None, 3, TK, TQ), lambda h: (h, 0, 0, 0)),
        out_shape=jax.ShapeDtypeStruct((nh, 3, TK, TQ), F32),
        compiler_params=_cparams(1),
        name="bias_tiles",
    )(rel_bias.astype(F32).reshape(-1), _rel_bucket(dist))


def _block_diag_ones(width, group):
    r = np.arange(width)
    return jnp.asarray((r[:, None] // group) == (r[None, :] // group), dtype=BF16)


def _pad_lanes(w, width=LANES):
    return jnp.pad(w, ((0, 0), (0, width - w.shape[1])))


def kernel(x, rel_bias, mix_norm_g, ev_w_in, ev_w_out, gmlp_w_s, gmlp_b_s, gdn_conv_w, gdn_a_log, gdn_dt_bias,
           gdn_norm_g, od_w_in, od_w_out, diff_q_norm_g, diff_k_norm_g, diff_lambda, diff_sub_norm_g,
           dsa_q_norm_g, dsa_k_norm_g, ffn_norm_g, ffn_w_up, ffn_conv_w, ffn_conv_b, ffn_w_down):
    batch, seq, d = x.shape
    n = batch * seq
    depth = mix_norm_g.shape[0]
    assert d == D_MODEL and seq % max(TM_PROJ, TL_PREP, TL_GDN, TQ) == 0
    h = x.reshape(n, d)
    bias_tiles = _bias_tables(rel_bias)
    bd64 = _block_diag_ones(512, DIFF_QK)
    bd128 = _block_diag_ones(512, DSA_DIM)
    gw = GMLP_GROUPS * LANES
    qkv_w = 3 * GDN_HEADS * GDN_DIM

    for layer in range(depth):
        j = layer // 2
        g_mix = mix_norm_g[layer].reshape(1, d)
        if layer % 2 == 0:
            w = ev_w_in[j]
            o_u, o_v, o_qkv = 0, gw, 2 * gw
            o_b = o_qkv + qkv_w
            o_a = o_b + GDN_HEADS
            o_z = o_a + GDN_HEADS
            w_main = jnp.concatenate(
                [w[:, o_u:o_v], w[:, o_v:o_qkv], w[:, o_z:o_z + GDN_HEADS * GDN_DIM], w[:, o_qkv:o_b]],
                axis=1).astype(BF16)
            w_gate = _pad_lanes(w[:, o_b:o_z]).astype(BF16)
            proj, gates = _in_proj(h, g_mix, w_main, w_gate)
            gp = jnp.stack([
                _pad_lanes(jnp.concatenate([jnp.zeros((GDN_HEADS,), F32), gdn_a_log[j]])[None])[0],
                _pad_lanes(jnp.concatenate([jnp.zeros((GDN_HEADS,), F32), gdn_dt_bias[j]])[None])[0]])
            y_a, qkvn, gates2 = _even_prep(proj, gates, gmlp_w_s[j], gmlp_b_s[j].T, gdn_conv_w[j], gp, batch, seq)
            y_b = _gdn(qkvn, gates2, proj, gdn_norm_g[j].reshape(1, GDN_DIM), batch, seq)
            h = _out_proj(h, y_a, y_b, ev_w_out[j].astype(BF16))
        else:
            lambda_init = 0.8 - 0.6 * math.exp(-0.3 * layer)
            w = od_w_in[j]
            c = np.cumsum([0, 512, 512, 512, 512, 128, 128, 512, 64, 8])
            dq, dk, dv, sq, sk, sv, iq, ik, iw = [w[:, c[i]:c[i + 1]] for i in range(9)]
            w_main = jnp.concatenate([dq, dk, dv, sq, iq, sk, sv, ik, ik], axis=1).astype(BF16)
            w_gate = _pad_lanes(iw).astype(BF16)
            proj, iw_out = _in_proj(h, g_mix, w_main, w_gate)
            dqn, dkn, sqn, skn, vt = _odd_prep(
                proj,
                jnp.tile(diff_q_norm_g[j], 2 * DIFF_HEADS)[None], jnp.tile(diff_k_norm_g[j], 2 * DIFF_HEADS)[None],
                jnp.tile(dsa_q_norm_g[j], DSA_HEADS)[None], dsa_k_norm_g[j][None], bd64, bd128, batch, seq)
            y_c = _diff_attn(dqn, dkn, vt, bias_tiles, diff_lambda[j], diff_sub_norm_g[j][None],
                             batch, seq, lambda_init)
            y_d = _dsa(sqn, skn, vt, proj, iw_out, bias_tiles, batch, seq)
            h = _out_proj(h, y_c, y_d, od_w_out[j].astype(BF16))
        h = _ffn(h, ffn_norm_g[layer].reshape(1, d), ffn_w_up[layer].astype(BF16), ffn_conv_w[layer],
                 ffn_conv_b[layer].reshape(1, 2 * D_FF), ffn_w_down[layer].astype(BF16), seq)
    return h.reshape(batch, seq, d)
```

```python
import functools
import math

import numpy as np
import jax
import jax.numpy as jnp
from jax import lax
from jax.experimental import pallas as pl
from jax.experimental.pallas import tpu as pltpu

F32 = jnp.float32
BF16 = jnp.bfloat16
I32 = jnp.int32
I16 = jnp.int16
HIGHEST = lax.Precision.HIGHEST

D_MODEL = 1024
GMLP_GROUPS = 4
GMLP_CHUNK = 128
GDN_HEADS = 4
GDN_DIM = 128
GDN_CHUNK = 64
GDN_CONV = 4
DIFF_HEADS = 4
DIFF_QK = 64
DSA_HEADS = 4
DSA_DIM = 128
IDX_HEADS = 8
IDX_DIM = 64
IDX_TOPK_MAX = 256
REL_BUCKETS = 32
REL_MAX_DIST = 128
D_FF = 2816
FFN_CONV = 3
EPS = 1e-6

LANES = 128
SUBLANES_F32 = 8
SUBLANES_BF16 = 16
VMEM_LIMIT = 56 * 1024 * 1024
VT_ROWS = LANES + SUBLANES_BF16

TM_PROJ = 512
TM_FFN = 512
TL_PREP = 512
TL_GDN = 512
GDN_SEQS_PER_STEP = 2
GDN_PREP_PROBLEMS = 16
TQ = 256
TK = 256
FF_CHUNK = 256
NEG = -1e30
LOG2E = 1.0 / math.log(2.0)
INT_MIN = -2 ** 31


def _cparams(n_axes):
    return pltpu.CompilerParams(dimension_semantics=("arbitrary",) * n_axes, vmem_limit_bytes=VMEM_LIMIT)


def _dot(a, b, **kw):
    return jnp.dot(a, b, preferred_element_type=F32, **kw)


def _dot_nt(a, b, **kw):
    return lax.dot_general(a, b, (((1,), (1,)), ((), ())), preferred_element_type=F32, **kw)


def _dot_tn(a, b, **kw):
    return lax.dot_general(a, b, (((0,), (0,)), ((), ())), preferred_element_type=F32, **kw)


def _rms(x, g):
    return x * lax.rsqrt(jnp.mean(x * x, axis=-1, keepdims=True) + EPS) * g


def _sigmoid(x):
    return 1.0 / (1.0 + jnp.exp2(x * -LOG2E))


def _silu(x):
    return x / (1.0 + jnp.exp2(x * -LOG2E))


def _gelu_tanh(x):
    k0 = -2.0 * math.sqrt(2.0 / math.pi) * LOG2E
    return x / (1.0 + jnp.exp2(x * (k0 + (k0 * 0.044715) * (x * x))))


def _softplus(x):
    return jnp.maximum(x, 0.0) + jnp.log(1.0 + jnp.exp(-jnp.abs(x)))


def _in_proj_body(h_ref, g_ref, w_ref, wg_ref, o_ref, og_ref, *, nout):
    xn = _rms(h_ref[...], g_ref[...]).astype(BF16)
    for c in range(0, nout, 512):
        e = min(c + 512, nout)
        o_ref[:, c:e] = _dot(xn, w_ref[:, c:e]).astype(BF16)
    og_ref[...] = _dot(xn, wg_ref[...])


def _in_proj(h, g, w, wg):
    n, d = h.shape
    nout = w.shape[1]
    return pl.pallas_call(
        functools.partial(_in_proj_body, nout=nout),
        grid=(n // TM_PROJ,),
        in_specs=[
            pl.BlockSpec((TM_PROJ, d), lambda i: (i, 0)),
            pl.BlockSpec((1, d), lambda i: (0, 0)),
            pl.BlockSpec((d, nout), lambda i: (0, 0)),
            pl.BlockSpec((d, LANES), lambda i: (0, 0)),
        ],
        out_specs=[
            pl.BlockSpec((TM_PROJ, nout), lambda i: (i, 0)),
            pl.BlockSpec((TM_PROJ, LANES), lambda i: (i, 0)),
        ],
        out_shape=[jax.ShapeDtypeStruct((n, nout), BF16), jax.ShapeDtypeStruct((n, LANES), F32)],
        compiler_params=_cparams(1),
        name="in_proj",
    )(h, g, w, wg)


def _out_proj_body(h_ref, y1_ref, y2_ref, w_ref, o_ref):
    half = y1_ref.shape[1]
    o_ref[...] = h_ref[...] + _dot(y1_ref[...], w_ref[:half, :]) + _dot(y2_ref[...], w_ref[half:, :])


def _out_proj(h, y1, y2, w):
    n, d = h.shape
    half = y1.shape[1]
    return pl.pallas_call(
        _out_proj_body,
        grid=(n // TM_PROJ,),
        in_specs=[
            pl.BlockSpec((TM_PROJ, d), lambda i: (i, 0)),
            pl.BlockSpec((TM_PROJ, half), lambda i: (i, 0)),
            pl.BlockSpec((TM_PROJ, half), lambda i: (i, 0)),
            pl.BlockSpec((2 * half, d), lambda i: (0, 0)),
        ],
        out_specs=pl.BlockSpec((TM_PROJ, d), lambda i: (i, 0)),
        out_shape=jax.ShapeDtypeStruct((n, d), F32),
        compiler_params=_cparams(1),
        name="out_proj",
    )(h, y1, y2, w)


def _ffn_body(h_ref, halo_ref, g_ref, wup_ref, cw_ref, cb_ref, wdn_ref, o_ref, xn_ref, act_ref, *, tiles_per_seq):
    tm = h_ref.shape[0]
    i = pl.program_id(0)
    h = h_ref[...]
    g = g_ref[...]
    hb = SUBLANES_BF16
    xn_ref[hb:, :] = _rms(h, g).astype(BF16)
    keep = jnp.where(i % tiles_per_seq == 0, 0.0, 1.0)
    xn_ref[:hb, :] = (_rms(halo_ref[...], g) * keep).astype(BF16)
    xn = xn_ref[...]

    def up_pair(c):
        return [_dot(xn, wup_ref[:, base:base + FF_CHUNK]) for base in (c, D_FF + c)]

    def conv(up, base):
        cw = cw_ref[:, base:base + FF_CHUNK]
        return (cw[0:1] * up[hb - 2:tm + hb - 2] + cw[1:2] * up[hb - 1:tm + hb - 1] + cw[2:3] * up[hb:tm + hb]
                + cb_ref[:, base:base + FF_CHUNK])

    chunks = list(range(0, D_FF, FF_CHUNK))
    ups = up_pair(chunks[0])
    for n, c in enumerate(chunks):
        nxt = up_pair(chunks[n + 1]) if n + 1 < len(chunks) else None
        act_ref[:, c:c + FF_CHUNK] = (_silu(conv(ups[0], c)) * conv(ups[1], D_FF + c)).astype(BF16)
        ups = nxt
    o_ref[...] = h + _dot(act_ref[...], wdn_ref[...])


def _ffn(h, g, wup, cw, cb, wdn, seq):
    n, d = h.shape
    tiles_per_seq = seq // TM_FFN
    halo_blocks = TM_FFN // SUBLANES_BF16
    const = dict(pipeline_mode=pl.Buffered(1))
    return pl.pallas_call(
        functools.partial(_ffn_body, tiles_per_seq=tiles_per_seq),
        grid=(n // TM_FFN,),
        in_specs=[
            pl.BlockSpec((TM_FFN, d), lambda i: (i, 0)),
            pl.BlockSpec((SUBLANES_BF16, d), lambda i: (jnp.maximum(i * halo_blocks - 1, 0), 0)),
            pl.BlockSpec((1, d), lambda i: (0, 0)),
            pl.BlockSpec((d, 2 * D_FF), lambda i: (0, 0), **const),
            pl.BlockSpec((FFN_CONV, 2 * D_FF), lambda i: (0, 0)),
            pl.BlockSpec((1, 2 * D_FF), lambda i: (0, 0)),
            pl.BlockSpec((D_FF, d), lambda i: (0, 0), **const),
        ],
        out_specs=pl.BlockSpec((TM_FFN, d), lambda i: (i, 0)),
        out_shape=jax.ShapeDtypeStruct((n, d), F32),
        scratch_shapes=[pltpu.VMEM((TM_FFN + SUBLANES_BF16, d), BF16), pltpu.VMEM((TM_FFN, D_FF), BF16)],
        compiler_params=_cparams(1),
        name="ffn",
    )(h, h, g, wup, cw, cb, wdn)


def _even_prep_body(uv_ref, qkv_ref, halo_ref, gate_ref, ws_ref, bs_ref, cw_ref, gp_ref,
                    ya_ref, qkvo_ref, gout_ref, xs_ref):
    tl = uv_ref.shape[0]
    t = pl.program_id(1)
    gw = LANES
    row = lax.broadcasted_iota(I32, (GMLP_CHUNK, GMLP_CHUNK), 0)
    col = lax.broadcasted_iota(I32, (GMLP_CHUNK, GMLP_CHUNK), 1)
    for gi in range(GMLP_GROUPS):
        u = _gelu_tanh(uv_ref[:, gi * gw:(gi + 1) * gw].astype(F32))
        v = _gelu_tanh(uv_ref[:, (GMLP_GROUPS + gi) * gw:(GMLP_GROUPS + gi + 1) * gw].astype(F32))
        vc = v - jnp.mean(v, axis=-1, keepdims=True)
        vn = (vc * lax.rsqrt(jnp.mean(vc * vc, axis=-1, keepdims=True) + EPS)).astype(BF16)
        w = jnp.where(col <= row, ws_ref[gi], 0.0).astype(BF16)
        b = bs_ref[:, gi:gi + 1]
        for c in range(tl // GMLP_CHUNK):
            r0 = c * GMLP_CHUNK
            mixed = _dot(w, vn[r0:r0 + GMLP_CHUNK]) + b
            ya_ref[r0:r0 + GMLP_CHUNK, gi * gw:(gi + 1) * gw] = (u[r0:r0 + GMLP_CHUNK] * mixed).astype(BF16)
    keep = jnp.where(t == 0, 0.0, 1.0)
    hb = SUBLANES_BF16
    for j in range(3 * GDN_HEADS):
        sl = slice(j * gw, (j + 1) * gw)
        xs_ref[:hb, :] = halo_ref[:, sl].astype(F32) * keep
        xs_ref[hb:, :] = qkv_ref[:, sl].astype(F32)
        cw = cw_ref[:, sl]
        y = cw[0:1] * xs_ref[hb - 3:tl + hb - 3, :]
        for k in range(1, GDN_CONV):
            y = y + cw[k:k + 1] * xs_ref[hb - 3 + k:tl + hb - 3 + k, :]
        y = _silu(y)
        if j < 2 * GDN_HEADS:
            y = y * lax.rsqrt(jnp.sum(y * y, axis=-1, keepdims=True) + EPS)
        if j < GDN_HEADS:
            y = y * (GDN_DIM ** -0.5)
        qkvo_ref[:, sl] = y.astype(BF16)
    x = gate_ref[...]
    lane = lax.broadcasted_iota(I32, x.shape, 1)
    beta = _sigmoid(x)
    gdec = -jnp.exp(gp_ref[0:1, :]) * _softplus(x + gp_ref[1:2, :])
    gout_ref[...] = jnp.where(lane < GDN_HEADS, beta, gdec)


def _even_prep(proj, gates, ws, bs_t, cw, gp, batch, seq):
    n = proj.shape[0]
    nt = seq // TL_PREP
    halo_blocks = TL_PREP // SUBLANES_BF16
    qkv_w = 3 * GDN_HEADS * GDN_DIM
    row = lambda b, t: b * nt + t
    return pl.pallas_call(
        _even_prep_body,
        grid=(batch, nt),
        in_specs=[
            pl.BlockSpec((TL_PREP, 2 * GMLP_GROUPS * LANES), lambda b, t: (row(b, t), 0)),
            pl.BlockSpec((TL_PREP, qkv_w), lambda b, t: (row(b, t), 1)),
            pl.BlockSpec((SUBLANES_BF16, qkv_w), lambda b, t: (jnp.maximum(row(b, t) * halo_blocks - 1, 0), 1)),
            pl.BlockSpec((TL_PREP, LANES), lambda b, t: (row(b, t), 0)),
            pl.BlockSpec((GMLP_GROUPS, GMLP_CHUNK, GMLP_CHUNK), lambda b, t: (0, 0, 0)),
            pl.BlockSpec((GMLP_CHUNK, GMLP_GROUPS), lambda b, t: (0, 0)),
            pl.BlockSpec((GDN_CONV, qkv_w), lambda b, t: (0, 0)),
            pl.BlockSpec((2, LANES), lambda b, t: (0, 0)),
        ],
        out_specs=[
            pl.BlockSpec((TL_PREP, GMLP_GROUPS * LANES), lambda b, t: (row(b, t), 0)),
            pl.BlockSpec((TL_PREP, qkv_w), lambda b, t: (row(b, t), 0)),
            pl.BlockSpec((TL_PREP, LANES), lambda b, t: (row(b, t), 0)),
        ],
        out_shape=[
            jax.ShapeDtypeStruct((n, GMLP_GROUPS * LANES), BF16),
            jax.ShapeDtypeStruct((n, qkv_w), BF16),
            jax.ShapeDtypeStruct((n, LANES), F32),
        ],
        scratch_shapes=[pltpu.VMEM((TL_PREP + SUBLANES_BF16, LANES), F32)],
        compiler_params=_cparams(2),
        name="even_prep",
    )(proj, proj, proj, gates, ws, bs_t, cw, gp)


def _split_bf16(x):
    hi = x.astype(BF16)
    return hi, (x - hi.astype(F32)).astype(BF16)


def _mm3(a, b):
    ah, al = _split_bf16(a)
    bh, bl = _split_bf16(b)
    return _dot(ah, bh) + (_dot(ah, bl) + _dot(al, bh))


def _unit_lower_inverses(lows):
    c = lows[0].shape[0]
    row = lax.broadcasted_iota(I32, (c, c), 0)
    col = lax.broadcasted_iota(I32, (c, c), 1)
    eye = jnp.where(row == col, 1.0, 0.0)
    same16 = jnp.right_shift(row, 4) == jnp.right_shift(col, 4)
    same32 = jnp.right_shift(row, 5) == jnp.right_shift(col, 5)
    ps = [jnp.where(same16, -low, 0.0) for low in lows]
    xs = [eye + p for p in ps]
    for _ in range(3):
        ps = [_mm3(p, p) for p in ps]
        xs = [x + _mm3(x, p) for x, p in zip(xs, ps)]
    mid = jnp.logical_and(same32, jnp.logical_not(same16))
    mm1 = lambda a, b: _dot(a.astype(BF16), b.astype(BF16))
    ys = [mm1(jnp.where(mid, low, 0.0), x) for low, x in zip(lows, xs)]
    xs = [x - mm1(x, y) for x, y in zip(xs, ys)]
    ys = [mm1(jnp.where(same32, 0.0, low), x) for low, x in zip(lows, xs)]
    return [x - mm1(x, y) for x, y in zip(xs, ys)]


def _gdn_body(qkv_ref, gate_ref, z_ref, gn_ref, o_ref,
              s_ref, u_ref, w_ref, qg_ref, kdt_ref, intra_ref, egl_ref):
    nseq, tl = qkv_ref.shape[0], qkv_ref.shape[1]
    c = GDN_CHUNK
    d = GDN_DIM
    lanes = [(bi, hd) for bi in range(nseq) for hd in range(GDN_HEADS)]
    flat = lambda bi, hd: bi * GDN_HEADS + hd
    prep_chunks = max(1, GDN_PREP_PROBLEMS // len(lanes))
    t = pl.program_id(1)

    @pl.when(t == 0)
    def _():
        s_ref[...] = jnp.zeros_like(s_ref)

    row = lax.broadcasted_iota(I32, (c, c), 0)
    col = lax.broadcasted_iota(I32, (c, c), 1)
    incl = col <= row
    strict = col < row
    lmat = jnp.where(incl, 1.0, 0.0)
    gn = gn_ref[...]

    def prepare(it, carry):
        cis = [it * prep_chunks + j for j in range(prep_chunks)]
        rows = [pl.ds(pl.multiple_of(ci * c, c), c) for ci in cis]
        gates = {(j, bi): gate_ref[bi, rows[j], :] for j in range(prep_chunks) for bi in range(nseq)}
        gcum = {key: _dot(lmat, g, precision=HIGHEST) for key, g in gates.items()}
        pairs = [(j, bi, hd) for j in range(prep_chunks) for bi, hd in lanes]
        qs = [qkv_ref[bi, rows[j], hd * d:(hd + 1) * d] for j, bi, hd in pairs]
        ks = [qkv_ref[bi, rows[j], (GDN_HEADS + hd) * d:(GDN_HEADS + hd + 1) * d] for j, bi, hd in pairs]
        vs = [qkv_ref[bi, rows[j], (2 * GDN_HEADS + hd) * d:(2 * GDN_HEADS + hd + 1) * d] for j, bi, hd in pairs]
        beta = [jnp.broadcast_to(gates[j, bi][:, hd:hd + 1], (c, d)) for j, bi, hd in pairs]
        gc = [jnp.broadcast_to(gcum[j, bi][:, GDN_HEADS + hd:GDN_HEADS + hd + 1], (c, d)) for j, bi, hd in pairs]
        decay = [jnp.where(incl, jnp.exp(jnp.where(incl, g[:, :c] - g.T[:c, :], 0.0)), 0.0) for g in gc]
        kb = [k.astype(F32) * b for k, b in zip(ks, beta)]
        lows = [jnp.where(strict, _dot_nt(x.astype(BF16), k) * dc, 0.0) for x, k, dc in zip(kb, ks, decay)]
        tinv = [x.astype(BF16) for x in _unit_lower_inverses(lows)]
        egc = [jnp.exp(g) for g in gc]
        rhs = [jnp.concatenate([(v.astype(F32) * b).astype(BF16), (x * e).astype(BF16)], axis=1)
               for v, b, x, e in zip(vs, beta, kb, egc)]
        uw = [_dot(ti, r) for ti, r in zip(tinv, rhs)]
        qk = [_dot_nt(q, k) for q, k in zip(qs, ks)]
        for n, (j, bi, hd) in enumerate(pairs):
            ci, f = cis[j], flat(bi, hd)
            u_ref[ci, f] = uw[n][:, :d]
            w_ref[ci, f] = uw[n][:, d:].astype(BF16)
            intra_ref[ci, f] = jnp.where(incl, qk[n] * decay[n], 0.0).astype(BF16)
            qg_ref[ci, f] = (qs[n].astype(F32) * egc[n]).astype(BF16)
            g_last = gc[n][c - 1:c, :]
            kdt_ref[ci, f] = (ks[n].astype(F32) * jnp.exp(g_last - gc[n])).T.astype(BF16)
            egl_ref[ci, f] = jnp.exp(g_last)
        return carry

    lax.fori_loop(0, tl // c // prep_chunks, prepare, 0)

    def scan(ci, carry):
        r0 = pl.multiple_of(ci * c, c)
        fs = [flat(bi, hd) for bi, hd in lanes]
        states = [s_ref[f] for f in fs]
        sb = [s.astype(BF16) for s in states]
        ws = [_dot(w_ref[ci, f], s) for f, s in zip(fs, sb)]
        qsd = [_dot(qg_ref[ci, f], s) for f, s in zip(fs, sb)]
        vnb = [(u_ref[ci, f] - w).astype(BF16) for f, w in zip(fs, ws)]
        for f, s, v in zip(fs, states, vnb):
            s_ref[f] = s * egl_ref[ci, f] + _dot(kdt_ref[ci, f], v)
        for (bi, hd), f, qd, v in zip(lanes, fs, qsd, vnb):
            out = qd + _dot(intra_ref[ci, f], v)
            z = z_ref[bi, pl.ds(r0, c), hd * d:(hd + 1) * d].astype(F32)
            o_ref[bi, pl.ds(r0, c), hd * d:(hd + 1) * d] = (_rms(out, gn) * _silu(z)).astype(BF16)
        return carry

    lax.fori_loop(0, tl // c, scan, 0)


def _gdn(qkv, gates, proj, gn, batch, seq):
    n = qkv.shape[0]
    nt = seq // TL_GDN
    nc = TL_GDN // GDN_CHUNK
    width = GDN_HEADS * GDN_DIM
    nseq = GDN_SEQS_PER_STEP if batch % GDN_SEQS_PER_STEP == 0 else 1
    nl = nseq * GDN_HEADS
    as3d = lambda a: a.reshape(batch, seq, a.shape[-1])
    out = pl.pallas_call(
        _gdn_body,
        grid=(batch // nseq, nt),
        in_specs=[
            pl.BlockSpec((nseq, TL_GDN, 3 * width), lambda b, t: (b, t, 0)),
            pl.BlockSpec((nseq, TL_GDN, LANES), lambda b, t: (b, t, 0)),
            pl.BlockSpec((nseq, TL_GDN, width), lambda b, t: (b, t, 2)),
            pl.BlockSpec((1, GDN_DIM), lambda b, t: (0, 0)),
        ],
        out_specs=pl.BlockSpec((nseq, TL_GDN, width), lambda b, t: (b, t, 0)),
        out_shape=jax.ShapeDtypeStruct((batch, seq, width), BF16),
        scratch_shapes=[
            pltpu.VMEM((nl, GDN_DIM, GDN_DIM), F32),
            pltpu.VMEM((nc, nl, GDN_CHUNK, GDN_DIM), F32),
            pltpu.VMEM((nc, nl, GDN_CHUNK, GDN_DIM), BF16),
            pltpu.VMEM((nc, nl, GDN_CHUNK, GDN_DIM), BF16),
            pltpu.VMEM((nc, nl, GDN_DIM, GDN_CHUNK), BF16),
            pltpu.VMEM((nc, nl, GDN_CHUNK, GDN_CHUNK), BF16),
            pltpu.VMEM((nc, nl, 1, GDN_DIM), F32),
        ],
        compiler_params=_cparams(2),
        name="gdn",
    )(as3d(qkv), as3d(gates), as3d(proj), gn)
    return out.reshape(n, width)


def _group_mean_sq(x, ones_bd, group):
    x2 = x * x
    hi = x2.astype(BF16)
    lo = (x2 - hi.astype(F32)).astype(BF16)
    return (_dot(hi, ones_bd) + _dot(lo, ones_bd)) * (1.0 / group)


def _odd_prep_body(dq_ref, dk_ref, dv_ref, sq_ref, sk_ref, sv_ref, gq_ref, gk_ref, gsq_ref, gsk_ref, bd64_ref,
                   bd128_ref, dqo_ref, dko_ref, sqo_ref, sko_ref, vt_ref):
    ones = jnp.ones((VT_ROWS - LANES, TK), BF16)
    for j in range(DIFF_HEADS + 1):
        src = sv_ref[...] if j == DIFF_HEADS else dv_ref[:, j * LANES:(j + 1) * LANES]
        xt = src.astype(F32).T
        for c in range(xt.shape[1] // TK):
            vt_ref[j, c, :LANES, :] = xt[:, c * TK:(c + 1) * TK].astype(BF16)
            vt_ref[j, c, LANES:, :] = ones
    bd64 = bd64_ref[...]
    x = dq_ref[...].astype(F32)
    dqo_ref[...] = (x * lax.rsqrt(_group_mean_sq(x, bd64, DIFF_QK) + EPS) * gq_ref[...]
                    * (DIFF_QK ** -0.5 * LOG2E)).astype(BF16)
    x = dk_ref[...].astype(F32)
    dko_ref[...] = (x * lax.rsqrt(_group_mean_sq(x, bd64, DIFF_QK) + EPS) * gk_ref[...]).astype(BF16)
    x = sq_ref[...].astype(F32)
    sqo_ref[...] = (x * lax.rsqrt(_group_mean_sq(x, bd128_ref[...], DSA_DIM) + EPS) * gsq_ref[...]
                    * (DSA_DIM ** -0.5 * LOG2E)).astype(BF16)
    x = sk_ref[...].astype(F32)
    sko_ref[...] = _rms(x, gsk_ref[...]).astype(BF16)


def _odd_prep(proj, gq, gk, gsq, gsk, bd64, bd128, batch, seq):
    n = proj.shape[0]
    w = 512
    nt = seq // TL_PREP
    kt = TL_PREP // TK
    full = lambda shape: pl.BlockSpec(shape, lambda b, t: (0,) * len(shape))
    row = lambda b, t: b * nt + t
    return pl.pallas_call(
        _odd_prep_body,
        grid=(batch, nt),
        in_specs=[
            pl.BlockSpec((TL_PREP, w), lambda b, t: (row(b, t), 0)),
            pl.BlockSpec((TL_PREP, w), lambda b, t: (row(b, t), 1)),
            pl.BlockSpec((TL_PREP, w), lambda b, t: (row(b, t), 2)),
            pl.BlockSpec((TL_PREP, w), lambda b, t: (row(b, t), 3)),
            pl.BlockSpec((TL_PREP, LANES), lambda b, t: (row(b, t), 5 * w // LANES)),
            pl.BlockSpec((TL_PREP, LANES), lambda b, t: (row(b, t), 5 * w // LANES + 1)),
            full((1, w)), full((1, w)), full((1, w)), full((1, LANES)), full((w, w)), full((w, w)),
        ],
        out_specs=[
            pl.BlockSpec((TL_PREP, w), lambda b, t: (row(b, t), 0)),
            pl.BlockSpec((TL_PREP, w), lambda b, t: (row(b, t), 0)),
            pl.BlockSpec((TL_PREP, w), lambda b, t: (row(b, t), 0)),
            pl.BlockSpec((TL_PREP, LANES), lambda b, t: (row(b, t), 0)),
            pl.BlockSpec((None, DIFF_HEADS + 1, kt, VT_ROWS, TK), lambda b, t: (b, 0, t, 0, 0)),
        ],
        out_shape=[
            jax.ShapeDtypeStruct((n, w), BF16), jax.ShapeDtypeStruct((n, w), BF16),
            jax.ShapeDtypeStruct((n, w), BF16), jax.ShapeDtypeStruct((n, LANES), BF16),
            jax.ShapeDtypeStruct((batch, DIFF_HEADS + 1, seq // TK, VT_ROWS, TK), BF16),
        ],
        compiler_params=_cparams(2),
        name="odd_prep",
    )(proj, proj, proj, proj, proj, proj, gq, gk, gsq, gsk, bd64, bd128)


def _bias_tile_index(kb, qb):
    return jnp.clip(kb - (qb - 2), 0, 2)


def _softmax_step_t(s_t, vt, m_ref, acc_ref):
    for g in range(s_t.shape[1] // LANES):
        sl = slice(g * LANES, (g + 1) * LANES)
        s = s_t[:, sl]
        m_prev = m_ref[:, sl]
        m_new = jnp.maximum(m_prev, jnp.max(s, axis=0, keepdims=True))
        alpha = jnp.exp2(m_prev - m_new)
        p = jnp.exp2(s - m_new)
        acc_ref[:, sl] = alpha * acc_ref[:, sl] + _dot(vt, p.astype(BF16))
        m_ref[:, sl] = m_new


def _attend_tiles(logits, values, nblk, s_ref, m_ref, acc_ref):
    streams = range(len(logits))
    m_ref[...] = jnp.full_like(m_ref, NEG)
    acc_ref[...] = jnp.zeros_like(acc_ref)
    for i in streams:
        s_ref[i, 0] = logits[i](0)

    def pair(it, carry):
        kb = 2 * it
        for i in streams:
            s_ref[i, 1] = logits[i](kb + 1)
        for i in streams:
            _softmax_step_t(s_ref[i, 0], values[i](kb), m_ref.at[i], acc_ref.at[i])
        for i in streams:
            s_ref[i, 0] = logits[i](jnp.minimum(kb + 2, nblk - 1))
        for i in streams:
            _softmax_step_t(s_ref[i, 1], values[i](kb + 1), m_ref.at[i], acc_ref.at[i])
        return carry

    lax.fori_loop(0, nblk // 2, pair, 0)

    @pl.when(nblk % 2 == 1)
    def _():
        for i in streams:
            _softmax_step_t(s_ref[i, 0], values[i](nblk - 1), m_ref.at[i], acc_ref.at[i])


def _diff_attn_body(q_ref, k_ref, vt_ref, bias_ref, lam_ref, sg_ref, o_ref,
                    qt_ref, m_ref, acc_ref, s_ref, *, lambda_init):
    qi = pl.program_id(1)
    nblk = qi + 1
    heads = range(DIFF_HEADS)
    for hd in heads:
        qt = q_ref[:, hd * LANES:(hd + 1) * LANES].astype(F32).T
        dim = lax.broadcasted_iota(I32, qt.shape, 0)
        qt_ref[hd, :, :TQ] = jnp.where(dim < DIFF_QK, qt, 0.0).astype(BF16)
        qt_ref[hd, :, TQ:] = jnp.where(dim >= DIFF_QK, qt, 0.0).astype(BF16)

    def logits_of(hd):
        def logits(kb):
            r0 = pl.multiple_of(kb * TK, TK)
            b = bias_ref[hd, _bias_tile_index(kb, qi)]
            s = _dot(k_ref[pl.ds(r0, TK), hd * LANES:(hd + 1) * LANES], qt_ref[hd])
            return s + jnp.concatenate([b, b], axis=1)
        return logits

    _attend_tiles([logits_of(hd) for hd in heads], [lambda kb, hd=hd: vt_ref[hd, kb] for hd in heads],
                  nblk, s_ref, m_ref, acc_ref)

    lf = lam_ref[...]
    lam = (jnp.exp(jnp.sum(lf[0:1] * lf[1:2], axis=-1, keepdims=True))
           - jnp.exp(jnp.sum(lf[2:3] * lf[3:4], axis=-1, keepdims=True)) + lambda_init)
    for hd in heads:
        den = acc_ref[hd, LANES:LANES + 1, :]
        ot = (acc_ref[hd, :LANES, :TQ] / den[:, :TQ] - lam * (acc_ref[hd, :LANES, TQ:] / den[:, TQ:]))
        ot = ot * lax.rsqrt(jnp.mean(ot * ot, axis=0, keepdims=True) + EPS)
        o_ref[:, hd * LANES:(hd + 1) * LANES] = (ot.T * sg_ref[...] * (1.0 - lambda_init)).astype(BF16)


def _diff_attn(dqn, dkn, vt, bias, lam_p, sub_g, batch, seq, lambda_init):
    n = dqn.shape[0]
    nq = seq // TQ
    width = DIFF_HEADS * LANES
    return pl.pallas_call(
        functools.partial(_diff_attn_body, lambda_init=lambda_init),
        grid=(batch, nq),
        in_specs=[
            pl.BlockSpec((TQ, width), lambda b, q: (b * nq + q, 0)),
            pl.BlockSpec((seq, width), lambda b, q: (b, 0)),
            pl.BlockSpec((None, DIFF_HEADS + 1, seq // TK, VT_ROWS, TK), lambda b, q: (b, 0, 0, 0, 0)),
            pl.BlockSpec((DIFF_HEADS, 3, TK, TQ), lambda b, q: (0, 0, 0, 0)),
            pl.BlockSpec((4, DIFF_QK), lambda b, q: (0, 0)),
            pl.BlockSpec((1, LANES), lambda b, q: (0, 0)),
        ],
        out_specs=pl.BlockSpec((TQ, width), lambda b, q: (b * nq + q, 0)),
        out_shape=jax.ShapeDtypeStruct((n, width), BF16),
        scratch_shapes=[
            pltpu.VMEM((DIFF_HEADS, LANES, 2 * TQ), BF16),
            pltpu.VMEM((DIFF_HEADS, 1, 2 * TQ), F32),
            pltpu.VMEM((DIFF_HEADS, VT_ROWS, 2 * TQ), F32),
            pltpu.VMEM((DIFF_HEADS, 2, TK, 2 * TQ), F32),
        ],
        compiler_params=_cparams(2),
        name="diff_attn",
    )(dqn, dkn, vt, bias, lam_p, sub_g)


def _dsa_body(q_ref, k_ref, vt_ref, iq_ref, ik_ref, iw_ref, bias_ref, o_ref,
              qt_ref, qit_ref, wt_ref, keys_ref, hi_ref, lo_ref, t_ref, m_ref, acc_ref, s_ref, *, top_k, pos_bits):
    qb = pl.program_id(1)
    nkb = qb + 1

    for p in range(IDX_HEADS // 2):
        gt = iq_ref[:, p * LANES:(p + 1) * LANES].astype(F32).T.astype(BF16)
        qit_ref[:, (2 * p) * TQ:(2 * p + 1) * TQ] = gt[:IDX_DIM]
        qit_ref[:, (2 * p + 1) * TQ:(2 * p + 2) * TQ] = gt[IDX_DIM:]
    for hd in range(DSA_HEADS):
        qt_ref[:, hd * TQ:(hd + 1) * TQ] = q_ref[:, hd * LANES:(hd + 1) * LANES].astype(F32).T.astype(BF16)
    wscale = (IDX_HEADS ** -0.5) * (IDX_DIM ** -0.5)
    wt_ref[...] = (iw_ref[...] * wscale).T[:IDX_HEADS, :]

    key_i = lax.broadcasted_iota(I32, (TK, TQ), 0)
    qry_i = lax.broadcasted_iota(I32, (TK, TQ), 1)
    causal = key_i <= qry_i

    def index_blocks(kbs, masked):
        iks = [ik_ref[pl.ds(pl.multiple_of(kb * TK, TK), TK), :IDX_DIM] for kb in kbs]
        idxs = [jnp.zeros((TK, TQ), F32) for _ in kbs]
        for hi in range(IDX_HEADS):
            ss = [_dot(ik, qit_ref[:, hi * TQ:(hi + 1) * TQ]) for ik in iks]
            idxs = [idx + jnp.maximum(s, 0.0) * wt_ref[hi:hi + 1, :] for idx, s in zip(idxs, ss)]
        for kb, idx in zip(kbs, idxs):
            idx = jnp.where(idx == 0.0, 0.0, idx)
            bits = pltpu.bitcast(idx, I32)
            key = jnp.where(bits < 0, bits ^ jnp.int32(0x7FFFFFFF), bits)
            if masked:
                key = jnp.where(causal, key, jnp.int32(INT_MIN))
            keys_ref[kb] = key
            hi_ref[kb] = jnp.right_shift(key, 16).astype(I16)
            lo_ref[kb] = ((key & 0xFFFF) - 0x8000).astype(I16)

    def index_pair(i, carry):
        index_blocks([2 * i, 2 * i + 1], False)
        return carry

    lax.fori_loop(0, (nkb - 1) // 2, index_pair, 0)

    @pl.when((nkb - 1) % 2 == 1)
    def _():
        index_blocks([nkb - 2], False)

    index_blocks([nkb - 1], True)

    def count(pred):
        def body(kb, acc):
            x = jnp.where(pred(keys_ref[kb], key_i + kb * TK), 1.0, 0.0)
            parts = [x[i * SUBLANES_F32:(i + 1) * SUBLANES_F32] for i in range(TK // SUBLANES_F32)]
            while len(parts) > 1:
                parts = [parts[i] + parts[i + 1] for i in range(0, len(parts), 2)]
            return acc + parts[0]
        acc = lax.fori_loop(0, nkb, body, jnp.zeros((SUBLANES_F32, TQ), F32))
        return jnp.sum(acc, axis=0, keepdims=True)

    kf = float(top_k)
    rows16 = SUBLANES_BF16
    i16_min = -2 ** 15

    def count16(ref, cand, strict=False):
        c16 = jnp.broadcast_to(cand, (rows16, TQ)).astype(I16)

        def body(kb, acc):
            x = ref[kb]
            parts = []
            for i in range(TK // rows16):
                slab = x[i * rows16:(i + 1) * rows16]
                hit = slab > c16 if strict else slab >= c16
                parts.append(jnp.where(hit, jnp.int16(1), jnp.int16(0)))
            while len(parts) > 1:
                parts = [parts[i] + parts[i + 1] for i in range(0, len(parts), 2)]
            return acc + parts[0]

        acc = lax.fori_loop(0, nkb, body, jnp.zeros((rows16, TQ), I16))
        return jnp.sum(acc.astype(F32), axis=0, keepdims=True)

    def digit_select(ref, need, fallback):
        c0 = count16(ref, jnp.zeros((1, TQ), I32))
        ok0 = c0 >= need
        d0 = jnp.where(ok0, jnp.int32(0), jnp.int32(i16_min))
        n0 = jnp.where(ok0, c0, fallback)

        def bit_step(i, carry):
            d, n = carry
            cand = d | jnp.left_shift(jnp.int32(1), 14 - i)
            cnt = count16(ref, cand)
            ok = cnt >= need
            return jnp.where(ok, cand, d), jnp.where(ok, cnt, n)

        return lax.fori_loop(0, 15, bit_step, (d0, n0))

    total = (nkb * TK).astype(F32) * jnp.ones((1, TQ), F32)
    t_hi, n_hi = digit_select(hi_ref, kf, total)
    n_above = count16(hi_ref, t_hi, strict=True)
    t_hi16 = jnp.broadcast_to(t_hi, (rows16, TQ)).astype(I16)

    def restrict(kb, carry):
        for i in range(TK // rows16):
            sl = slice(i * rows16, (i + 1) * rows16)
            lo_ref[kb, sl, :] = jnp.where(hi_ref[kb, sl, :] == t_hi16, lo_ref[kb, sl, :], jnp.int16(i16_min))
        return carry

    lax.fori_loop(0, nkb, restrict, 0)
    t_lo, n_lo = digit_select(lo_ref, kf - n_above, n_hi - n_above)
    t = jnp.left_shift(t_hi, 16) + (t_lo + 0x8000)
    cge = n_above + n_lo
    t_ref[...] = t

    @pl.when(jnp.max(cge) > kf)
    def _():
        tt = t_ref[...]
        r = kf - count(lambda kk, pos: kk > tt)

        def pos_step(i, pcut):
            cand = pcut | jnp.left_shift(jnp.int32(1), pos_bits - 1 - i)
            tied_before = count(lambda kk, pos: jnp.logical_and(kk == tt, pos < cand))
            return jnp.where(tied_before <= r, cand, pcut)

        pcut = lax.fori_loop(0, pos_bits, pos_step, jnp.zeros((1, TQ), I32))

        def demote(kb, carry):
            kk = keys_ref[kb]
            drop = jnp.logical_and(kk == tt, key_i + kb * TK >= pcut)
            keys_ref[kb] = jnp.where(drop, kk - 1, kk)
            return carry

        lax.fori_loop(0, nkb, demote, 0)

    def logits_of(hd):
        def logits(kb):
            r0 = pl.multiple_of(kb * TK, TK)
            sel = keys_ref[kb] >= t_ref[...]
            s = _dot(k_ref[pl.ds(r0, TK), :], qt_ref[:, hd * TQ:(hd + 1) * TQ])
            return jnp.where(sel, s + bias_ref[hd, _bias_tile_index(kb, qb)], NEG)
        return logits

    heads = range(DSA_HEADS)
    _attend_tiles([logits_of(hd) for hd in heads], [lambda kb: vt_ref[kb]] * DSA_HEADS, nkb, s_ref, m_ref, acc_ref)

    for hd in heads:
        o_ref[:, hd * LANES:(hd + 1) * LANES] = (acc_ref[hd, :LANES, :] / acc_ref[hd, LANES:LANES + 1, :]).T.astype(BF16)


def _dsa(sqn, skn, vt, proj, iw, bias, batch, seq):
    n = sqn.shape[0]
    nq = seq // TQ
    top_k = min(IDX_TOPK_MAX, seq // 4)
    pos_bits = int(seq).bit_length()
    iq_block = 4
    ik_block = (5 * 512 + 2 * LANES) // LANES
    return pl.pallas_call(
        functools.partial(_dsa_body, top_k=top_k, pos_bits=pos_bits),
        grid=(batch, nq),
        in_specs=[
            pl.BlockSpec((TQ, DSA_HEADS * LANES), lambda b, q: (b * nq + q, 0)),
            pl.BlockSpec((seq, LANES), lambda b, q: (b, 0)),
            pl.BlockSpec((None, None, seq // TK, VT_ROWS, TK), lambda b, q: (b, DIFF_HEADS, 0, 0, 0)),
            pl.BlockSpec((TQ, IDX_HEADS * IDX_DIM), lambda b, q: (b * nq + q, iq_block)),
            pl.BlockSpec((seq, LANES), lambda b, q: (b, ik_block)),
            pl.BlockSpec((TQ, LANES), lambda b, q: (b * nq + q, 0)),
            pl.BlockSpec((DSA_HEADS, 3, TK, TQ), lambda b, q: (1, 0, 0, 0)),
        ],
        out_specs=pl.BlockSpec((TQ, DSA_HEADS * LANES), lambda b, q: (b * nq + q, 0)),
        out_shape=jax.ShapeDtypeStruct((n, DSA_HEADS * LANES), BF16),
        scratch_shapes=[
            pltpu.VMEM((LANES, DSA_HEADS * TQ), BF16),
            pltpu.VMEM((IDX_DIM, IDX_HEADS * TQ), BF16),
            pltpu.VMEM((IDX_HEADS, TQ), F32),
            pltpu.VMEM((seq // TK, TK, TQ), I32),
            pltpu.VMEM((seq // TK, TK, TQ), I16),
            pltpu.VMEM((seq // TK, TK, TQ), I16),
            pltpu.VMEM((1, TQ), I32),
            pltpu.VMEM((DSA_HEADS, 1, TQ), F32),
            pltpu.VMEM((DSA_HEADS, VT_ROWS, TQ), F32),
            pltpu.VMEM((DSA_HEADS, 2, TK, TQ), F32),
        ],
        compiler_params=_cparams(2),
        name="dsa",
    )(sqn, skn, vt, proj, proj, iw, bias)


def _rel_bucket(dist):
    exact = REL_BUCKETS // 2
    n = jnp.maximum(dist, 0)
    nf = jnp.maximum(n, exact).astype(F32)
    far = exact + (jnp.log(nf / exact) / math.log(REL_MAX_DIST / exact) * (REL_BUCKETS - exact)).astype(I32)
    return jnp.where(n < exact, n, jnp.minimum(far, REL_BUCKETS - 1))


def _bias_tiles_body(tab_ref, bucket_ref, o_ref):
    hd = pl.program_id(0)
    nh = DIFF_HEADS + DSA_HEADS
    key = lax.broadcasted_iota(I32, (TK, TQ), 0)
    qry = lax.broadcasted_iota(I32, (TK, TQ), 1)
    for tile in range(3):
        bucket = bucket_ref[tile]
        acc = jnp.zeros(bucket.shape, F32)
        for b in range(REL_BUCKETS):
            acc = jnp.where(bucket == b, tab_ref[b * nh + hd], acc)
        if tile == 2:
            acc = jnp.where(key <= qry, acc, NEG)
        o_ref[tile] = acc * LOG2E


def _bias_tables(rel_bias):
    nh = DIFF_HEADS + DSA_HEADS
    assert TK + 1 >= REL_MAX_DIST
    j = jnp.arange(TK, dtype=I32)[:, None]
    i = jnp.arange(TQ, dtype=I32)[None, :]
    dist = jnp.stack([jnp.full((TK, TQ), 2 * TK, I32), i - j + TK, i - j])
    return pl.pallas_call(
        _bias_tiles_body,
        grid=(nh,),
        in_specs=[pl.BlockSpec(memory_space=pltpu.SMEM), pl.BlockSpec((3, TK, TQ), lambda h: (0, 0, 0))],
        out_specs=pl.BlockSpec((None, 3, TK, TQ), lambda h: (h, 0, 0, 0)),
        out_shape=jax.ShapeDtypeStruct((nh, 3, TK, TQ), F32),
        compiler_params=_cparams(1),
        name="bias_tiles",
    )(rel_bias.astype(F32).reshape(-1), _rel_bucket(dist))


def _block_diag_ones(width, group):
    r = np.arange(width)
    return jnp.asarray((r[:, None] // group) == (r[None, :] // group), dtype=BF16)


def _pad_lanes(w, width=LANES):
    return jnp.pad(w, ((0, 0), (0, width - w.shape[1])))


def kernel(x, rel_bias, mix_norm_g, ev_w_in, ev_w_out, gmlp_w_s, gmlp_b_s, gdn_conv_w, gdn_a_log, gdn_dt_bias,
           gdn_norm_g, od_w_in, od_w_out, diff_q_norm_g, diff_k_norm_g, diff_lambda, diff_sub_norm_g,
           dsa_q_norm_g, dsa_k_norm_g, ffn_norm_g, ffn_w_up, ffn_conv_w, ffn_conv_b, ffn_w_down):
    batch, seq, d = x.shape
    n = batch * seq
    depth = mix_norm_g.shape[0]
    assert d == D_MODEL and seq % max(TM_PROJ, TL_PREP, TL_GDN, TQ) == 0
    h = x.reshape(n, d)
    bias_tiles = _bias_tables(rel_bias)
    bd64 = _block_diag_ones(512, DIFF_QK)
    bd128 = _block_diag_ones(512, DSA_DIM)
    gw = GMLP_GROUPS * LANES
    qkv_w = 3 * GDN_HEADS * GDN_DIM

    for layer in range(depth):
        j = layer // 2
        g_mix = mix_norm_g[layer].reshape(1, d)
        if layer % 2 == 0:
            w = ev_w_in[j]
            o_u, o_v, o_qkv = 0, gw, 2 * gw
            o_b = o_qkv + qkv_w
            o_a = o_b + GDN_HEADS
            o_z = o_a + GDN_HEADS
            w_main = jnp.concatenate(
                [w[:, o_u:o_v], w[:, o_v:o_qkv], w[:, o_z:o_z + GDN_HEADS * GDN_DIM], w[:, o_qkv:o_b]],
                axis=1).astype(BF16)
            w_gate = _pad_lanes(w[:, o_b:o_z]).astype(BF16)
            proj, gates = _in_proj(h, g_mix, w_main, w_gate)
            gp = jnp.stack([
                _pad_lanes(jnp.concatenate([jnp.zeros((GDN_HEADS,), F32), gdn_a_log[j]])[None])[0],
                _pad_lanes(jnp.concatenate([jnp.zeros((GDN_HEADS,), F32), gdn_dt_bias[j]])[None])[0]])
            y_a, qkvn, gates2 = _even_prep(proj, gates, gmlp_w_s[j], gmlp_b_s[j].T, gdn_conv_w[j], gp, batch, seq)
            y_b = _gdn(qkvn, gates2, proj, gdn_norm_g[j].reshape(1, GDN_DIM), batch, seq)
            h = _out_proj(h, y_a, y_b, ev_w_out[j].astype(BF16))
        else:
            lambda_init = 0.8 - 0.6 * math.exp(-0.3 * layer)
            w = od_w_in[j]
            c = np.cumsum([0, 512, 512, 512, 512, 128, 128, 512, 64, 8])
            dq, dk, dv, sq, sk, sv, iq, ik, iw = [w[:, c[i]:c[i + 1]] for i in range(9)]
            w_main = jnp.concatenate([dq, dk, dv, sq, iq, sk, sv, ik, ik], axis=1).astype(BF16)
            w_gate = _pad_lanes(iw).astype(BF16)
            proj, iw_out = _in_proj(h, g_mix, w_main, w_gate)
            dqn, dkn, sqn, skn, vt = _odd_prep(
                proj,
                jnp.tile(diff_q_norm_g[j], 2 * DIFF_HEADS)[None], jnp.tile(diff_k_norm_g[j], 2 * DIFF_HEADS)[None],
                jnp.tile(dsa_q_norm_g[j], DSA_HEADS)[None], dsa_k_norm_g[j][None], bd64, bd128, batch, seq)
            y_c = _diff_attn(dqn, dkn, vt, bias_tiles, diff_lambda[j], diff_sub_norm_g[j][None],
                             batch, seq, lambda_init)
            y_d = _dsa(sqn, skn, vt, proj, iw_out, bias_tiles, batch, seq)
            h = _out_proj(h, y_c, y_d, od_w_out[j].astype(BF16))
        h = _ffn(h, ffn_norm_g[layer].reshape(1, d), ffn_w_up[layer].astype(BF16), ffn_conv_w[layer],
                 ffn_conv_b[layer].reshape(1, 2 * D_FF), ffn_w_down[layer].astype(BF16), seq)
    return h.reshape(batch, seq, d)
```

```python
import functools
import math

import numpy as np
import jax
import jax.numpy as jnp
from jax import lax
from jax.experimental import pallas as pl
from jax.experimental.pallas import tpu as pltpu

F32 = jnp.float32
BF16 = jnp.bfloat16
I32 = jnp.int32
I16 = jnp.int16
HIGHEST = lax.Precision.HIGHEST

D_MODEL = 1024
GMLP_GROUPS = 4
GMLP_CHUNK = 128
GDN_HEADS = 4
GDN_DIM = 128
GDN_CHUNK = 64
GDN_CONV = 4
DIFF_HEADS = 4
DIFF_QK = 64
DSA_HEADS = 4
DSA_DIM = 128
IDX_HEADS = 8
IDX_DIM = 64
IDX_TOPK_MAX = 256
REL_BUCKETS = 32
REL_MAX_DIST = 128
D_FF = 2816
FFN_CONV = 3
EPS = 1e-6

LANES = 128
SUBLANES_F32 = 8
SUBLANES_BF16 = 16
VMEM_LIMIT = 56 * 1024 * 1024
VT_ROWS = LANES + SUBLANES_BF16

TM_PROJ = 512
TM_FFN = 512
TL_PREP = 512
TL_GDN = 512
GDN_SEQS_PER_STEP = 2
GDN_PREP_PROBLEMS = 16
TQ = 256
TK = 256
FF_CHUNK = 256
NEG = -1e30
LOG2E = 1.0 / math.log(2.0)
INT_MIN = -2 ** 31


def _cparams(n_axes):
    return pltpu.CompilerParams(dimension_semantics=("arbitrary",) * n_axes, vmem_limit_bytes=VMEM_LIMIT)


def _dot(a, b, **kw):
    return jnp.dot(a, b, preferred_element_type=F32, **kw)


def _dot_nt(a, b, **kw):
    return lax.dot_general(a, b, (((1,), (1,)), ((), ())), preferred_element_type=F32, **kw)


def _dot_tn(a, b, **kw):
    return lax.dot_general(a, b, (((0,), (0,)), ((), ())), preferred_element_type=F32, **kw)


def _rms(x, g):
    return x * lax.rsqrt(jnp.mean(x * x, axis=-1, keepdims=True) + EPS) * g


def _sigmoid(x):
    return 1.0 / (1.0 + jnp.exp2(x * -LOG2E))


def _silu(x):
    return x / (1.0 + jnp.exp2(x * -LOG2E))


def _gelu_tanh(x):
    k0 = -2.0 * math.sqrt(2.0 / math.pi) * LOG2E
    return x / (1.0 + jnp.exp2(x * (k0 + (k0 * 0.044715) * (x * x))))


def _softplus(x):
    return jnp.maximum(x, 0.0) + jnp.log(1.0 + jnp.exp(-jnp.abs(x)))


def _in_proj_body(h_ref, g_ref, w_ref, wg_ref, o_ref, og_ref, *, nout):
    xn = _rms(h_ref[...], g_ref[...]).astype(BF16)
    for c in range(0, nout, 512):
        e = min(c + 512, nout)
        o_ref[:, c:e] = _dot(xn, w_ref[:, c:e]).astype(BF16)
    og_ref[...] = _dot(xn, wg_ref[...])


def _in_proj(h, g, w, wg):
    n, d = h.shape
    nout = w.shape[1]
    return pl.pallas_call(
        functools.partial(_in_proj_body, nout=nout),
        grid=(n // TM_PROJ,),
        in_specs=[
            pl.BlockSpec((TM_PROJ, d), lambda i: (i, 0)),
            pl.BlockSpec((1, d), lambda i: (0, 0)),
            pl.BlockSpec((d, nout), lambda i: (0, 0)),
            pl.BlockSpec((d, LANES), lambda i: (0, 0)),
        ],
        out_specs=[
            pl.BlockSpec((TM_PROJ, nout), lambda i: (i, 0)),
            pl.BlockSpec((TM_PROJ, LANES), lambda i: (i, 0)),
        ],
        out_shape=[jax.ShapeDtypeStruct((n, nout), BF16), jax.ShapeDtypeStruct((n, LANES), F32)],
        compiler_params=_cparams(1),
        name="in_proj",
    )(h, g, w, wg)


def _out_proj_body(h_ref, y1_ref, y2_ref, w_ref, o_ref):
    half = y1_ref.shape[1]
    o_ref[...] = h_ref[...] + _dot(y1_ref[...], w_ref[:half, :]) + _dot(y2_ref[...], w_ref[half:, :])


def _out_proj(h, y1, y2, w):
    n, d = h.shape
    half = y1.shape[1]
    return pl.pallas_call(
        _out_proj_body,
        grid=(n // TM_PROJ,),
        in_specs=[
            pl.BlockSpec((TM_PROJ, d), lambda i: (i, 0)),
            pl.BlockSpec((TM_PROJ, half), lambda i: (i, 0)),
            pl.BlockSpec((TM_PROJ, half), lambda i: (i, 0)),
            pl.BlockSpec((2 * half, d), lambda i: (0, 0)),
        ],
        out_specs=pl.BlockSpec((TM_PROJ, d), lambda i: (i, 0)),
        out_shape=jax.ShapeDtypeStruct((n, d), F32),
        compiler_params=_cparams(1),
        name="out_proj",
    )(h, y1, y2, w)


def _ffn_body(h_ref, halo_ref, g_ref, wup_ref, cw_ref, cb_ref, wdn_ref, o_ref, xn_ref, act_ref, *, tiles_per_seq):
    tm = h_ref.shape[0]
    i = pl.program_id(0)
    h = h_ref[...]
    g = g_ref[...]
    hb = SUBLANES_BF16
    xn_ref[hb:, :] = _rms(h, g).astype(BF16)
    keep = jnp.where(i % tiles_per_seq == 0, 0.0, 1.0)
    xn_ref[:hb, :] = (_rms(halo_ref[...], g) * keep).astype(BF16)
    xn = xn_ref[...]

    def up_pair(c):
        return [_dot(xn, wup_ref[:, base:base + FF_CHUNK]) for base in (c, D_FF + c)]

    def conv(up, base):
        cw = cw_ref[:, base:base + FF_CHUNK]
        return (cw[0:1] * up[hb - 2:tm + hb - 2] + cw[1:2] * up[hb - 1:tm + hb - 1] + cw[2:3] * up[hb:tm + hb]
                + cb_ref[:, base:base + FF_CHUNK])

    chunks = list(range(0, D_FF, FF_CHUNK))
    ups = up_pair(chunks[0])
    for n, c in enumerate(chunks):
        nxt = up_pair(chunks[n + 1]) if n + 1 < len(chunks) else None
        act_ref[:, c:c + FF_CHUNK] = (_silu(conv(ups[0], c)) * conv(ups[1], D_FF + c)).astype(BF16)
        ups = nxt
    o_ref[...] = h + _dot(act_ref[...], wdn_ref[...])


def _ffn(h, g, wup, cw, cb, wdn, seq):
    n, d = h.shape
    tiles_per_seq = seq // TM_FFN
    halo_blocks = TM_FFN // SUBLANES_BF16
    const = dict(pipeline_mode=pl.Buffered(1))
    return pl.pallas_call(
        functools.partial(_ffn_body, tiles_per_seq=tiles_per_seq),
        grid=(n // TM_FFN,),
        in_specs=[
            pl.BlockSpec((TM_FFN, d), lambda i: (i, 0)),
            pl.BlockSpec((SUBLANES_BF16, d), lambda i: (jnp.maximum(i * halo_blocks - 1, 0), 0)),
            pl.BlockSpec((1, d), lambda i: (0, 0)),
            pl.BlockSpec((d, 2 * D_FF), lambda i: (0, 0), **const),
            pl.BlockSpec((FFN_CONV, 2 * D_FF), lambda i: (0, 0)),
            pl.BlockSpec((1, 2 * D_FF), lambda i: (0, 0)),
            pl.BlockSpec((D_FF, d), lambda i: (0, 0), **const),
        ],
        out_specs=pl.BlockSpec((TM_FFN, d), lambda i: (i, 0)),
        out_shape=jax.ShapeDtypeStruct((n, d), F32),
        scratch_shapes=[pltpu.VMEM((TM_FFN + SUBLANES_BF16, d), BF16), pltpu.VMEM((TM_FFN, D_FF), BF16)],
        compiler_params=_cparams(1),
        name="ffn",
    )(h, h, g, wup, cw, cb, wdn)


def _even_prep_body(uv_ref, qkv_ref, halo_ref, gate_ref, ws_ref, bs_ref, cw_ref, gp_ref,
                    ya_ref, qkvo_ref, gout_ref, xs_ref):
    tl = uv_ref.shape[0]
    t = pl.program_id(1)
    gw = LANES
    row = lax.broadcasted_iota(I32, (GMLP_CHUNK, GMLP_CHUNK), 0)
    col = lax.broadcasted_iota(I32, (GMLP_CHUNK, GMLP_CHUNK), 1)
    for gi in range(GMLP_GROUPS):
        u = _gelu_tanh(uv_ref[:, gi * gw:(gi + 1) * gw].astype(F32))
        v = _gelu_tanh(uv_ref[:, (GMLP_GROUPS + gi) * gw:(GMLP_GROUPS + gi + 1) * gw].astype(F32))
        vc = v - jnp.mean(v, axis=-1, keepdims=True)
        vn = (vc * lax.rsqrt(jnp.mean(vc * vc, axis=-1, keepdims=True) + EPS)).astype(BF16)
        w = jnp.where(col <= row, ws_ref[gi], 0.0).astype(BF16)
        b = bs_ref[:, gi:gi + 1]
        for c in range(tl // GMLP_CHUNK):
            r0 = c * GMLP_CHUNK
            mixed = _dot(w, vn[r0:r0 + GMLP_CHUNK]) + b
            ya_ref[r0:r0 + GMLP_CHUNK, gi * gw:(gi + 1) * gw] = (u[r0:r0 + GMLP_CHUNK] * mixed).astype(BF16)
    keep = jnp.where(t == 0, 0.0, 1.0)
    hb = SUBLANES_BF16
    for j in range(3 * GDN_HEADS):
        sl = slice(j * gw, (j + 1) * gw)
        xs_ref[:hb, :] = halo_ref[:, sl].astype(F32) * keep
        xs_ref[hb:, :] = qkv_ref[:, sl].astype(F32)
        cw = cw_ref[:, sl]
        y = cw[0:1] * xs_ref[hb - 3:tl + hb - 3, :]
        for k in range(1, GDN_CONV):
            y = y + cw[k:k + 1] * xs_ref[hb - 3 + k:tl + hb - 3 + k, :]
        y = _silu(y)
        if j < 2 * GDN_HEADS:
            y = y * lax.rsqrt(jnp.sum(y * y, axis=-1, keepdims=True) + EPS)
        if j < GDN_HEADS:
            y = y * (GDN_DIM ** -0.5)
        qkvo_ref[:, sl] = y.astype(BF16)
    x = gate_ref[...]
    lane = lax.broadcasted_iota(I32, x.shape, 1)
    beta = _sigmoid(x)
    gdec = -jnp.exp(gp_ref[0:1, :]) * _softplus(x + gp_ref[1:2, :])
    gout_ref[...] = jnp.where(lane < GDN_HEADS, beta, gdec)


def _even_prep(proj, gates, ws, bs_t, cw, gp, batch, seq):
    n = proj.shape[0]
    nt = seq // TL_PREP
    halo_blocks = TL_PREP // SUBLANES_BF16
    qkv_w = 3 * GDN_HEADS * GDN_DIM
    row = lambda b, t: b * nt + t
    return pl.pallas_call(
        _even_prep_body,
        grid=(batch, nt),
        in_specs=[
            pl.BlockSpec((TL_PREP, 2 * GMLP_GROUPS * LANES), lambda b, t: (row(b, t), 0)),
            pl.BlockSpec((TL_PREP, qkv_w), lambda b, t: (row(b, t), 1)),
            pl.BlockSpec((SUBLANES_BF16, qkv_w), lambda b, t: (jnp.maximum(row(b, t) * halo_blocks - 1, 0), 1)),
            pl.BlockSpec((TL_PREP, LANES), lambda b, t: (row(b, t), 0)),
            pl.BlockSpec((GMLP_GROUPS, GMLP_CHUNK, GMLP_CHUNK), lambda b, t: (0, 0, 0)),
            pl.BlockSpec((GMLP_CHUNK, GMLP_GROUPS), lambda b, t: (0, 0)),
            pl.BlockSpec((GDN_CONV, qkv_w), lambda b, t: (0, 0)),
            pl.BlockSpec((2, LANES), lambda b, t: (0, 0)),
        ],
        out_specs=[
            pl.BlockSpec((TL_PREP, GMLP_GROUPS * LANES), lambda b, t: (row(b, t), 0)),
            pl.BlockSpec((TL_PREP, qkv_w), lambda b, t: (row(b, t), 0)),
            pl.BlockSpec((TL_PREP, LANES), lambda b, t: (row(b, t), 0)),
        ],
        out_shape=[
            jax.ShapeDtypeStruct((n, GMLP_GROUPS * LANES), BF16),
            jax.ShapeDtypeStruct((n, qkv_w), BF16),
            jax.ShapeDtypeStruct((n, LANES), F32),
        ],
        scratch_shapes=[pltpu.VMEM((TL_PREP + SUBLANES_BF16, LANES), F32)],
        compiler_params=_cparams(2),
        name="even_prep",
    )(proj, proj, proj, gates, ws, bs_t, cw, gp)


def _split_bf16(x):
    hi = x.astype(BF16)
    return hi, (x - hi.astype(F32)).astype(BF16)


def _mm3(a, b):
    ah, al = _split_bf16(a)
    bh, bl = _split_bf16(b)
    return _dot(ah, bh) + (_dot(ah, bl) + _dot(al, bh))


def _unit_lower_inverses(lows):
    c = lows[0].shape[0]
    row = lax.broadcasted_iota(I32, (c, c), 0)
    col = lax.broadcasted_iota(I32, (c, c), 1)
    eye = jnp.where(row == col, 1.0, 0.0)
    same16 = jnp.right_shift(row, 4) == jnp.right_shift(col, 4)
    same32 = jnp.right_shift(row, 5) == jnp.right_shift(col, 5)
    ps = [jnp.where(same16, -low, 0.0) for low in lows]
    xs = [eye + p for p in ps]
    for _ in range(3):
        ps = [_mm3(p, p) for p in ps]
        xs = [x + _mm3(x, p) for x, p in zip(xs, ps)]
    mid = jnp.logical_and(same32, jnp.logical_not(same16))
    mm1 = lambda a, b: _dot(a.astype(BF16), b.astype(BF16))
    ys = [mm1(jnp.where(mid, low, 0.0), x) for low, x in zip(lows, xs)]
    xs = [x - mm1(x, y) for x, y in zip(xs, ys)]
    ys = [mm1(jnp.where(same32, 0.0, low), x) for low, x in zip(lows, xs)]
    return [x - mm1(x, y) for x, y in zip(xs, ys)]


def _gdn_body(qkv_ref, gate_ref, z_ref, gn_ref, o_ref,
              s_ref, u_ref, w_ref, qg_ref, kdt_ref, intra_ref, egl_ref):
    nseq, tl = qkv_ref.shape[0], qkv_ref.shape[1]
    c = GDN_CHUNK
    d = GDN_DIM
    lanes = [(bi, hd) for bi in range(nseq) for hd in range(GDN_HEADS)]
    flat = lambda bi, hd: bi * GDN_HEADS + hd
    prep_chunks = max(1, GDN_PREP_PROBLEMS // len(lanes))
    t = pl.program_id(1)

    @pl.when(t == 0)
    def _():
        s_ref[...] = jnp.zeros_like(s_ref)

    row = lax.broadcasted_iota(I32, (c, c), 0)
    col = lax.broadcasted_iota(I32, (c, c), 1)
    incl = col <= row
    strict = col < row
    lmat = jnp.where(incl, 1.0, 0.0)
    gn = gn_ref[...]

    def prepare(it, carry):
        cis = [it * prep_chunks + j for j in range(prep_chunks)]
        rows = [pl.ds(pl.multiple_of(ci * c, c), c) for ci in cis]
        gates = {(j, bi): gate_ref[bi, rows[j], :] for j in range(prep_chunks) for bi in range(nseq)}
        gcum = {key: _dot(lmat, g, precision=HIGHEST) for key, g in gates.items()}
        pairs = [(j, bi, hd) for j in range(prep_chunks) for bi, hd in lanes]
        qs = [qkv_ref[bi, rows[j], hd * d:(hd + 1) * d] for j, bi, hd in pairs]
        ks = [qkv_ref[bi, rows[j], (GDN_HEADS + hd) * d:(GDN_HEADS + hd + 1) * d] for j, bi, hd in pairs]
        vs = [qkv_ref[bi, rows[j], (2 * GDN_HEADS + hd) * d:(2 * GDN_HEADS + hd + 1) * d] for j, bi, hd in pairs]
        beta = [jnp.broadcast_to(gates[j, bi][:, hd:hd + 1], (c, d)) for j, bi, hd in pairs]
        gc = [jnp.broadcast_to(gcum[j, bi][:, GDN_HEADS + hd:GDN_HEADS + hd + 1], (c, d)) for j, bi, hd in pairs]
        decay = [jnp.where(incl, jnp.exp(jnp.where(incl, g[:, :c] - g.T[:c, :], 0.0)), 0.0) for g in gc]
        kb = [k.astype(F32) * b for k, b in zip(ks, beta)]
        lows = [jnp.where(strict, _dot_nt(x.astype(BF16), k) * dc, 0.0) for x, k, dc in zip(kb, ks, decay)]
        tinv = [x.astype(BF16) for x in _unit_lower_inverses(lows)]
        egc = [jnp.exp(g) for g in gc]
        rhs = [jnp.concatenate([(v.astype(F32) * b).astype(BF16), (x * e).astype(BF16)], axis=1)
               for v, b, x, e in zip(vs, beta, kb, egc)]
        uw = [_dot(ti, r) for ti, r in zip(tinv, rhs)]
        qk = [_dot_nt(q, k) for q, k in zip(qs, ks)]
        for n, (j, bi, hd) in enumerate(pairs):
            ci, f = cis[j], flat(bi, hd)
            u_ref[ci, f] = uw[n][:, :d]
            w_ref[ci, f] = uw[n][:, d:].astype(BF16)
            intra_ref[ci, f] = jnp.where(incl, qk[n] * decay[n], 0.0).astype(BF16)
            qg_ref[ci, f] = (qs[n].astype(F32) * egc[n]).astype(BF16)
            g_last = gc[n][c - 1:c, :]
            kdt_ref[ci, f] = (ks[n].astype(F32) * jnp.exp(g_last - gc[n])).T.astype(BF16)
            egl_ref[ci, f] = jnp.exp(g_last)
        return carry

    lax.fori_loop(0, tl // c // prep_chunks, prepare, 0)

    def scan(ci, carry):
        r0 = pl.multiple_of(ci * c, c)
        fs = [flat(bi, hd) for bi, hd in lanes]
        states = [s_ref[f] for f in fs]
        sb = [s.astype(BF16) for s in states]
        ws = [_dot(w_ref[ci, f], s) for f, s in zip(fs, sb)]
        qsd = [_dot(qg_ref[ci, f], s) for f, s in zip(fs, sb)]
        vnb = [(u_ref[ci, f] - w).astype(BF16) for f, w in zip(fs, ws)]
        for f, s, v in zip(fs, states, vnb):
            s_ref[f] = s * egl_ref[ci, f] + _dot(kdt_ref[ci, f], v)
        for (bi, hd), f, qd, v in zip(lanes, fs, qsd, vnb):
            out = qd + _dot(intra_ref[ci, f], v)
            z = z_ref[bi, pl.ds(r0, c), hd * d:(hd + 1) * d].astype(F32)
            o_ref[bi, pl.ds(r0, c), hd * d:(hd + 1) * d] = (_rms(out, gn) * _silu(z)).astype(BF16)
        return carry

    lax.fori_loop(0, tl // c, scan, 0)


def _gdn(qkv, gates, proj, gn, batch, seq):
    n = qkv.shape[0]
    nt = seq // TL_GDN
    nc = TL_GDN // GDN_CHUNK
    width = GDN_HEADS * GDN_DIM
    nseq = GDN_SEQS_PER_STEP if batch % GDN_SEQS_PER_STEP == 0 else 1
    nl = nseq * GDN_HEADS
    as3d = lambda a: a.reshape(batch, seq, a.shape[-1])
    out = pl.pallas_call(
        _gdn_body,
        grid=(batch // nseq, nt),
        in_specs=[
            pl.BlockSpec((nseq, TL_GDN, 3 * width), lambda b, t: (b, t, 0)),
            pl.BlockSpec((nseq, TL_GDN, LANES), lambda b, t: (b, t, 0)),
            pl.BlockSpec((nseq, TL_GDN, width), lambda b, t: (b, t, 2)),
            pl.BlockSpec((1, GDN_DIM), lambda b, t: (0, 0)),
        ],
        out_specs=pl.BlockSpec((nseq, TL_GDN, width), lambda b, t: (b, t, 0)),
        out_shape=jax.ShapeDtypeStruct((batch, seq, width), BF16),
        scratch_shapes=[
            pltpu.VMEM((nl, GDN_DIM, GDN_DIM), F32),
            pltpu.VMEM((nc, nl, GDN_CHUNK, GDN_DIM), F32),
            pltpu.VMEM((nc, nl, GDN_CHUNK, GDN_DIM), BF16),
            pltpu.VMEM((nc, nl, GDN_CHUNK, GDN_DIM), BF16),
            pltpu.VMEM((nc, nl, GDN_DIM, GDN_CHUNK), BF16),
            pltpu.VMEM((nc, nl, GDN_CHUNK, GDN_CHUNK), BF16),
            pltpu.VMEM((nc, nl, 1, GDN_DIM), F32),
        ],
        compiler_params=_cparams(2),
        name="gdn",
    )(as3d(qkv), as3d(gates), as3d(proj), gn)
    return out.reshape(n, width)


def _group_mean_sq(x, ones_bd, group):
    x2 = x * x
    hi = x2.astype(BF16)
    lo = (x2 - hi.astype(F32)).astype(BF16)
    return (_dot(hi, ones_bd) + _dot(lo, ones_bd)) * (1.0 / group)


def _odd_prep_body(dq_ref, dk_ref, dv_ref, sq_ref, sk_ref, sv_ref, gq_ref, gk_ref, gsq_ref, gsk_ref, bd64_ref,
                   bd128_ref, dqo_ref, dko_ref, sqo_ref, sko_ref, vt_ref):
    ones = jnp.ones((VT_ROWS - LANES, TK), BF16)
    for j in range(DIFF_HEADS + 1):
        src = sv_ref[...] if j == DIFF_HEADS else dv_ref[:, j * LANES:(j + 1) * LANES]
        xt = src.astype(F32).T
        for c in range(xt.shape[1] // TK):
            vt_ref[j, c, :LANES, :] = xt[:, c * TK:(c + 1) * TK].astype(BF16)
            vt_ref[j, c, LANES:, :] = ones
    bd64 = bd64_ref[...]
    x = dq_ref[...].astype(F32)
    dqo_ref[...] = (x * lax.rsqrt(_group_mean_sq(x, bd64, DIFF_QK) + EPS) * gq_ref[...]
                    * (DIFF_QK ** -0.5 * LOG2E)).astype(BF16)
    x = dk_ref[...].astype(F32)
    dko_ref[...] = (x * lax.rsqrt(_group_mean_sq(x, bd64, DIFF_QK) + EPS) * gk_ref[...]).astype(BF16)
    x = sq_ref[...].astype(F32)
    sqo_ref[...] = (x * lax.rsqrt(_group_mean_sq(x, bd128_ref[...], DSA_DIM) + EPS) * gsq_ref[...]
                    * (DSA_DIM ** -0.5 * LOG2E)).astype(BF16)
    x = sk_ref[...].astype(F32)
    sko_ref[...] = _rms(x, gsk_ref[...]).astype(BF16)


def _odd_prep(proj, gq, gk, gsq, gsk, bd64, bd128, batch, seq):
    n = proj.shape[0]
    w = 512
    nt = seq // TL_PREP
    kt = TL_PREP // TK
    full = lambda shape: pl.BlockSpec(shape, lambda b, t: (0,) * len(shape))
    row = lambda b, t: b * nt + t
    return pl.pallas_call(
        _odd_prep_body,
        grid=(batch, nt),
        in_specs=[
            pl.BlockSpec((TL_PREP, w), lambda b, t: (row(b, t), 0)),
            pl.BlockSpec((TL_PREP, w), lambda b, t: (row(b, t), 1)),
            pl.BlockSpec((TL_PREP, w), lambda b, t: (row(b, t), 2)),
            pl.BlockSpec((TL_PREP, w), lambda b, t: (row(b, t), 3)),
            pl.BlockSpec((TL_PREP, LANES), lambda b, t: (row(b, t), 5 * w // LANES)),
            pl.BlockSpec((TL_PREP, LANES), lambda b, t: (row(b, t), 5 * w // LANES + 1)),
            full((1, w)), full((1, w)), full((1, w)), full((1, LANES)), full((w, w)), full((w, w)),
        ],
        out_specs=[
            pl.BlockSpec((TL_PREP, w), lambda b, t: (row(b, t), 0)),
            pl.BlockSpec((TL_PREP, w), lambda b, t: (row(b, t), 0)),
            pl.BlockSpec((TL_PREP, w), lambda b, t: (row(b, t), 0)),
            pl.BlockSpec((TL_PREP, LANES), lambda b, t: (row(b, t), 0)),
            pl.BlockSpec((None, DIFF_HEADS + 1, kt, VT_ROWS, TK), lambda b, t: (b, 0, t, 0, 0)),
        ],
        out_shape=[
            jax.ShapeDtypeStruct((n, w), BF16), jax.ShapeDtypeStruct((n, w), BF16),
            jax.ShapeDtypeStruct((n, w), BF16), jax.ShapeDtypeStruct((n, LANES), BF16),
            jax.ShapeDtypeStruct((batch, DIFF_HEADS + 1, seq // TK, VT_ROWS, TK), BF16),
        ],
        compiler_params=_cparams(2),
        name="odd_prep",
    )(proj, proj, proj, proj, proj, proj, gq, gk, gsq, gsk, bd64, bd128)


def _bias_tile_index(kb, qb):
    return jnp.clip(kb - (qb - 2), 0, 2)


def _softmax_step_t(s_t, vt, m_ref, acc_ref):
    for g in range(s_t.shape[1] // LANES):
        sl = slice(g * LANES, (g + 1) * LANES)
        s = s_t[:, sl]
        m_prev = m_ref[:, sl]
        m_new = jnp.maximum(m_prev, jnp.max(s, axis=0, keepdims=True))
        alpha = jnp.exp2(m_prev - m_new)
        p = jnp.exp2(s - m_new)
        acc_ref[:, sl] = alpha * acc_ref[:, sl] + _dot(vt, p.astype(BF16))
        m_ref[:, sl] = m_new


def _attend_tiles(logits, values, nblk, s_ref, m_ref, acc_ref):
    streams = range(len(logits))
    m_ref[...] = jnp.full_like(m_ref, NEG)
    acc_ref[...] = jnp.zeros_like(acc_ref)
    for i in streams:
        s_ref[i, 0] = logits[i](0)

    def pair(it, carry):
        kb = 2 * it
        for i in streams:
            s_ref[i, 1] = logits[i](kb + 1)
        for i in streams:
            _softmax_step_t(s_ref[i, 0], values[i](kb), m_ref.at[i], acc_ref.at[i])
        for i in streams:
            s_ref[i, 0] = logits[i](jnp.minimum(kb + 2, nblk - 1))
        for i in streams:
            _softmax_step_t(s_ref[i, 1], values[i](kb + 1), m_ref.at[i], acc_ref.at[i])
        return carry

    lax.fori_loop(0, nblk // 2, pair, 0)

    @pl.when(nblk % 2 == 1)
    def _():
        for i in streams:
            _softmax_step_t(s_ref[i, 0], values[i](nblk - 1), m_ref.at[i], acc_ref.at[i])


def _diff_attn_body(q_ref, k_ref, vt_ref, bias_ref, lam_ref, sg_ref, o_ref,
                    qt_ref, m_ref, acc_ref, s_ref, *, lambda_init):
    qi = pl.program_id(1)
    nblk = qi + 1
    heads = range(DIFF_HEADS)
    for hd in heads:
        qt = q_ref[:, hd * LANES:(hd + 1) * LANES].astype(F32).T
        dim = lax.broadcasted_iota(I32, qt.shape, 0)
        qt_ref[hd, :, :TQ] = jnp.where(dim < DIFF_QK, qt, 0.0).astype(BF16)
        qt_ref[hd, :, TQ:] = jnp.where(dim >= DIFF_QK, qt, 0.0).astype(BF16)

    def logits_of(hd):
        def logits(kb):
            r0 = pl.multiple_of(kb * TK, TK)
            b = bias_ref[hd, _bias_tile_index(kb, qi)]
            s = _dot(k_ref[pl.ds(r0, TK), hd * LANES:(hd + 1) * LANES], qt_ref[hd])
            return s + jnp.concatenate([b, b], axis=1)
        return logits

    _attend_tiles([logits_of(hd) for hd in heads], [lambda kb, hd=hd: vt_ref[hd, kb] for hd in heads],
                  nblk, s_ref, m_ref, acc_ref)

    lf = lam_ref[...]
    lam = (jnp.exp(jnp.sum(lf[0:1] * lf[1:2], axis=-1, keepdims=True))
           - jnp.exp(jnp.sum(lf[2:3] * lf[3:4], axis=-1, keepdims=True)) + lambda_init)
    for hd in heads:
        den = acc_ref[hd, LANES:LANES + 1, :]
        ot = (acc_ref[hd, :LANES, :TQ] / den[:, :TQ] - lam * (acc_ref[hd, :LANES, TQ:] / den[:, TQ:]))
        ot = ot * lax.rsqrt(jnp.mean(ot * ot, axis=0, keepdims=True) + EPS)
        o_ref[:, hd * LANES:(hd + 1) * LANES] = (ot.T * sg_ref[...] * (1.0 - lambda_init)).astype(BF16)


def _diff_attn(dqn, dkn, vt, bias, lam_p, sub_g, batch, seq, lambda_init):
    n = dqn.shape[0]
    nq = seq // TQ
    width = DIFF_HEADS * LANES
    return pl.pallas_call(
        functools.partial(_diff_attn_body, lambda_init=lambda_init),
        grid=(batch, nq),
        in_specs=[
            pl.BlockSpec((TQ, width), lambda b, q: (b * nq + q, 0)),
            pl.BlockSpec((seq, width), lambda b, q: (b, 0)),
            pl.BlockSpec((None, DIFF_HEADS + 1, seq // TK, VT_ROWS, TK), lambda b, q: (b, 0, 0, 0, 0)),
            pl.BlockSpec((DIFF_HEADS, 3, TK, TQ), lambda b, q: (0, 0, 0, 0)),
            pl.BlockSpec((4, DIFF_QK), lambda b, q: (0, 0)),
            pl.BlockSpec((1, LANES), lambda b, q: (0, 0)),
        ],
        out_specs=pl.BlockSpec((TQ, width), lambda b, q: (b * nq + q, 0)),
        out_shape=jax.ShapeDtypeStruct((n, width), BF16),
        scratch_shapes=[
            pltpu.VMEM((DIFF_HEADS, LANES, 2 * TQ), BF16),
            pltpu.VMEM((DIFF_HEADS, 1, 2 * TQ), F32),
            pltpu.VMEM((DIFF_HEADS, VT_ROWS, 2 * TQ), F32),
            pltpu.VMEM((DIFF_HEADS, 2, TK, 2 * TQ), F32),
        ],
        compiler_params=_cparams(2),
        name="diff_attn",
    )(dqn, dkn, vt, bias, lam_p, sub_g)


def _dsa_body(q_ref, k_ref, vt_ref, iq_ref, ik_ref, iw_ref, bias_ref, o_ref,
              qt_ref, qit_ref, wt_ref, keys_ref, hi_ref, lo_ref, t_ref, m_ref, acc_ref, s_ref, *, top_k, pos_bits):
    qb = pl.program_id(1)
    nkb = qb + 1

    for p in range(IDX_HEADS // 2):
        gt = iq_ref[:, p * LANES:(p + 1) * LANES].astype(F32).T.astype(BF16)
        qit_ref[:, (2 * p) * TQ:(2 * p + 1) * TQ] = gt[:IDX_DIM]
        qit_ref[:, (2 * p + 1) * TQ:(2 * p + 2) * TQ] = gt[IDX_DIM:]
    for hd in range(DSA_HEADS):
        qt_ref[:, hd * TQ:(hd + 1) * TQ] = q_ref[:, hd * LANES:(hd + 1) * LANES].astype(F32).T.astype(BF16)
    wscale = (IDX_HEADS ** -0.5) * (IDX_DIM ** -0.5)
    wt_ref[...] = (iw_ref[...] * wscale).T[:IDX_HEADS, :]

    key_i = lax.broadcasted_iota(I32, (TK, TQ), 0)
    qry_i = lax.broadcasted_iota(I32, (TK, TQ), 1)
    causal = key_i <= qry_i

    def index_blocks(kbs, masked):
        iks = [ik_ref[pl.ds(pl.multiple_of(kb * TK, TK), TK), :IDX_DIM] for kb in kbs]
        idxs = [jnp.zeros((TK, TQ), F32) for _ in kbs]
        for hi in range(IDX_HEADS):
            ss = [_dot(ik, qit_ref[:, hi * TQ:(hi + 1) * TQ]) for ik in iks]
            idxs = [idx + jnp.maximum(s, 0.0) * wt_ref[hi:hi + 1, :] for idx, s in zip(idxs, ss)]
        for kb, idx in zip(kbs, idxs):
            idx = jnp.where(idx == 0.0, 0.0, idx)
            bits = pltpu.bitcast(idx, I32)
            key = jnp.where(bits < 0, bits ^ jnp.int32(0x7FFFFFFF), bits)
            if masked:
                key = jnp.where(causal, key, jnp.int32(INT_MIN))
            keys_ref[kb] = key
            hi_ref[kb] = jnp.right_shift(key, 16).astype(I16)
            lo_ref[kb] = ((key & 0xFFFF) - 0x8000).astype(I16)

    def index_pair(i, carry):
        index_blocks([2 * i, 2 * i + 1], False)
        return carry

    lax.fori_loop(0, (nkb - 1) // 2, index_pair, 0)

    @pl.when((nkb - 1) % 2 == 1)
    def _():
        index_blocks([nkb - 2], False)

    index_blocks([nkb - 1], True)

    kf = float(top_k)
    rows16 = SUBLANES_BF16
    i16_min = -2 ** 15

    def count16(ref, cand, strict=False):
        c16 = jnp.broadcast_to(cand, (rows16, TQ)).astype(I16)

        def body(kb, acc):
            x = ref[kb]
            parts = []
            for i in range(TK // rows16):
                slab = x[i * rows16:(i + 1) * rows16]
                hit = slab > c16 if strict else slab >= c16
                parts.append(jnp.where(hit, jnp.int16(1), jnp.int16(0)))
            while len(parts) > 1:
                parts = [parts[i] + parts[i + 1] for i in range(0, len(parts), 2)]
            return acc + parts[0]

        acc = lax.fori_loop(0, nkb, body, jnp.zeros((rows16, TQ), I16))
        return jnp.sum(acc.astype(F32), axis=0, keepdims=True)

    def digit_select(ref, need, fallback):
        c0 = count16(ref, jnp.zeros((1, TQ), I32))
        ok0 = c0 >= need
        d0 = jnp.where(ok0, jnp.int32(0), jnp.int32(i16_min))
        n0 = jnp.where(ok0, c0, fallback)

        def bit_step(i, carry):
            d, n = carry
            cand = d | jnp.left_shift(jnp.int32(1), 14 - i)
            cnt = count16(ref, cand)
            ok = cnt >= need
            return jnp.where(ok, cand, d), jnp.where(ok, cnt, n)

        return lax.fori_loop(0, 15, bit_step, (d0, n0))

    total = (nkb * TK).astype(F32) * jnp.ones((1, TQ), F32)
    t_hi, n_hi = digit_select(hi_ref, kf, total)
    n_above = count16(hi_ref, t_hi, strict=True)
    t_hi16 = jnp.broadcast_to(t_hi, (rows16, TQ)).astype(I16)

    def restrict(kb, carry):
        for i in range(TK // rows16):
            sl = slice(i * rows16, (i + 1) * rows16)
            lo_ref[kb, sl, :] = jnp.where(hi_ref[kb, sl, :] == t_hi16, lo_ref[kb, sl, :], jnp.int16(i16_min))
        return carry

    lax.fori_loop(0, nkb, restrict, 0)
    t_lo, n_lo = digit_select(lo_ref, kf - n_above, n_hi - n_above)
    t = jnp.left_shift(t_hi, 16) + (t_lo + 0x8000)
    cge = n_above + n_lo
    t_ref[...] = t

    @pl.when(jnp.max(cge) > kf)
    def _():
        tt = t_ref[...]
        r = kf - (n_above + count16(lo_ref, t_lo, strict=True))
        off_threshold = 2 ** 15 - 1

        def mark(kb, carry):
            lo_ref[kb] = jnp.where(keys_ref[kb] == tt, key_i + kb * TK, off_threshold).astype(I16)
            return carry

        lax.fori_loop(0, nkb, mark, 0)

        def pos_step(i, pcut):
            cand = pcut | jnp.left_shift(jnp.int32(1), pos_bits - 1 - i)
            tied_before = total - count16(lo_ref, cand)
            return jnp.where(tied_before <= r, cand, pcut)

        pcut = lax.fori_loop(0, pos_bits, pos_step, jnp.zeros((1, TQ), I32))

        def demote(kb, carry):
            kk = keys_ref[kb]
            drop = jnp.logical_and(kk == tt, key_i + kb * TK >= pcut)
            keys_ref[kb] = jnp.where(drop, kk - 1, kk)
            return carry

        lax.fori_loop(0, nkb, demote, 0)

    def logits_of(hd):
        def logits(kb):
            r0 = pl.multiple_of(kb * TK, TK)
            sel = keys_ref[kb] >= t_ref[...]
            s = _dot(k_ref[pl.ds(r0, TK), :], qt_ref[:, hd * TQ:(hd + 1) * TQ])
            return jnp.where(sel, s + bias_ref[hd, _bias_tile_index(kb, qb)], NEG)
        return logits

    heads = range(DSA_HEADS)
    _attend_tiles([logits_of(hd) for hd in heads], [lambda kb: vt_ref[kb]] * DSA_HEADS, nkb, s_ref, m_ref, acc_ref)

    for hd in heads:
        o_ref[:, hd * LANES:(hd + 1) * LANES] = (acc_ref[hd, :LANES, :] / acc_ref[hd, LANES:LANES + 1, :]).T.astype(BF16)


def _dsa(sqn, skn, vt, proj, iw, bias, batch, seq):
    n = sqn.shape[0]
    nq = seq // TQ
    top_k = min(IDX_TOPK_MAX, seq // 4)
    pos_bits = int(seq).bit_length()
    iq_block = 4
    ik_block = (5 * 512 + 2 * LANES) // LANES
    return pl.pallas_call(
        functools.partial(_dsa_body, top_k=top_k, pos_bits=pos_bits),
        grid=(batch, nq),
        in_specs=[
            pl.BlockSpec((TQ, DSA_HEADS * LANES), lambda b, q: (b * nq + q, 0)),
            pl.BlockSpec((seq, LANES), lambda b, q: (b, 0)),
            pl.BlockSpec((None, None, seq // TK, VT_ROWS, TK), lambda b, q: (b, DIFF_HEADS, 0, 0, 0)),
            pl.BlockSpec((TQ, IDX_HEADS * IDX_DIM), lambda b, q: (b * nq + q, iq_block)),
            pl.BlockSpec((seq, LANES), lambda b, q: (b, ik_block)),
            pl.BlockSpec((TQ, LANES), lambda b, q: (b * nq + q, 0)),
            pl.BlockSpec((DSA_HEADS, 3, TK, TQ), lambda b, q: (1, 0, 0, 0)),
        ],
        out_specs=pl.BlockSpec((TQ, DSA_HEADS * LANES), lambda b, q: (b * nq + q, 0)),
        out_shape=jax.ShapeDtypeStruct((n, DSA_HEADS * LANES), BF16),
        scratch_shapes=[
            pltpu.VMEM((LANES, DSA_HEADS * TQ), BF16),
            pltpu.VMEM((IDX_DIM, IDX_HEADS * TQ), BF16),
            pltpu.VMEM((IDX_HEADS, TQ), F32),
            pltpu.VMEM((seq // TK, TK, TQ), I32),
            pltpu.VMEM((seq // TK, TK, TQ), I16),
            pltpu.VMEM((seq // TK, TK, TQ), I16),
            pltpu.VMEM((1, TQ), I32),
            pltpu.VMEM((DSA_HEADS, 1, TQ), F32),
            pltpu.VMEM((DSA_HEADS, VT_ROWS, TQ), F32),
            pltpu.VMEM((DSA_HEADS, 2, TK, TQ), F32),
        ],
        compiler_params=_cparams(2),
        name="dsa",
    )(sqn, skn, vt, proj, proj, iw, bias)


def _rel_bucket(dist):
    exact = REL_BUCKETS // 2
    n = jnp.maximum(dist, 0)
    nf = jnp.maximum(n, exact).astype(F32)
    far = exact + (jnp.log(nf / exact) / math.log(REL_MAX_DIST / exact) * (REL_BUCKETS - exact)).astype(I32)
    return jnp.where(n < exact, n, jnp.minimum(far, REL_BUCKETS - 1))


def _bias_tiles_body(tab_ref, bucket_ref, o_ref):
    hd = pl.program_id(0)
    nh = DIFF_HEADS + DSA_HEADS
    key = lax.broadcasted_iota(I32, (TK, TQ), 0)
    qry = lax.broadcasted_iota(I32, (TK, TQ), 1)
    for tile in range(3):
        bucket = bucket_ref[tile]
        acc = jnp.zeros(bucket.shape, F32)
        for b in range(REL_BUCKETS):
            acc = jnp.where(bucket == b, tab_ref[b * nh + hd], acc)
        if tile == 2:
            acc = jnp.where(key <= qry, acc, NEG)
        o_ref[tile] = acc * LOG2E


def _bias_tables(rel_bias):
    nh = DIFF_HEADS + DSA_HEADS
    assert TK + 1 >= REL_MAX_DIST
    j = jnp.arange(TK, dtype=I32)[:, None]
    i = jnp.arange(TQ, dtype=I32)[None, :]
    dist = jnp.stack([jnp.full((TK, TQ), 2 * TK, I32), i - j + TK, i - j])
    return pl.pallas_call(
        _bias_tiles_body,
        grid=(nh,),
        in_specs=[pl.BlockSpec(memory_space=pltpu.SMEM), pl.BlockSpec((3, TK, TQ), lambda h: (0, 0, 0))],
        out_specs=pl.BlockSpec((None, 3, TK, TQ), lambda h: (h, 0, 0, 0)),
        out_shape=jax.ShapeDtypeStruct((nh, 3, TK, TQ), F32),
        compiler_params=_cparams(1),
        name="bias_tiles",
    )(rel_bias.astype(F32).reshape(-1), _rel_bucket(dist))


def _block_diag_ones(width, group):
    r = np.arange(width)
    return jnp.asarray((r[:, None] // group) == (r[None, :] // group), dtype=BF16)


def _pad_lanes(w, width=LANES):
    return jnp.pad(w, ((0, 0), (0, width - w.shape[1])))


def kernel(x, rel_bias, mix_norm_g, ev_w_in, ev_w_out, gmlp_w_s, gmlp_b_s, gdn_conv_w, gdn_a_log, gdn_dt_bias,
           gdn_norm_g, od_w_in, od_w_out, diff_q_norm_g, diff_k_norm_g, diff_lambda, diff_sub_norm_g,
           dsa_q_norm_g, dsa_k_norm_g, ffn_norm_g, ffn_w_up, ffn_conv_w, ffn_conv_b, ffn_w_down):
    batch, seq, d = x.shape
    n = batch * seq
    depth = mix_norm_g.shape[0]
    assert d == D_MODEL and seq % max(TM_PROJ, TL_PREP, TL_GDN, TQ) == 0
    h = x.reshape(n, d)
    bias_tiles = _bias_tables(rel_bias)
    bd64 = _block_diag_ones(512, DIFF_QK)
    bd128 = _block_diag_ones(512, DSA_DIM)
    gw = GMLP_GROUPS * LANES
    qkv_w = 3 * GDN_HEADS * GDN_DIM

    for layer in range(depth):
        j = layer // 2
        g_mix = mix_norm_g[layer].reshape(1, d)
        if layer % 2 == 0:
            w = ev_w_in[j]
            o_u, o_v, o_qkv = 0, gw, 2 * gw
            o_b = o_qkv + qkv_w
            o_a = o_b + GDN_HEADS
            o_z = o_a + GDN_HEADS
            w_main = jnp.concatenate(
                [w[:, o_u:o_v], w[:, o_v:o_qkv], w[:, o_z:o_z + GDN_HEADS * GDN_DIM], w[:, o_qkv:o_b]],
                axis=1).astype(BF16)
            w_gate = _pad_lanes(w[:, o_b:o_z]).astype(BF16)
            proj, gates = _in_proj(h, g_mix, w_main, w_gate)
            gp = jnp.stack([
                _pad_lanes(jnp.concatenate([jnp.zeros((GDN_HEADS,), F32), gdn_a_log[j]])[None])[0],
                _pad_lanes(jnp.concatenate([jnp.zeros((GDN_HEADS,), F32), gdn_dt_bias[j]])[None])[0]])
            y_a, qkvn, gates2 = _even_prep(proj, gates, gmlp_w_s[j], gmlp_b_s[j].T, gdn_conv_w[j], gp, batch, seq)
            y_b = _gdn(qkvn, gates2, proj, gdn_norm_g[j].reshape(1, GDN_DIM), batch, seq)
            h = _out_proj(h, y_a, y_b, ev_w_out[j].astype(BF16))
        else:
            lambda_init = 0.8 - 0.6 * math.exp(-0.3 * layer)
            w = od_w_in[j]
            c = np.cumsum([0, 512, 512, 512, 512, 128, 128, 512, 64, 8])
            dq, dk, dv, sq, sk, sv, iq, ik, iw = [w[:, c[i]:c[i + 1]] for i in range(9)]
            w_main = jnp.concatenate([dq, dk, dv, sq, iq, sk, sv, ik, ik], axis=1).astype(BF16)
            w_gate = _pad_lanes(iw).astype(BF16)
            proj, iw_out = _in_proj(h, g_mix, w_main, w_gate)
            dqn, dkn, sqn, skn, vt = _odd_prep(
                proj,
                jnp.tile(diff_q_norm_g[j], 2 * DIFF_HEADS)[None], jnp.tile(diff_k_norm_g[j], 2 * DIFF_HEADS)[None],
                jnp.tile(dsa_q_norm_g[j], DSA_HEADS)[None], dsa_k_norm_g[j][None], bd64, bd128, batch, seq)
            y_c = _diff_attn(dqn, dkn, vt, bias_tiles, diff_lambda[j], diff_sub_norm_g[j][None],
                             batch, seq, lambda_init)
            y_d = _dsa(sqn, skn, vt, proj, iw_out, bias_tiles, batch, seq)
            h = _out_proj(h, y_c, y_d, od_w_out[j].astype(BF16))
        h = _ffn(h, ffn_norm_g[layer].reshape(1, d), ffn_w_up[layer].astype(BF16), ffn_conv_w[layer],
                 ffn_conv_b[layer].reshape(1, 2 * D_FF), ffn_w_down[layer].astype(BF16), seq)
    return h.reshape(batch, seq, d)
```

```python
import functools
import math

import numpy as np
import jax
import jax.numpy as jnp
from jax import lax
from jax.experimental import pallas as pl
from jax.experimental.pallas import tpu as pltpu

F32 = jnp.float32
BF16 = jnp.bfloat16
I32 = jnp.int32
I16 = jnp.int16
HIGHEST = lax.Precision.HIGHEST

D_MODEL = 1024
GMLP_GROUPS = 4
GMLP_CHUNK = 128
GDN_HEADS = 4
GDN_DIM = 128
GDN_CHUNK = 64
GDN_CONV = 4
DIFF_HEADS = 4
DIFF_QK = 64
DSA_HEADS = 4
DSA_DIM = 128
IDX_HEADS = 8
IDX_DIM = 64
IDX_TOPK_MAX = 256
REL_BUCKETS = 32
REL_MAX_DIST = 128
D_FF = 2816
FFN_CONV = 3
EPS = 1e-6

LANES = 128
SUBLANES_F32 = 8
SUBLANES_BF16 = 16
VMEM_LIMIT = 56 * 1024 * 1024
VT_ROWS = LANES + SUBLANES_BF16

TM_PROJ = 512
TM_FFN = 512
TL_PREP = 512
TL_GDN = 512
GDN_SEQS_PER_STEP = 2
GDN_PREP_PROBLEMS = 16
TQ = 256
TK = 256
FF_CHUNK = 256
NEG = -1e30
LOG2E = 1.0 / math.log(2.0)
INT_MIN = -2 ** 31


def _cparams(n_axes):
    return pltpu.CompilerParams(dimension_semantics=("arbitrary",) * n_axes, vmem_limit_bytes=VMEM_LIMIT)


def _dot(a, b, **kw):
    return jnp.dot(a, b, preferred_element_type=F32, **kw)


def _dot_nt(a, b, **kw):
    return lax.dot_general(a, b, (((1,), (1,)), ((), ())), preferred_element_type=F32, **kw)


def _dot_tn(a, b, **kw):
    return lax.dot_general(a, b, (((0,), (0,)), ((), ())), preferred_element_type=F32, **kw)


def _rms(x, g):
    return x * lax.rsqrt(jnp.mean(x * x, axis=-1, keepdims=True) + EPS) * g


def _sigmoid(x):
    return 1.0 / (1.0 + jnp.exp2(x * -LOG2E))


def _silu(x):
    return x / (1.0 + jnp.exp2(x * -LOG2E))


def _gelu_tanh(x):
    k0 = -2.0 * math.sqrt(2.0 / math.pi) * LOG2E
    return x / (1.0 + jnp.exp2(x * (k0 + (k0 * 0.044715) * (x * x))))


def _softplus(x):
    return jnp.maximum(x, 0.0) + jnp.log(1.0 + jnp.exp(-jnp.abs(x)))


def _in_proj_body(h_ref, g_ref, w_ref, wg_ref, o_ref, og_ref, *, nout):
    xn = _rms(h_ref[...], g_ref[...]).astype(BF16)
    for c in range(0, nout, 512):
        e = min(c + 512, nout)
        o_ref[:, c:e] = _dot(xn, w_ref[:, c:e]).astype(BF16)
    og_ref[...] = _dot(xn, wg_ref[...])


def _in_proj(h, g, w, wg):
    n, d = h.shape
    nout = w.shape[1]
    return pl.pallas_call(
        functools.partial(_in_proj_body, nout=nout),
        grid=(n // TM_PROJ,),
        in_specs=[
            pl.BlockSpec((TM_PROJ, d), lambda i: (i, 0)),
            pl.BlockSpec((1, d), lambda i: (0, 0)),
            pl.BlockSpec((d, nout), lambda i: (0, 0)),
            pl.BlockSpec((d, LANES), lambda i: (0, 0)),
        ],
        out_specs=[
            pl.BlockSpec((TM_PROJ, nout), lambda i: (i, 0)),
            pl.BlockSpec((TM_PROJ, LANES), lambda i: (i, 0)),
        ],
        out_shape=[jax.ShapeDtypeStruct((n, nout), BF16), jax.ShapeDtypeStruct((n, LANES), F32)],
        compiler_params=_cparams(1),
        name="in_proj",
    )(h, g, w, wg)


def _ffn_body(h_ref, halo_ref, y1_ref, y1h_ref, y2_ref, y2h_ref, wo_ref, g_ref, wup_ref, cw_ref, cb_ref, wdn_ref,
              o_ref, xn_ref, act_ref, *, tiles_per_seq):
    tm = h_ref.shape[0]
    i = pl.program_id(0)
    half = y1_ref.shape[1]

    def mixer_residual(hh, y1, y2):
        return hh + _dot(y1, wo_ref[:half, :]) + _dot(y2, wo_ref[half:, :])

    h = mixer_residual(h_ref[...], y1_ref[...], y2_ref[...])
    g = g_ref[...]
    hb = SUBLANES_BF16
    xn_ref[hb:, :] = _rms(h, g).astype(BF16)
    keep = jnp.where(i % tiles_per_seq == 0, 0.0, 1.0)
    halo = mixer_residual(halo_ref[...], y1h_ref[...], y2h_ref[...])
    xn_ref[:hb, :] = (_rms(halo, g) * keep).astype(BF16)
    xn = xn_ref[...]

    def up_pair(c):
        return [_dot(xn, wup_ref[:, base:base + FF_CHUNK]) for base in (c, D_FF + c)]

    def conv(up, base):
        cw = cw_ref[:, base:base + FF_CHUNK]
        return (cw[0:1] * up[hb - 2:tm + hb - 2] + cw[1:2] * up[hb - 1:tm + hb - 1] + cw[2:3] * up[hb:tm + hb]
                + cb_ref[:, base:base + FF_CHUNK])

    chunks = list(range(0, D_FF, FF_CHUNK))
    ups = up_pair(chunks[0])
    for n, c in enumerate(chunks):
        nxt = up_pair(chunks[n + 1]) if n + 1 < len(chunks) else None
        act_ref[:, c:c + FF_CHUNK] = (_silu(conv(ups[0], c)) * conv(ups[1], D_FF + c)).astype(BF16)
        ups = nxt
    o_ref[...] = h + _dot(act_ref[...], wdn_ref[...])


def _ffn(h, y1, y2, wo, g, wup, cw, cb, wdn, seq):
    n, d = h.shape
    half = y1.shape[1]
    tiles_per_seq = seq // TM_FFN
    halo_blocks = TM_FFN // SUBLANES_BF16
    const = dict(pipeline_mode=pl.Buffered(1))
    tile = lambda i: (i, 0)
    halo = lambda i: (jnp.maximum(i * halo_blocks - 1, 0), 0)
    return pl.pallas_call(
        functools.partial(_ffn_body, tiles_per_seq=tiles_per_seq),
        grid=(n // TM_FFN,),
        in_specs=[
            pl.BlockSpec((TM_FFN, d), tile),
            pl.BlockSpec((SUBLANES_BF16, d), halo),
            pl.BlockSpec((TM_FFN, half), tile),
            pl.BlockSpec((SUBLANES_BF16, half), halo),
            pl.BlockSpec((TM_FFN, half), tile),
            pl.BlockSpec((SUBLANES_BF16, half), halo),
            pl.BlockSpec((2 * half, d), lambda i: (0, 0), **const),
            pl.BlockSpec((1, d), lambda i: (0, 0)),
            pl.BlockSpec((d, 2 * D_FF), lambda i: (0, 0), **const),
            pl.BlockSpec((FFN_CONV, 2 * D_FF), lambda i: (0, 0)),
            pl.BlockSpec((1, 2 * D_FF), lambda i: (0, 0)),
            pl.BlockSpec((D_FF, d), lambda i: (0, 0), **const),
        ],
        out_specs=pl.BlockSpec((TM_FFN, d), lambda i: (i, 0)),
        out_shape=jax.ShapeDtypeStruct((n, d), F32),
        scratch_shapes=[pltpu.VMEM((TM_FFN + SUBLANES_BF16, d), BF16), pltpu.VMEM((TM_FFN, D_FF), BF16)],
        compiler_params=_cparams(1),
        name="ffn",
    )(h, h, y1, y1, y2, y2, wo, g, wup, cw, cb, wdn)


def _even_prep_body(uv_ref, qkv_ref, halo_ref, gate_ref, ws_ref, bs_ref, cw_ref, gp_ref,
                    ya_ref, qkvo_ref, gout_ref, xs_ref):
    tl = uv_ref.shape[0]
    t = pl.program_id(1)
    gw = LANES
    row = lax.broadcasted_iota(I32, (GMLP_CHUNK, GMLP_CHUNK), 0)
    col = lax.broadcasted_iota(I32, (GMLP_CHUNK, GMLP_CHUNK), 1)
    for gi in range(GMLP_GROUPS):
        u = _gelu_tanh(uv_ref[:, gi * gw:(gi + 1) * gw].astype(F32))
        v = _gelu_tanh(uv_ref[:, (GMLP_GROUPS + gi) * gw:(GMLP_GROUPS + gi + 1) * gw].astype(F32))
        vc = v - jnp.mean(v, axis=-1, keepdims=True)
        vn = (vc * lax.rsqrt(jnp.mean(vc * vc, axis=-1, keepdims=True) + EPS)).astype(BF16)
        w = jnp.where(col <= row, ws_ref[gi], 0.0).astype(BF16)
        b = bs_ref[:, gi:gi + 1]
        for c in range(tl // GMLP_CHUNK):
            r0 = c * GMLP_CHUNK
            mixed = _dot(w, vn[r0:r0 + GMLP_CHUNK]) + b
            ya_ref[r0:r0 + GMLP_CHUNK, gi * gw:(gi + 1) * gw] = (u[r0:r0 + GMLP_CHUNK] * mixed).astype(BF16)
    keep = jnp.where(t == 0, 0.0, 1.0)
    hb = SUBLANES_BF16
    for j in range(3 * GDN_HEADS):
        sl = slice(j * gw, (j + 1) * gw)
        xs_ref[:hb, :] = halo_ref[:, sl].astype(F32) * keep
        xs_ref[hb:, :] = qkv_ref[:, sl].astype(F32)
        cw = cw_ref[:, sl]
        y = cw[0:1] * xs_ref[hb - 3:tl + hb - 3, :]
        for k in range(1, GDN_CONV):
            y = y + cw[k:k + 1] * xs_ref[hb - 3 + k:tl + hb - 3 + k, :]
        y = _silu(y)
        if j < 2 * GDN_HEADS:
            y = y * lax.rsqrt(jnp.sum(y * y, axis=-1, keepdims=True) + EPS)
        if j < GDN_HEADS:
            y = y * (GDN_DIM ** -0.5)
        qkvo_ref[:, sl] = y.astype(BF16)
    x = gate_ref[...]
    lane = lax.broadcasted_iota(I32, x.shape, 1)
    beta = _sigmoid(x)
    gdec = -jnp.exp(gp_ref[0:1, :]) * _softplus(x + gp_ref[1:2, :])
    gout_ref[...] = jnp.where(lane < GDN_HEADS, beta, gdec)


def _even_prep(proj, gates, ws, bs_t, cw, gp, batch, seq):
    n = proj.shape[0]
    nt = seq // TL_PREP
    halo_blocks = TL_PREP // SUBLANES_BF16
    qkv_w = 3 * GDN_HEADS * GDN_DIM
    row = lambda b, t: b * nt + t
    return pl.pallas_call(
        _even_prep_body,
        grid=(batch, nt),
        in_specs=[
            pl.BlockSpec((TL_PREP, 2 * GMLP_GROUPS * LANES), lambda b, t: (row(b, t), 0)),
            pl.BlockSpec((TL_PREP, qkv_w), lambda b, t: (row(b, t), 1)),
            pl.BlockSpec((SUBLANES_BF16, qkv_w), lambda b, t: (jnp.maximum(row(b, t) * halo_blocks - 1, 0), 1)),
            pl.BlockSpec((TL_PREP, LANES), lambda b, t: (row(b, t), 0)),
            pl.BlockSpec((GMLP_GROUPS, GMLP_CHUNK, GMLP_CHUNK), lambda b, t: (0, 0, 0)),
            pl.BlockSpec((GMLP_CHUNK, GMLP_GROUPS), lambda b, t: (0, 0)),
            pl.BlockSpec((GDN_CONV, qkv_w), lambda b, t: (0, 0)),
            pl.BlockSpec((2, LANES), lambda b, t: (0, 0)),
        ],
        out_specs=[
            pl.BlockSpec((TL_PREP, GMLP_GROUPS * LANES), lambda b, t: (row(b, t), 0)),
            pl.BlockSpec((TL_PREP, qkv_w), lambda b, t: (row(b, t), 0)),
            pl.BlockSpec((TL_PREP, LANES), lambda b, t: (row(b, t), 0)),
        ],
        out_shape=[
            jax.ShapeDtypeStruct((n, GMLP_GROUPS * LANES), BF16),
            jax.ShapeDtypeStruct((n, qkv_w), BF16),
            jax.ShapeDtypeStruct((n, LANES), F32),
        ],
        scratch_shapes=[pltpu.VMEM((TL_PREP + SUBLANES_BF16, LANES), F32)],
        compiler_params=_cparams(2),
        name="even_prep",
    )(proj, proj, proj, gates, ws, bs_t, cw, gp)


def _split_bf16(x):
    hi = x.astype(BF16)
    return hi, (x - hi.astype(F32)).astype(BF16)


def _mm3(a, b):
    ah, al = _split_bf16(a)
    bh, bl = _split_bf16(b)
    return _dot(ah, bh) + (_dot(ah, bl) + _dot(al, bh))


def _unit_lower_inverses(lows):
    c = lows[0].shape[0]
    row = lax.broadcasted_iota(I32, (c, c), 0)
    col = lax.broadcasted_iota(I32, (c, c), 1)
    eye = jnp.where(row == col, 1.0, 0.0)
    same16 = jnp.right_shift(row, 4) == jnp.right_shift(col, 4)
    same32 = jnp.right_shift(row, 5) == jnp.right_shift(col, 5)
    mm1 = lambda a, b: _dot(a.astype(BF16), b.astype(BF16))
    ps = [jnp.where(same16, -low, 0.0) for low in lows]
    xs = [eye + p for p in ps]
    for mm in (_mm3, mm1, mm1):
        ps = [mm(p, p) for p in ps]
        xs = [x + mm(x, p) for x, p in zip(xs, ps)]
    mid = jnp.logical_and(same32, jnp.logical_not(same16))
    ys = [mm1(jnp.where(mid, low, 0.0), x) for low, x in zip(lows, xs)]
    xs = [x - mm1(x, y) for x, y in zip(xs, ys)]
    ys = [mm1(jnp.where(same32, 0.0, low), x) for low, x in zip(lows, xs)]
    return [x - mm1(x, y) for x, y in zip(xs, ys)]


def _gdn_body(qkv_ref, gate_ref, z_ref, gn_ref, o_ref,
              s_ref, u_ref, w_ref, qg_ref, kdt_ref, intra_ref, egl_ref):
    nseq, tl = qkv_ref.shape[0], qkv_ref.shape[1]
    c = GDN_CHUNK
    d = GDN_DIM
    lanes = [(bi, hd) for bi in range(nseq) for hd in range(GDN_HEADS)]
    flat = lambda bi, hd: bi * GDN_HEADS + hd
    prep_chunks = max(1, GDN_PREP_PROBLEMS // len(lanes))
    t = pl.program_id(1)

    @pl.when(t == 0)
    def _():
        s_ref[...] = jnp.zeros_like(s_ref)

    row = lax.broadcasted_iota(I32, (c, c), 0)
    col = lax.broadcasted_iota(I32, (c, c), 1)
    incl = col <= row
    strict = col < row
    lmat = jnp.where(incl, 1.0, 0.0)
    gn = gn_ref[...]

    def prepare(it, carry):
        cis = [it * prep_chunks + j for j in range(prep_chunks)]
        rows = [pl.ds(pl.multiple_of(ci * c, c), c) for ci in cis]
        gates = {(j, bi): gate_ref[bi, rows[j], :] for j in range(prep_chunks) for bi in range(nseq)}
        gcum = {key: _dot(lmat, g, precision=HIGHEST) for key, g in gates.items()}
        pairs = [(j, bi, hd) for j in range(prep_chunks) for bi, hd in lanes]
        qs = [qkv_ref[bi, rows[j], hd * d:(hd + 1) * d] for j, bi, hd in pairs]
        ks = [qkv_ref[bi, rows[j], (GDN_HEADS + hd) * d:(GDN_HEADS + hd + 1) * d] for j, bi, hd in pairs]
        vs = [qkv_ref[bi, rows[j], (2 * GDN_HEADS + hd) * d:(2 * GDN_HEADS + hd + 1) * d] for j, bi, hd in pairs]
        beta = [jnp.broadcast_to(gates[j, bi][:, hd:hd + 1], (c, d)) for j, bi, hd in pairs]
        gc = [jnp.broadcast_to(gcum[j, bi][:, GDN_HEADS + hd:GDN_HEADS + hd + 1], (c, d)) for j, bi, hd in pairs]
        decay = [jnp.where(incl, jnp.exp(jnp.where(incl, g[:, :c] - g.T[:c, :], 0.0)), 0.0) for g in gc]
        kb = [k.astype(F32) * b for k, b in zip(ks, beta)]
        lows = [jnp.where(strict, _dot_nt(x.astype(BF16), k) * dc, 0.0) for x, k, dc in zip(kb, ks, decay)]
        tinv = [x.astype(BF16) for x in _unit_lower_inverses(lows)]
        egc = [jnp.exp(g) for g in gc]
        rhs = [jnp.concatenate([(v.astype(F32) * b).astype(BF16), (x * e).astype(BF16)], axis=1)
               for v, b, x, e in zip(vs, beta, kb, egc)]
        uw = [_dot(ti, r) for ti, r in zip(tinv, rhs)]
        qk = [_dot_nt(q, k) for q, k in zip(qs, ks)]
        for n, (j, bi, hd) in enumerate(pairs):
            ci, f = cis[j], flat(bi, hd)
            u_ref[ci, f] = uw[n][:, :d]
            w_ref[ci, f] = uw[n][:, d:].astype(BF16)
            intra_ref[ci, f] = jnp.where(incl, qk[n] * decay[n], 0.0).astype(BF16)
            qg_ref[ci, f] = (qs[n].astype(F32) * egc[n]).astype(BF16)
            g_last = gc[n][c - 1:c, :]
            kdt_ref[ci, f] = (ks[n].astype(F32) * jnp.exp(g_last - gc[n])).T.astype(BF16)
            egl_ref[ci, f] = jnp.exp(g_last)
        return carry

    lax.fori_loop(0, tl // c // prep_chunks, prepare, 0)

    def scan(ci, carry):
        r0 = pl.multiple_of(ci * c, c)
        fs = [flat(bi, hd) for bi, hd in lanes]
        states = [s_ref[f] for f in fs]
        sb = [s.astype(BF16) for s in states]
        ws = [_dot(w_ref[ci, f], s) for f, s in zip(fs, sb)]
        qsd = [_dot(qg_ref[ci, f], s) for f, s in zip(fs, sb)]
        vnb = [(u_ref[ci, f] - w).astype(BF16) for f, w in zip(fs, ws)]
        for f, s, v in zip(fs, states, vnb):
            s_ref[f] = s * egl_ref[ci, f] + _dot(kdt_ref[ci, f], v)
        for (bi, hd), f, qd, v in zip(lanes, fs, qsd, vnb):
            out = qd + _dot(intra_ref[ci, f], v)
            z = z_ref[bi, pl.ds(r0, c), hd * d:(hd + 1) * d].astype(F32)
            o_ref[bi, pl.ds(r0, c), hd * d:(hd + 1) * d] = (_rms(out, gn) * _silu(z)).astype(BF16)
        return carry

    lax.fori_loop(0, tl // c, scan, 0)


def _gdn(qkv, gates, proj, gn, batch, seq):
    n = qkv.shape[0]
    nt = seq // TL_GDN
    nc = TL_GDN // GDN_CHUNK
    width = GDN_HEADS * GDN_DIM
    nseq = GDN_SEQS_PER_STEP if batch % GDN_SEQS_PER_STEP == 0 else 1
    nl = nseq * GDN_HEADS
    as3d = lambda a: a.reshape(batch, seq, a.shape[-1])
    out = pl.pallas_call(
        _gdn_body,
        grid=(batch // nseq, nt),
        in_specs=[
            pl.BlockSpec((nseq, TL_GDN, 3 * width), lambda b, t: (b, t, 0)),
            pl.BlockSpec((nseq, TL_GDN, LANES), lambda b, t: (b, t, 0)),
            pl.BlockSpec((nseq, TL_GDN, width), lambda b, t: (b, t, 2)),
            pl.BlockSpec((1, GDN_DIM), lambda b, t: (0, 0)),
        ],
        out_specs=pl.BlockSpec((nseq, TL_GDN, width), lambda b, t: (b, t, 0)),
        out_shape=jax.ShapeDtypeStruct((batch, seq, width), BF16),
        scratch_shapes=[
            pltpu.VMEM((nl, GDN_DIM, GDN_DIM), F32),
            pltpu.VMEM((nc, nl, GDN_CHUNK, GDN_DIM), F32),
            pltpu.VMEM((nc, nl, GDN_CHUNK, GDN_DIM), BF16),
            pltpu.VMEM((nc, nl, GDN_CHUNK, GDN_DIM), BF16),
            pltpu.VMEM((nc, nl, GDN_DIM, GDN_CHUNK), BF16),
            pltpu.VMEM((nc, nl, GDN_CHUNK, GDN_CHUNK), BF16),
            pltpu.VMEM((nc, nl, 1, GDN_DIM), F32),
        ],
        compiler_params=_cparams(2),
        name="gdn",
    )(as3d(qkv), as3d(gates), as3d(proj), gn)
    return out.reshape(n, width)


def _group_mean_sq(x, ones_bd, group):
    x2 = x * x
    hi = x2.astype(BF16)
    lo = (x2 - hi.astype(F32)).astype(BF16)
    return (_dot(hi, ones_bd) + _dot(lo, ones_bd)) * (1.0 / group)


def _odd_prep_body(dq_ref, dk_ref, dv_ref, sq_ref, sk_ref, sv_ref, gq_ref, gk_ref, gsq_ref, gsk_ref, bd64_ref,
                   bd128_ref, dqo_ref, dko_ref, sqo_ref, sko_ref, vt_ref):
    ones = jnp.ones((VT_ROWS - LANES, TK), BF16)
    for j in range(DIFF_HEADS + 1):
        src = sv_ref[...] if j == DIFF_HEADS else dv_ref[:, j * LANES:(j + 1) * LANES]
        xt = src.astype(F32).T
        for c in range(xt.shape[1] // TK):
            vt_ref[j, c, :LANES, :] = xt[:, c * TK:(c + 1) * TK].astype(BF16)
            vt_ref[j, c, LANES:, :] = ones
    bd64 = bd64_ref[...]
    x = dq_ref[...].astype(F32)
    dqo_ref[...] = (x * lax.rsqrt(_group_mean_sq(x, bd64, DIFF_QK) + EPS) * gq_ref[...]
                    * (DIFF_QK ** -0.5 * LOG2E)).astype(BF16)
    x = dk_ref[...].astype(F32)
    dko_ref[...] = (x * lax.rsqrt(_group_mean_sq(x, bd64, DIFF_QK) + EPS) * gk_ref[...]).astype(BF16)
    x = sq_ref[...].astype(F32)
    sqo_ref[...] = (x * lax.rsqrt(_group_mean_sq(x, bd128_ref[...], DSA_DIM) + EPS) * gsq_ref[...]
                    * (DSA_DIM ** -0.5 * LOG2E)).astype(BF16)
    x = sk_ref[...].astype(F32)
    sko_ref[...] = _rms(x, gsk_ref[...]).astype(BF16)


def _odd_prep(proj, gq, gk, gsq, gsk, bd64, bd128, batch, seq):
    n = proj.shape[0]
    w = 512
    nt = seq // TL_PREP
    kt = TL_PREP // TK
    full = lambda shape: pl.BlockSpec(shape, lambda b, t: (0,) * len(shape))
    row = lambda b, t: b * nt + t
    return pl.pallas_call(
        _odd_prep_body,
        grid=(batch, nt),
        in_specs=[
            pl.BlockSpec((TL_PREP, w), lambda b, t: (row(b, t), 0)),
            pl.BlockSpec((TL_PREP, w), lambda b, t: (row(b, t), 1)),
            pl.BlockSpec((TL_PREP, w), lambda b, t: (row(b, t), 2)),
            pl.BlockSpec((TL_PREP, w), lambda b, t: (row(b, t), 3)),
            pl.BlockSpec((TL_PREP, LANES), lambda b, t: (row(b, t), 5 * w // LANES)),
            pl.BlockSpec((TL_PREP, LANES), lambda b, t: (row(b, t), 5 * w // LANES + 1)),
            full((1, w)), full((1, w)), full((1, w)), full((1, LANES)), full((w, w)), full((w, w)),
        ],
        out_specs=[
            pl.BlockSpec((TL_PREP, w), lambda b, t: (row(b, t), 0)),
            pl.BlockSpec((TL_PREP, w), lambda b, t: (row(b, t), 0)),
            pl.BlockSpec((TL_PREP, w), lambda b, t: (row(b, t), 0)),
            pl.BlockSpec((TL_PREP, LANES), lambda b, t: (row(b, t), 0)),
            pl.BlockSpec((None, DIFF_HEADS + 1, kt, VT_ROWS, TK), lambda b, t: (b, 0, t, 0, 0)),
        ],
        out_shape=[
            jax.ShapeDtypeStruct((n, w), BF16), jax.ShapeDtypeStruct((n, w), BF16),
            jax.ShapeDtypeStruct((n, w), BF16), jax.ShapeDtypeStruct((n, LANES), BF16),
            jax.ShapeDtypeStruct((batch, DIFF_HEADS + 1, seq // TK, VT_ROWS, TK), BF16),
        ],
        compiler_params=_cparams(2),
        name="odd_prep",
    )(proj, proj, proj, proj, proj, proj, gq, gk, gsq, gsk, bd64, bd128)


def _bias_tile_index(kb, qb):
    return jnp.clip(kb - (qb - 2), 0, 2)


def _softmax_step_t(s_t, vt, m_ref, acc_ref):
    for g in range(s_t.shape[1] // LANES):
        sl = slice(g * LANES, (g + 1) * LANES)
        s = s_t[:, sl]
        m_prev = m_ref[:, sl]
        m_new = jnp.maximum(m_prev, jnp.max(s, axis=0, keepdims=True))
        alpha = jnp.exp2(m_prev - m_new)
        p = jnp.exp2(s - m_new)
        acc_ref[:, sl] = alpha * acc_ref[:, sl] + _dot(vt, p.astype(BF16))
        m_ref[:, sl] = m_new


def _attend_tiles(logits, values, nblk, s_ref, m_ref, acc_ref):
    streams = range(len(logits))
    m_ref[...] = jnp.full_like(m_ref, NEG)
    acc_ref[...] = jnp.zeros_like(acc_ref)
    for i in streams:
        s_ref[i, 0] = logits[i](0)

    def pair(it, carry):
        kb = 2 * it
        for i in streams:
            s_ref[i, 1] = logits[i](kb + 1)
        for i in streams:
            _softmax_step_t(s_ref[i, 0], values[i](kb), m_ref.at[i], acc_ref.at[i])
        for i in streams:
            s_ref[i, 0] = logits[i](jnp.minimum(kb + 2, nblk - 1))
        for i in streams:
            _softmax_step_t(s_ref[i, 1], values[i](kb + 1), m_ref.at[i], acc_ref.at[i])
        return carry

    lax.fori_loop(0, nblk // 2, pair, 0)

    @pl.when(nblk % 2 == 1)
    def _():
        for i in streams:
            _softmax_step_t(s_ref[i, 0], values[i](nblk - 1), m_ref.at[i], acc_ref.at[i])


def _diff_attn_body(q_ref, k_ref, vt_ref, bias_ref, lam_ref, sg_ref, o_ref,
                    qt_ref, m_ref, acc_ref, s_ref, *, lambda_init):
    qi = pl.program_id(1)
    nblk = qi + 1
    heads = range(DIFF_HEADS)
    for hd in heads:
        qt = q_ref[:, hd * LANES:(hd + 1) * LANES].astype(F32).T
        dim = lax.broadcasted_iota(I32, qt.shape, 0)
        qt_ref[hd, :, :TQ] = jnp.where(dim < DIFF_QK, qt, 0.0).astype(BF16)
        qt_ref[hd, :, TQ:] = jnp.where(dim >= DIFF_QK, qt, 0.0).astype(BF16)

    def logits_of(hd):
        def logits(kb):
            r0 = pl.multiple_of(kb * TK, TK)
            b = bias_ref[hd, _bias_tile_index(kb, qi)]
            s = _dot(k_ref[pl.ds(r0, TK), hd * LANES:(hd + 1) * LANES], qt_ref[hd])
            return s + jnp.concatenate([b, b], axis=1)
        return logits

    _attend_tiles([logits_of(hd) for hd in heads], [lambda kb, hd=hd: vt_ref[hd, kb] for hd in heads],
                  nblk, s_ref, m_ref, acc_ref)

    lf = lam_ref[...]
    lam = (jnp.exp(jnp.sum(lf[0:1] * lf[1:2], axis=-1, keepdims=True))
           - jnp.exp(jnp.sum(lf[2:3] * lf[3:4], axis=-1, keepdims=True)) + lambda_init)
    for hd in heads:
        den = acc_ref[hd, LANES:LANES + 1, :]
        ot = (acc_ref[hd, :LANES, :TQ] / den[:, :TQ] - lam * (acc_ref[hd, :LANES, TQ:] / den[:, TQ:]))
        ot = ot * lax.rsqrt(jnp.mean(ot * ot, axis=0, keepdims=True) + EPS)
        o_ref[:, hd * LANES:(hd + 1) * LANES] = (ot.T * sg_ref[...] * (1.0 - lambda_init)).astype(BF16)


def _diff_attn(dqn, dkn, vt, bias, lam_p, sub_g, batch, seq, lambda_init):
    n = dqn.shape[0]
    nq = seq // TQ
    width = DIFF_HEADS * LANES
    return pl.pallas_call(
        functools.partial(_diff_attn_body, lambda_init=lambda_init),
        grid=(batch, nq),
        in_specs=[
            pl.BlockSpec((TQ, width), lambda b, q: (b * nq + q, 0)),
            pl.BlockSpec((seq, width), lambda b, q: (b, 0)),
            pl.BlockSpec((None, DIFF_HEADS + 1, seq // TK, VT_ROWS, TK), lambda b, q: (b, 0, 0, 0, 0)),
            pl.BlockSpec((DIFF_HEADS, 3, TK, TQ), lambda b, q: (0, 0, 0, 0)),
            pl.BlockSpec((4, DIFF_QK), lambda b, q: (0, 0)),
            pl.BlockSpec((1, LANES), lambda b, q: (0, 0)),
        ],
        out_specs=pl.BlockSpec((TQ, width), lambda b, q: (b * nq + q, 0)),
        out_shape=jax.ShapeDtypeStruct((n, width), BF16),
        scratch_shapes=[
            pltpu.VMEM((DIFF_HEADS, LANES, 2 * TQ), BF16),
            pltpu.VMEM((DIFF_HEADS, 1, 2 * TQ), F32),
            pltpu.VMEM((DIFF_HEADS, VT_ROWS, 2 * TQ), F32),
            pltpu.VMEM((DIFF_HEADS, 2, TK, 2 * TQ), F32),
        ],
        compiler_params=_cparams(2),
        name="diff_attn",
    )(dqn, dkn, vt, bias, lam_p, sub_g)


def _dsa_body(q_ref, k_ref, vt_ref, iq_ref, ik_ref, iw_ref, bias_ref, o_ref,
              qt_ref, qit_ref, wt_ref, keys_ref, hi_ref, lo_ref, t_ref, m_ref, acc_ref, s_ref, *, top_k, pos_bits):
    qb = pl.program_id(1)
    nkb = qb + 1

    for p in range(IDX_HEADS // 2):
        gt = iq_ref[:, p * LANES:(p + 1) * LANES].astype(F32).T.astype(BF16)
        qit_ref[:, (2 * p) * TQ:(2 * p + 1) * TQ] = gt[:IDX_DIM]
        qit_ref[:, (2 * p + 1) * TQ:(2 * p + 2) * TQ] = gt[IDX_DIM:]
    for hd in range(DSA_HEADS):
        qt_ref[:, hd * TQ:(hd + 1) * TQ] = q_ref[:, hd * LANES:(hd + 1) * LANES].astype(F32).T.astype(BF16)
    wscale = (IDX_HEADS ** -0.5) * (IDX_DIM ** -0.5)
    wt_ref[...] = (iw_ref[...] * wscale).T[:IDX_HEADS, :]

    key_i = lax.broadcasted_iota(I32, (TK, TQ), 0)
    qry_i = lax.broadcasted_iota(I32, (TK, TQ), 1)
    causal = key_i <= qry_i

    def index_blocks(kbs, masked):
        iks = [ik_ref[pl.ds(pl.multiple_of(kb * TK, TK), TK), :IDX_DIM] for kb in kbs]
        idxs = [jnp.zeros((TK, TQ), F32) for _ in kbs]
        for hi in range(IDX_HEADS):
            ss = [_dot(ik, qit_ref[:, hi * TQ:(hi + 1) * TQ]) for ik in iks]
            idxs = [idx + jnp.maximum(s, 0.0) * wt_ref[hi:hi + 1, :] for idx, s in zip(idxs, ss)]
        for kb, idx in zip(kbs, idxs):
            idx = jnp.where(idx == 0.0, 0.0, idx)
            bits = pltpu.bitcast(idx, I32)
            key = jnp.where(bits < 0, bits ^ jnp.int32(0x7FFFFFFF), bits)
            if masked:
                key = jnp.where(causal, key, jnp.int32(INT_MIN))
            keys_ref[kb] = key
            hi_ref[kb] = jnp.right_shift(key, 16).astype(I16)
            lo_ref[kb] = ((key & 0xFFFF) - 0x8000).astype(I16)

    def index_pair(i, carry):
        index_blocks([2 * i, 2 * i + 1], False)
        return carry

    lax.fori_loop(0, (nkb - 1) // 2, index_pair, 0)

    @pl.when((nkb - 1) % 2 == 1)
    def _():
        index_blocks([nkb - 2], False)

    index_blocks([nkb - 1], True)

    kf = float(top_k)
    rows16 = SUBLANES_BF16
    i16_min = -2 ** 15

    def count16(ref, cand, strict=False):
        c16 = jnp.broadcast_to(cand, (rows16, TQ)).astype(I16)

        def body(kb, acc):
            x = ref[kb]
            parts = []
            for i in range(TK // rows16):
                slab = x[i * rows16:(i + 1) * rows16]
                hit = slab > c16 if strict else slab >= c16
                parts.append(jnp.where(hit, jnp.int16(1), jnp.int16(0)))
            while len(parts) > 1:
                parts = [parts[i] + parts[i + 1] for i in range(0, len(parts), 2)]
            return acc + parts[0]

        acc = lax.fori_loop(0, nkb, body, jnp.zeros((rows16, TQ), I16))
        return jnp.sum(acc.astype(F32), axis=0, keepdims=True)

    def digit_select(ref, need, fallback):
        c0 = count16(ref, jnp.zeros((1, TQ), I32))
        ok0 = c0 >= need
        d0 = jnp.where(ok0, jnp.int32(0), jnp.int32(i16_min))
        n0 = jnp.where(ok0, c0, fallback)

        def bit_step(i, carry):
            d, n = carry
            cand = d | jnp.left_shift(jnp.int32(1), 14 - i)
            cnt = count16(ref, cand)
            ok = cnt >= need
            return jnp.where(ok, cand, d), jnp.where(ok, cnt, n)

        return lax.fori_loop(0, 15, bit_step, (d0, n0))

    total = (nkb * TK).astype(F32) * jnp.ones((1, TQ), F32)
    t_hi, n_hi = digit_select(hi_ref, kf, total)
    n_above = count16(hi_ref, t_hi, strict=True)
    t_hi16 = jnp.broadcast_to(t_hi, (rows16, TQ)).astype(I16)

    def restrict(kb, carry):
        for i in range(TK // rows16):
            sl = slice(i * rows16, (i + 1) * rows16)
            lo_ref[kb, sl, :] = jnp.where(hi_ref[kb, sl, :] == t_hi16, lo_ref[kb, sl, :], jnp.int16(i16_min))
        return carry

    lax.fori_loop(0, nkb, restrict, 0)
    t_lo, n_lo = digit_select(lo_ref, kf - n_above, n_hi - n_above)
    t = jnp.left_shift(t_hi, 16) + (t_lo + 0x8000)
    cge = n_above + n_lo
    t_ref[...] = t

    @pl.when(jnp.max(cge) > kf)
    def _():
        tt = t_ref[...]
        r = kf - (n_above + count16(lo_ref, t_lo, strict=True))
        off_threshold = 2 ** 15 - 1

        def mark(kb, carry):
            lo_ref[kb] = jnp.where(keys_ref[kb] == tt, key_i + kb * TK, off_threshold).astype(I16)
            return carry

        lax.fori_loop(0, nkb, mark, 0)

        def pos_step(i, pcut):
            cand = pcut | jnp.left_shift(jnp.int32(1), pos_bits - 1 - i)
            tied_before = total - count16(lo_ref, cand)
            return jnp.where(tied_before <= r, cand, pcut)

        pcut = lax.fori_loop(0, pos_bits, pos_step, jnp.zeros((1, TQ), I32))

        def demote(kb, carry):
            kk = keys_ref[kb]
            drop = jnp.logical_and(kk == tt, key_i + kb * TK >= pcut)
            keys_ref[kb] = jnp.where(drop, kk - 1, kk)
            return carry

        lax.fori_loop(0, nkb, demote, 0)

    def logits_of(hd):
        def logits(kb):
            r0 = pl.multiple_of(kb * TK, TK)
            sel = keys_ref[kb] >= t_ref[...]
            s = _dot(k_ref[pl.ds(r0, TK), :], qt_ref[:, hd * TQ:(hd + 1) * TQ])
            return jnp.where(sel, s + bias_ref[hd, _bias_tile_index(kb, qb)], NEG)
        return logits

    heads = range(DSA_HEADS)
    _attend_tiles([logits_of(hd) for hd in heads], [lambda kb: vt_ref[kb]] * DSA_HEADS, nkb, s_ref, m_ref, acc_ref)

    for hd in heads:
        o_ref[:, hd * LANES:(hd + 1) * LANES] = (acc_ref[hd, :LANES, :] / acc_ref[hd, LANES:LANES + 1, :]).T.astype(BF16)


def _dsa(sqn, skn, vt, proj, iw, bias, batch, seq):
    n = sqn.shape[0]
    nq = seq // TQ
    top_k = min(IDX_TOPK_MAX, seq // 4)
    pos_bits = int(seq).bit_length()
    iq_block = 4
    ik_block = (5 * 512 + 2 * LANES) // LANES
    return pl.pallas_call(
        functools.partial(_dsa_body, top_k=top_k, pos_bits=pos_bits),
        grid=(batch, nq),
        in_specs=[
            pl.BlockSpec((TQ, DSA_HEADS * LANES), lambda b, q: (b * nq + q, 0)),
            pl.BlockSpec((seq, LANES), lambda b, q: (b, 0)),
            pl.BlockSpec((None, None, seq // TK, VT_ROWS, TK), lambda b, q: (b, DIFF_HEADS, 0, 0, 0)),
            pl.BlockSpec((TQ, IDX_HEADS * IDX_DIM), lambda b, q: (b * nq + q, iq_block)),
            pl.BlockSpec((seq, LANES), lambda b, q: (b, ik_block)),
            pl.BlockSpec((TQ, LANES), lambda b, q: (b * nq + q, 0)),
            pl.BlockSpec((DSA_HEADS, 3, TK, TQ), lambda b, q: (1, 0, 0, 0)),
        ],
        out_specs=pl.BlockSpec((TQ, DSA_HEADS * LANES), lambda b, q: (b * nq + q, 0)),
        out_shape=jax.ShapeDtypeStruct((n, DSA_HEADS * LANES), BF16),
        scratch_shapes=[
            pltpu.VMEM((LANES, DSA_HEADS * TQ), BF16),
            pltpu.VMEM((IDX_DIM, IDX_HEADS * TQ), BF16),
            pltpu.VMEM((IDX_HEADS, TQ), F32),
            pltpu.VMEM((seq // TK, TK, TQ), I32),
            pltpu.VMEM((seq // TK, TK, TQ), I16),
            pltpu.VMEM((seq // TK, TK, TQ), I16),
            pltpu.VMEM((1, TQ), I32),
            pltpu.VMEM((DSA_HEADS, 1, TQ), F32),
            pltpu.VMEM((DSA_HEADS, VT_ROWS, TQ), F32),
            pltpu.VMEM((DSA_HEADS, 2, TK, TQ), F32),
        ],
        compiler_params=_cparams(2),
        name="dsa",
    )(sqn, skn, vt, proj, proj, iw, bias)


def _rel_bucket(dist):
    exact = REL_BUCKETS // 2
    n = jnp.maximum(dist, 0)
    nf = jnp.maximum(n, exact).astype(F32)
    far = exact + (jnp.log(nf / exact) / math.log(REL_MAX_DIST / exact) * (REL_BUCKETS - exact)).astype(I32)
    return jnp.where(n < exact, n, jnp.minimum(far, REL_BUCKETS - 1))


def _bias_tiles_body(tab_ref, bucket_ref, o_ref):
    hd = pl.program_id(0)
    nh = DIFF_HEADS + DSA_HEADS
    key = lax.broadcasted_iota(I32, (TK, TQ), 0)
    qry = lax.broadcasted_iota(I32, (TK, TQ), 1)
    for tile in range(3):
        bucket = bucket_ref[tile]
        acc = jnp.zeros(bucket.shape, F32)
        for b in range(REL_BUCKETS):
            acc = jnp.where(bucket == b, tab_ref[b * nh + hd], acc)
        if tile == 2:
            acc = jnp.where(key <= qry, acc, NEG)
        o_ref[tile] = acc * LOG2E


def _bias_tables(rel_bias):
    nh = DIFF_HEADS + DSA_HEADS
    assert TK + 1 >= REL_MAX_DIST
    j = jnp.arange(TK, dtype=I32)[:, None]
    i = jnp.arange(TQ, dtype=I32)[None, :]
    dist = jnp.stack([jnp.full((TK, TQ), 2 * TK, I32), i - j + TK, i - j])
    return pl.pallas_call(
        _bias_tiles_body,
        grid=(nh,),
        in_specs=[pl.BlockSpec(memory_space=pltpu.SMEM), pl.BlockSpec((3, TK, TQ), lambda h: (0, 0, 0))],
        out_specs=pl.BlockSpec((None, 3, TK, TQ), lambda h: (h, 0, 0, 0)),
        out_shape=jax.ShapeDtypeStruct((nh, 3, TK, TQ), F32),
        compiler_params=_cparams(1),
        name="bias_tiles",
    )(rel_bias.astype(F32).reshape(-1), _rel_bucket(dist))


def _block_diag_ones(width, group):
    r = np.arange(width)
    return jnp.asarray((r[:, None] // group) == (r[None, :] // group), dtype=BF16)


def _pad_lanes(w, width=LANES):
    return jnp.pad(w, ((0, 0), (0, width - w.shape[1])))


def kernel(x, rel_bias, mix_norm_g, ev_w_in, ev_w_out, gmlp_w_s, gmlp_b_s, gdn_conv_w, gdn_a_log, gdn_dt_bias,
           gdn_norm_g, od_w_in, od_w_out, diff_q_norm_g, diff_k_norm_g, diff_lambda, diff_sub_norm_g,
           dsa_q_norm_g, dsa_k_norm_g, ffn_norm_g, ffn_w_up, ffn_conv_w, ffn_conv_b, ffn_w_down):
    batch, seq, d = x.shape
    n = batch * seq
    depth = mix_norm_g.shape[0]
    assert d == D_MODEL and seq % max(TM_PROJ, TL_PREP, TL_GDN, TQ) == 0
    h = x.reshape(n, d)
    bias_tiles = _bias_tables(rel_bias)
    bd64 = _block_diag_ones(512, DIFF_QK)
    bd128 = _block_diag_ones(512, DSA_DIM)
    gw = GMLP_GROUPS * LANES
    qkv_w = 3 * GDN_HEADS * GDN_DIM

    for layer in range(depth):
        j = layer // 2
        g_mix = mix_norm_g[layer].reshape(1, d)
        if layer % 2 == 0:
            w = ev_w_in[j]
            o_u, o_v, o_qkv = 0, gw, 2 * gw
            o_b = o_qkv + qkv_w
            o_a = o_b + GDN_HEADS
            o_z = o_a + GDN_HEADS
            w_main = jnp.concatenate(
                [w[:, o_u:o_v], w[:, o_v:o_qkv], w[:, o_z:o_z + GDN_HEADS * GDN_DIM], w[:, o_qkv:o_b]],
                axis=1).astype(BF16)
            w_gate = _pad_lanes(w[:, o_b:o_z]).astype(BF16)
            proj, gates = _in_proj(h, g_mix, w_main, w_gate)
            gp = jnp.stack([
                _pad_lanes(jnp.concatenate([jnp.zeros((GDN_HEADS,), F32), gdn_a_log[j]])[None])[0],
                _pad_lanes(jnp.concatenate([jnp.zeros((GDN_HEADS,), F32), gdn_dt_bias[j]])[None])[0]])
            y_a, qkvn, gates2 = _even_prep(proj, gates, gmlp_w_s[j], gmlp_b_s[j].T, gdn_conv_w[j], gp, batch, seq)
            y_b = _gdn(qkvn, gates2, proj, gdn_norm_g[j].reshape(1, GDN_DIM), batch, seq)
            y1, y2, w_out = y_a, y_b, ev_w_out[j]
        else:
            lambda_init = 0.8 - 0.6 * math.exp(-0.3 * layer)
            w = od_w_in[j]
            c = np.cumsum([0, 512, 512, 512, 512, 128, 128, 512, 64, 8])
            dq, dk, dv, sq, sk, sv, iq, ik, iw = [w[:, c[i]:c[i + 1]] for i in range(9)]
            w_main = jnp.concatenate([dq, dk, dv, sq, iq, sk, sv, ik, ik], axis=1).astype(BF16)
            w_gate = _pad_lanes(iw).astype(BF16)
            proj, iw_out = _in_proj(h, g_mix, w_main, w_gate)
            dqn, dkn, sqn, skn, vt = _odd_prep(
                proj,
                jnp.tile(diff_q_norm_g[j], 2 * DIFF_HEADS)[None], jnp.tile(diff_k_norm_g[j], 2 * DIFF_HEADS)[None],
                jnp.tile(dsa_q_norm_g[j], DSA_HEADS)[None], dsa_k_norm_g[j][None], bd64, bd128, batch, seq)
            y_c = _diff_attn(dqn, dkn, vt, bias_tiles, diff_lambda[j], diff_sub_norm_g[j][None],
                             batch, seq, lambda_init)
            y_d = _dsa(sqn, skn, vt, proj, iw_out, bias_tiles, batch, seq)
            y1, y2, w_out = y_c, y_d, od_w_out[j]
        h = _ffn(h, y1, y2, w_out.astype(BF16), ffn_norm_g[layer].reshape(1, d), ffn_w_up[layer].astype(BF16),
                 ffn_conv_w[layer], ffn_conv_b[layer].reshape(1, 2 * D_FF), ffn_w_down[layer].astype(BF16), seq)
    return h.reshape(batch, seq, d)
```

```python
import functools
import math

import numpy as np
import jax
import jax.numpy as jnp
from jax import lax
from jax.experimental import pallas as pl
from jax.experimental.pallas import tpu as pltpu

F32 = jnp.float32
BF16 = jnp.bfloat16
I32 = jnp.int32
I16 = jnp.int16
HIGHEST = lax.Precision.HIGHEST

D_MODEL = 1024
GMLP_GROUPS = 4
GMLP_CHUNK = 128
GDN_HEADS = 4
GDN_DIM = 128
GDN_CHUNK = 64
GDN_CONV = 4
DIFF_HEADS = 4
DIFF_QK = 64
DSA_HEADS = 4
DSA_DIM = 128
IDX_HEADS = 8
IDX_DIM = 64
IDX_TOPK_MAX = 256
REL_BUCKETS = 32
REL_MAX_DIST = 128
D_FF = 2816
FFN_CONV = 3
EPS = 1e-6

LANES = 128
SUBLANES_F32 = 8
SUBLANES_BF16 = 16
VMEM_LIMIT = 56 * 1024 * 1024
VT_ROWS = LANES + SUBLANES_BF16

TM_PROJ = 512
TM_FFN = 512
TL_PREP = 512
TL_GDN = 512
GDN_SEQS_PER_STEP = 4
GDN_PREP_PROBLEMS = 16
TQ = 256
TK = 256
FF_CHUNK = 256
NEG = -1e30
LOG2E = 1.0 / math.log(2.0)
INT_MIN = -2 ** 31


def _cparams(n_axes):
    return pltpu.CompilerParams(dimension_semantics=("arbitrary",) * n_axes, vmem_limit_bytes=VMEM_LIMIT)


def _dot(a, b, **kw):
    return jnp.dot(a, b, preferred_element_type=F32, **kw)


def _dot_nt(a, b, **kw):
    return lax.dot_general(a, b, (((1,), (1,)), ((), ())), preferred_element_type=F32, **kw)


def _dot_tn(a, b, **kw):
    return lax.dot_general(a, b, (((0,), (0,)), ((), ())), preferred_element_type=F32, **kw)


def _rms(x, g):
    return x * lax.rsqrt(jnp.mean(x * x, axis=-1, keepdims=True) + EPS) * g


def _sigmoid(x):
    return 1.0 / (1.0 + jnp.exp2(x * -LOG2E))


def _silu(x):
    return x / (1.0 + jnp.exp2(x * -LOG2E))


def _gelu_tanh(x):
    k0 = -2.0 * math.sqrt(2.0 / math.pi) * LOG2E
    return x / (1.0 + jnp.exp2(x * (k0 + (k0 * 0.044715) * (x * x))))


def _softplus(x):
    return jnp.maximum(x, 0.0) + jnp.log(1.0 + jnp.exp(-jnp.abs(x)))


def _in_proj_body(h_ref, g_ref, w_ref, wg_ref, o_ref, og_ref, *, nout):
    xn = _rms(h_ref[...], g_ref[...]).astype(BF16)
    for c in range(0, nout, 512):
        e = min(c + 512, nout)
        o_ref[:, c:e] = _dot(xn, w_ref[:, c:e]).astype(BF16)
    og_ref[...] = _dot(xn, wg_ref[...])


def _in_proj(h, g, w, wg):
    n, d = h.shape
    nout = w.shape[1]
    return pl.pallas_call(
        functools.partial(_in_proj_body, nout=nout),
        grid=(n // TM_PROJ,),
        in_specs=[
            pl.BlockSpec((TM_PROJ, d), lambda i: (i, 0)),
            pl.BlockSpec((1, d), lambda i: (0, 0)),
            pl.BlockSpec((d, nout), lambda i: (0, 0)),
            pl.BlockSpec((d, LANES), lambda i: (0, 0)),
        ],
        out_specs=[
            pl.BlockSpec((TM_PROJ, nout), lambda i: (i, 0)),
            pl.BlockSpec((TM_PROJ, LANES), lambda i: (i, 0)),
        ],
        out_shape=[jax.ShapeDtypeStruct((n, nout), BF16), jax.ShapeDtypeStruct((n, LANES), F32)],
        compiler_params=_cparams(1),
        name="in_proj",
    )(h, g, w, wg)


def _ffn_body(h_ref, halo_ref, y1_ref, y1h_ref, y2_ref, y2h_ref, wo_ref, g_ref, wup_ref, cw_ref, cb_ref, wdn_ref,
              o_ref, xn_ref, act_ref, *, tiles_per_seq):
    tm = h_ref.shape[0]
    i = pl.program_id(0)
    half = y1_ref.shape[1]

    def mixer_residual(hh, y1, y2):
        return hh + _dot(y1, wo_ref[:half, :]) + _dot(y2, wo_ref[half:, :])

    h = mixer_residual(h_ref[...], y1_ref[...], y2_ref[...])
    g = g_ref[...]
    hb = SUBLANES_BF16
    xn_ref[hb:, :] = _rms(h, g).astype(BF16)
    keep = jnp.where(i % tiles_per_seq == 0, 0.0, 1.0)
    halo = mixer_residual(halo_ref[...], y1h_ref[...], y2h_ref[...])
    xn_ref[:hb, :] = (_rms(halo, g) * keep).astype(BF16)
    xn = xn_ref[...]

    def up_pair(c):
        return [_dot(xn, wup_ref[:, base:base + FF_CHUNK]) for base in (c, D_FF + c)]

    def conv(up, base):
        cw = cw_ref[:, base:base + FF_CHUNK]
        return (cw[0:1] * up[hb - 2:tm + hb - 2] + cw[1:2] * up[hb - 1:tm + hb - 1] + cw[2:3] * up[hb:tm + hb]
                + cb_ref[:, base:base + FF_CHUNK])

    chunks = list(range(0, D_FF, FF_CHUNK))
    ups = up_pair(chunks[0])
    for n, c in enumerate(chunks):
        nxt = up_pair(chunks[n + 1]) if n + 1 < len(chunks) else None
        act_ref[:, c:c + FF_CHUNK] = (_silu(conv(ups[0], c)) * conv(ups[1], D_FF + c)).astype(BF16)
        ups = nxt
    o_ref[...] = h + _dot(act_ref[...], wdn_ref[...])


def _ffn(h, y1, y2, wo, g, wup, cw, cb, wdn, seq):
    n, d = h.shape
    half = y1.shape[1]
    tiles_per_seq = seq // TM_FFN
    halo_blocks = TM_FFN // SUBLANES_BF16
    const = dict(pipeline_mode=pl.Buffered(1))
    tile = lambda i: (i, 0)
    halo = lambda i: (jnp.maximum(i * halo_blocks - 1, 0), 0)
    return pl.pallas_call(
        functools.partial(_ffn_body, tiles_per_seq=tiles_per_seq),
        grid=(n // TM_FFN,),
        in_specs=[
            pl.BlockSpec((TM_FFN, d), tile),
            pl.BlockSpec((SUBLANES_BF16, d), halo),
            pl.BlockSpec((TM_FFN, half), tile),
            pl.BlockSpec((SUBLANES_BF16, half), halo),
            pl.BlockSpec((TM_FFN, half), tile),
            pl.BlockSpec((SUBLANES_BF16, half), halo),
            pl.BlockSpec((2 * half, d), lambda i: (0, 0), **const),
            pl.BlockSpec((1, d), lambda i: (0, 0)),
            pl.BlockSpec((d, 2 * D_FF), lambda i: (0, 0), **const),
            pl.BlockSpec((FFN_CONV, 2 * D_FF), lambda i: (0, 0)),
            pl.BlockSpec((1, 2 * D_FF), lambda i: (0, 0)),
            pl.BlockSpec((D_FF, d), lambda i: (0, 0), **const),
        ],
        out_specs=pl.BlockSpec((TM_FFN, d), lambda i: (i, 0)),
        out_shape=jax.ShapeDtypeStruct((n, d), F32),
        scratch_shapes=[pltpu.VMEM((TM_FFN + SUBLANES_BF16, d), BF16), pltpu.VMEM((TM_FFN, D_FF), BF16)],
        compiler_params=_cparams(1),
        name="ffn",
    )(h, h, y1, y1, y2, y2, wo, g, wup, cw, cb, wdn)


def _even_prep_body(uv_ref, qkv_ref, halo_ref, gate_ref, ws_ref, bs_ref, cw_ref, gp_ref,
                    ya_ref, qkvo_ref, gout_ref, xs_ref):
    tl = uv_ref.shape[0]
    t = pl.program_id(1)
    gw = LANES
    row = lax.broadcasted_iota(I32, (GMLP_CHUNK, GMLP_CHUNK), 0)
    col = lax.broadcasted_iota(I32, (GMLP_CHUNK, GMLP_CHUNK), 1)
    for gi in range(GMLP_GROUPS):
        u = _gelu_tanh(uv_ref[:, gi * gw:(gi + 1) * gw].astype(F32))
        v = _gelu_tanh(uv_ref[:, (GMLP_GROUPS + gi) * gw:(GMLP_GROUPS + gi + 1) * gw].astype(F32))
        vc = v - jnp.mean(v, axis=-1, keepdims=True)
        vn = (vc * lax.rsqrt(jnp.mean(vc * vc, axis=-1, keepdims=True) + EPS)).astype(BF16)
        w = jnp.where(col <= row, ws_ref[gi], 0.0).astype(BF16)
        b = bs_ref[:, gi:gi + 1]
        for c in range(tl // GMLP_CHUNK):
            r0 = c * GMLP_CHUNK
            mixed = _dot(w, vn[r0:r0 + GMLP_CHUNK]) + b
            ya_ref[r0:r0 + GMLP_CHUNK, gi * gw:(gi + 1) * gw] = (u[r0:r0 + GMLP_CHUNK] * mixed).astype(BF16)
    keep = jnp.where(t == 0, 0.0, 1.0)
    hb = SUBLANES_BF16
    for j in range(3 * GDN_HEADS):
        sl = slice(j * gw, (j + 1) * gw)
        xs_ref[:hb, :] = halo_ref[:, sl].astype(F32) * keep
        xs_ref[hb:, :] = qkv_ref[:, sl].astype(F32)
        cw = cw_ref[:, sl]
        y = cw[0:1] * xs_ref[hb - 3:tl + hb - 3, :]
        for k in range(1, GDN_CONV):
            y = y + cw[k:k + 1] * xs_ref[hb - 3 + k:tl + hb - 3 + k, :]
        y = _silu(y)
        if j < 2 * GDN_HEADS:
            y = y * lax.rsqrt(jnp.sum(y * y, axis=-1, keepdims=True) + EPS)
        if j < GDN_HEADS:
            y = y * (GDN_DIM ** -0.5)
        qkvo_ref[:, sl] = y.astype(BF16)
    x = gate_ref[...]
    lane = lax.broadcasted_iota(I32, x.shape, 1)
    beta = _sigmoid(x)
    gdec = -jnp.exp(gp_ref[0:1, :]) * _softplus(x + gp_ref[1:2, :])
    gout_ref[...] = jnp.where(lane < GDN_HEADS, beta, gdec)


def _even_prep(proj, gates, ws, bs_t, cw, gp, batch, seq):
    n = proj.shape[0]
    nt = seq // TL_PREP
    halo_blocks = TL_PREP // SUBLANES_BF16
    qkv_w = 3 * GDN_HEADS * GDN_DIM
    row = lambda b, t: b * nt + t
    return pl.pallas_call(
        _even_prep_body,
        grid=(batch, nt),
        in_specs=[
            pl.BlockSpec((TL_PREP, 2 * GMLP_GROUPS * LANES), lambda b, t: (row(b, t), 0)),
            pl.BlockSpec((TL_PREP, qkv_w), lambda b, t: (row(b, t), 1)),
            pl.BlockSpec((SUBLANES_BF16, qkv_w), lambda b, t: (jnp.maximum(row(b, t) * halo_blocks - 1, 0), 1)),
            pl.BlockSpec((TL_PREP, LANES), lambda b, t: (row(b, t), 0)),
            pl.BlockSpec((GMLP_GROUPS, GMLP_CHUNK, GMLP_CHUNK), lambda b, t: (0, 0, 0)),
            pl.BlockSpec((GMLP_CHUNK, GMLP_GROUPS), lambda b, t: (0, 0)),
            pl.BlockSpec((GDN_CONV, qkv_w), lambda b, t: (0, 0)),
            pl.BlockSpec((2, LANES), lambda b, t: (0, 0)),
        ],
        out_specs=[
            pl.BlockSpec((TL_PREP, GMLP_GROUPS * LANES), lambda b, t: (row(b, t), 0)),
            pl.BlockSpec((TL_PREP, qkv_w), lambda b, t: (row(b, t), 0)),
            pl.BlockSpec((TL_PREP, LANES), lambda b, t: (row(b, t), 0)),
        ],
        out_shape=[
            jax.ShapeDtypeStruct((n, GMLP_GROUPS * LANES), BF16),
            jax.ShapeDtypeStruct((n, qkv_w), BF16),
            jax.ShapeDtypeStruct((n, LANES), F32),
        ],
        scratch_shapes=[pltpu.VMEM((TL_PREP + SUBLANES_BF16, LANES), F32)],
        compiler_params=_cparams(2),
        name="even_prep",
    )(proj, proj, proj, gates, ws, bs_t, cw, gp)


def _split_bf16(x):
    hi = x.astype(BF16)
    return hi, (x - hi.astype(F32)).astype(BF16)


def _mm3(a, b):
    ah, al = _split_bf16(a)
    bh, bl = _split_bf16(b)
    return _dot(ah, bh) + (_dot(ah, bl) + _dot(al, bh))


def _unit_lower_inverses(lows):
    c = lows[0].shape[0]
    row = lax.broadcasted_iota(I32, (c, c), 0)
    col = lax.broadcasted_iota(I32, (c, c), 1)
    eye = jnp.where(row == col, 1.0, 0.0)
    same16 = jnp.right_shift(row, 4) == jnp.right_shift(col, 4)
    same32 = jnp.right_shift(row, 5) == jnp.right_shift(col, 5)
    mm1 = lambda a, b: _dot(a.astype(BF16), b.astype(BF16))
    ps = [jnp.where(same16, -low, 0.0) for low in lows]
    xs = [eye + p for p in ps]
    for mm in (_mm3, mm1, mm1):
        ps = [mm(p, p) for p in ps]
        xs = [x + mm(x, p) for x, p in zip(xs, ps)]
    mid = jnp.logical_and(same32, jnp.logical_not(same16))
    ys = [mm1(jnp.where(mid, low, 0.0), x) for low, x in zip(lows, xs)]
    xs = [x - mm1(x, y) for x, y in zip(xs, ys)]
    ys = [mm1(jnp.where(same32, 0.0, low), x) for low, x in zip(lows, xs)]
    return [x - mm1(x, y) for x, y in zip(xs, ys)]


def _gdn_body(qkv_ref, gate_ref, z_ref, gn_ref, o_ref,
              s_ref, u_ref, w_ref, qg_ref, kdt_ref, intra_ref, egl_ref):
    nseq, tl = qkv_ref.shape[0], qkv_ref.shape[1]
    c = GDN_CHUNK
    d = GDN_DIM
    lanes = [(bi, hd) for bi in range(nseq) for hd in range(GDN_HEADS)]
    flat = lambda bi, hd: bi * GDN_HEADS + hd
    prep_chunks = max(1, GDN_PREP_PROBLEMS // len(lanes))
    t = pl.program_id(1)

    @pl.when(t == 0)
    def _():
        s_ref[...] = jnp.zeros_like(s_ref)

    row = lax.broadcasted_iota(I32, (c, c), 0)
    col = lax.broadcasted_iota(I32, (c, c), 1)
    incl = col <= row
    strict = col < row
    lmat = jnp.where(incl, 1.0, 0.0)
    gn = gn_ref[...]

    def prepare(it, carry):
        cis = [it * prep_chunks + j for j in range(prep_chunks)]
        rows = [pl.ds(pl.multiple_of(ci * c, c), c) for ci in cis]
        gates = {(j, bi): gate_ref[bi, rows[j], :] for j in range(prep_chunks) for bi in range(nseq)}
        gcum = {key: _dot(lmat, g, precision=HIGHEST) for key, g in gates.items()}
        pairs = [(j, bi, hd) for j in range(prep_chunks) for bi, hd in lanes]
        qs = [qkv_ref[bi, rows[j], hd * d:(hd + 1) * d] for j, bi, hd in pairs]
        ks = [qkv_ref[bi, rows[j], (GDN_HEADS + hd) * d:(GDN_HEADS + hd + 1) * d] for j, bi, hd in pairs]
        vs = [qkv_ref[bi, rows[j], (2 * GDN_HEADS + hd) * d:(2 * GDN_HEADS + hd + 1) * d] for j, bi, hd in pairs]
        beta = [jnp.broadcast_to(gates[j, bi][:, hd:hd + 1], (c, d)) for j, bi, hd in pairs]
        gc = [jnp.broadcast_to(gcum[j, bi][:, GDN_HEADS + hd:GDN_HEADS + hd + 1], (c, d)) for j, bi, hd in pairs]
        decay = [jnp.where(incl, jnp.exp(jnp.where(incl, g[:, :c] - g.T[:c, :], 0.0)), 0.0) for g in gc]
        kb = [k.astype(F32) * b for k, b in zip(ks, beta)]
        lows = [jnp.where(strict, _dot_nt(x.astype(BF16), k) * dc, 0.0) for x, k, dc in zip(kb, ks, decay)]
        tinv = [x.astype(BF16) for x in _unit_lower_inverses(lows)]
        egc = [jnp.exp(g) for g in gc]
        rhs = [jnp.concatenate([(v.astype(F32) * b).astype(BF16), (x * e).astype(BF16)], axis=1)
               for v, b, x, e in zip(vs, beta, kb, egc)]
        uw = [_dot(ti, r) for ti, r in zip(tinv, rhs)]
        qk = [_dot_nt(q, k) for q, k in zip(qs, ks)]
        for n, (j, bi, hd) in enumerate(pairs):
            ci, f = cis[j], flat(bi, hd)
            u_ref[ci, f] = uw[n][:, :d]
            w_ref[ci, f] = uw[n][:, d:].astype(BF16)
            intra_ref[ci, f] = jnp.where(incl, qk[n] * decay[n], 0.0).astype(BF16)
            qg_ref[ci, f] = (qs[n].astype(F32) * egc[n]).astype(BF16)
            g_last = gc[n][c - 1:c, :]
            kdt_ref[ci, f] = (ks[n].astype(F32) * jnp.exp(g_last - gc[n])).T.astype(BF16)
            egl_ref[ci, f] = jnp.exp(g_last)
        return carry

    lax.fori_loop(0, tl // c // prep_chunks, prepare, 0)

    def scan(ci, carry):
        r0 = pl.multiple_of(ci * c, c)
        fs = [flat(bi, hd) for bi, hd in lanes]
        states = [s_ref[f] for f in fs]
        sb = [s.astype(BF16) for s in states]
        ws = [_dot(w_ref[ci, f], s) for f, s in zip(fs, sb)]
        qsd = [_dot(qg_ref[ci, f], s) for f, s in zip(fs, sb)]
        vnb = [(u_ref[ci, f] - w).astype(BF16) for f, w in zip(fs, ws)]
        for f, s, v in zip(fs, states, vnb):
            s_ref[f] = s * egl_ref[ci, f] + _dot(kdt_ref[ci, f], v)
        for (bi, hd), f, qd, v in zip(lanes, fs, qsd, vnb):
            out = qd + _dot(intra_ref[ci, f], v)
            z = z_ref[bi, pl.ds(r0, c), hd * d:(hd + 1) * d].astype(F32)
            o_ref[bi, pl.ds(r0, c), hd * d:(hd + 1) * d] = (_rms(out, gn) * _silu(z)).astype(BF16)
        return carry

    lax.fori_loop(0, tl // c, scan, 0)


def _gdn(qkv, gates, proj, gn, batch, seq):
    n = qkv.shape[0]
    nt = seq // TL_GDN
    nc = TL_GDN // GDN_CHUNK
    width = GDN_HEADS * GDN_DIM
    nseq = GDN_SEQS_PER_STEP if batch % GDN_SEQS_PER_STEP == 0 else 1
    nl = nseq * GDN_HEADS
    as3d = lambda a: a.reshape(batch, seq, a.shape[-1])
    out = pl.pallas_call(
        _gdn_body,
        grid=(batch // nseq, nt),
        in_specs=[
            pl.BlockSpec((nseq, TL_GDN, 3 * width), lambda b, t: (b, t, 0)),
            pl.BlockSpec((nseq, TL_GDN, LANES), lambda b, t: (b, t, 0)),
            pl.BlockSpec((nseq, TL_GDN, width), lambda b, t: (b, t, 2)),
            pl.BlockSpec((1, GDN_DIM), lambda b, t: (0, 0)),
        ],
        out_specs=pl.BlockSpec((nseq, TL_GDN, width), lambda b, t: (b, t, 0)),
        out_shape=jax.ShapeDtypeStruct((batch, seq, width), BF16),
        scratch_shapes=[
            pltpu.VMEM((nl, GDN_DIM, GDN_DIM), F32),
            pltpu.VMEM((nc, nl, GDN_CHUNK, GDN_DIM), F32),
            pltpu.VMEM((nc, nl, GDN_CHUNK, GDN_DIM), BF16),
            pltpu.VMEM((nc, nl, GDN_CHUNK, GDN_DIM), BF16),
            pltpu.VMEM((nc, nl, GDN_DIM, GDN_CHUNK), BF16),
            pltpu.VMEM((nc, nl, GDN_CHUNK, GDN_CHUNK), BF16),
            pltpu.VMEM((nc, nl, 1, GDN_DIM), F32),
        ],
        compiler_params=_cparams(2),
        name="gdn",
    )(as3d(qkv), as3d(gates), as3d(proj), gn)
    return out.reshape(n, width)


def _group_mean_sq(x, ones_bd, group):
    x2 = x * x
    hi = x2.astype(BF16)
    lo = (x2 - hi.astype(F32)).astype(BF16)
    return (_dot(hi, ones_bd) + _dot(lo, ones_bd)) * (1.0 / group)


def _odd_prep_body(dq_ref, dk_ref, dv_ref, sq_ref, sk_ref, sv_ref, gq_ref, gk_ref, gsq_ref, gsk_ref, bd64_ref,
                   bd128_ref, dqo_ref, dko_ref, sqo_ref, sko_ref, vt_ref):
    ones = jnp.ones((VT_ROWS - LANES, TK), BF16)
    for j in range(DIFF_HEADS + 1):
        src = sv_ref[...] if j == DIFF_HEADS else dv_ref[:, j * LANES:(j + 1) * LANES]
        xt = src.astype(F32).T
        for c in range(xt.shape[1] // TK):
            vt_ref[j, c, :LANES, :] = xt[:, c * TK:(c + 1) * TK].astype(BF16)
            vt_ref[j, c, LANES:, :] = ones
    bd64 = bd64_ref[...]
    x = dq_ref[...].astype(F32)
    dqo_ref[...] = (x * lax.rsqrt(_group_mean_sq(x, bd64, DIFF_QK) + EPS) * gq_ref[...]
                    * (DIFF_QK ** -0.5 * LOG2E)).astype(BF16)
    x = dk_ref[...].astype(F32)
    dko_ref[...] = (x * lax.rsqrt(_group_mean_sq(x, bd64, DIFF_QK) + EPS) * gk_ref[...]).astype(BF16)
    x = sq_ref[...].astype(F32)
    sqo_ref[...] = (x * lax.rsqrt(_group_mean_sq(x, bd128_ref[...], DSA_DIM) + EPS) * gsq_ref[...]
                    * (DSA_DIM ** -0.5 * LOG2E)).astype(BF16)
    x = sk_ref[...].astype(F32)
    sko_ref[...] = _rms(x, gsk_ref[...]).astype(BF16)


def _odd_prep(proj, gq, gk, gsq, gsk, bd64, bd128, batch, seq):
    n = proj.shape[0]
    w = 512
    nt = seq // TL_PREP
    kt = TL_PREP // TK
    full = lambda shape: pl.BlockSpec(shape, lambda b, t: (0,) * len(shape))
    row = lambda b, t: b * nt + t
    return pl.pallas_call(
        _odd_prep_body,
        grid=(batch, nt),
        in_specs=[
            pl.BlockSpec((TL_PREP, w), lambda b, t: (row(b, t), 0)),
            pl.BlockSpec((TL_PREP, w), lambda b, t: (row(b, t), 1)),
            pl.BlockSpec((TL_PREP, w), lambda b, t: (row(b, t), 2)),
            pl.BlockSpec((TL_PREP, w), lambda b, t: (row(b, t), 3)),
            pl.BlockSpec((TL_PREP, LANES), lambda b, t: (row(b, t), 5 * w // LANES)),
            pl.BlockSpec((TL_PREP, LANES), lambda b, t: (row(b, t), 5 * w // LANES + 1)),
            full((1, w)), full((1, w)), full((1, w)), full((1, LANES)), full((w, w)), full((w, w)),
        ],
        out_specs=[
            pl.BlockSpec((TL_PREP, w), lambda b, t: (row(b, t), 0)),
            pl.BlockSpec((TL_PREP, w), lambda b, t: (row(b, t), 0)),
            pl.BlockSpec((TL_PREP, w), lambda b, t: (row(b, t), 0)),
            pl.BlockSpec((TL_PREP, LANES), lambda b, t: (row(b, t), 0)),
            pl.BlockSpec((None, DIFF_HEADS + 1, kt, VT_ROWS, TK), lambda b, t: (b, 0, t, 0, 0)),
        ],
        out_shape=[
            jax.ShapeDtypeStruct((n, w), BF16), jax.ShapeDtypeStruct((n, w), BF16),
            jax.ShapeDtypeStruct((n, w), BF16), jax.ShapeDtypeStruct((n, LANES), BF16),
            jax.ShapeDtypeStruct((batch, DIFF_HEADS + 1, seq // TK, VT_ROWS, TK), BF16),
        ],
        compiler_params=_cparams(2),
        name="odd_prep",
    )(proj, proj, proj, proj, proj, proj, gq, gk, gsq, gsk, bd64, bd128)


def _bias_tile_index(kb, qb):
    return jnp.clip(kb - (qb - 2), 0, 2)


def _softmax_step_t(s_t, vt, m_ref, acc_ref):
    for g in range(s_t.shape[1] // LANES):
        sl = slice(g * LANES, (g + 1) * LANES)
        s = s_t[:, sl]
        m_prev = m_ref[:, sl]
        m_new = jnp.maximum(m_prev, jnp.max(s, axis=0, keepdims=True))
        alpha = jnp.exp2(m_prev - m_new)
        p = jnp.exp2(s - m_new)
        acc_ref[:, sl] = alpha * acc_ref[:, sl] + _dot(vt, p.astype(BF16))
        m_ref[:, sl] = m_new


def _attend_tiles(logits, values, nblk, s_ref, m_ref, acc_ref):
    streams = range(len(logits))
    m_ref[...] = jnp.full_like(m_ref, NEG)
    acc_ref[...] = jnp.zeros_like(acc_ref)
    for i in streams:
        s_ref[i, 0] = logits[i](0)

    def pair(it, carry):
        kb = 2 * it
        for i in streams:
            s_ref[i, 1] = logits[i](kb + 1)
        for i in streams:
            _softmax_step_t(s_ref[i, 0], values[i](kb), m_ref.at[i], acc_ref.at[i])
        for i in streams:
            s_ref[i, 0] = logits[i](jnp.minimum(kb + 2, nblk - 1))
        for i in streams:
            _softmax_step_t(s_ref[i, 1], values[i](kb + 1), m_ref.at[i], acc_ref.at[i])
        return carry

    lax.fori_loop(0, nblk // 2, pair, 0)

    @pl.when(nblk % 2 == 1)
    def _():
        for i in streams:
            _softmax_step_t(s_ref[i, 0], values[i](nblk - 1), m_ref.at[i], acc_ref.at[i])


def _diff_attn_body(q_ref, k_ref, vt_ref, bias_ref, lam_ref, sg_ref, o_ref,
                    qt_ref, m_ref, acc_ref, s_ref, *, lambda_init):
    qi = pl.program_id(1)
    nblk = qi + 1
    heads = range(DIFF_HEADS)
    for hd in heads:
        qt = q_ref[:, hd * LANES:(hd + 1) * LANES].astype(F32).T
        dim = lax.broadcasted_iota(I32, qt.shape, 0)
        qt_ref[hd, :, :TQ] = jnp.where(dim < DIFF_QK, qt, 0.0).astype(BF16)
        qt_ref[hd, :, TQ:] = jnp.where(dim >= DIFF_QK, qt, 0.0).astype(BF16)

    def logits_of(hd):
        def logits(kb):
            r0 = pl.multiple_of(kb * TK, TK)
            b = bias_ref[hd, _bias_tile_index(kb, qi)]
            s = _dot(k_ref[pl.ds(r0, TK), hd * LANES:(hd + 1) * LANES], qt_ref[hd])
            return s + jnp.concatenate([b, b], axis=1)
        return logits

    _attend_tiles([logits_of(hd) for hd in heads], [lambda kb, hd=hd: vt_ref[hd, kb] for hd in heads],
                  nblk, s_ref, m_ref, acc_ref)

    lf = lam_ref[...]
    lam = (jnp.exp(jnp.sum(lf[0:1] * lf[1:2], axis=-1, keepdims=True))
           - jnp.exp(jnp.sum(lf[2:3] * lf[3:4], axis=-1, keepdims=True)) + lambda_init)
    for hd in heads:
        den = acc_ref[hd, LANES:LANES + 1, :]
        ot = (acc_ref[hd, :LANES, :TQ] / den[:, :TQ] - lam * (acc_ref[hd, :LANES, TQ:] / den[:, TQ:]))
        ot = ot * lax.rsqrt(jnp.mean(ot * ot, axis=0, keepdims=True) + EPS)
        o_ref[:, hd * LANES:(hd + 1) * LANES] = (ot.T * sg_ref[...] * (1.0 - lambda_init)).astype(BF16)


def _diff_attn(dqn, dkn, vt, bias, lam_p, sub_g, batch, seq, lambda_init):
    n = dqn.shape[0]
    nq = seq // TQ
    width = DIFF_HEADS * LANES
    return pl.pallas_call(
        functools.partial(_diff_attn_body, lambda_init=lambda_init),
        grid=(batch, nq),
        in_specs=[
            pl.BlockSpec((TQ, width), lambda b, q: (b * nq + q, 0)),
            pl.BlockSpec((seq, width), lambda b, q: (b, 0)),
            pl.BlockSpec((None, DIFF_HEADS + 1, seq // TK, VT_ROWS, TK), lambda b, q: (b, 0, 0, 0, 0)),
            pl.BlockSpec((DIFF_HEADS, 3, TK, TQ), lambda b, q: (0, 0, 0, 0)),
            pl.BlockSpec((4, DIFF_QK), lambda b, q: (0, 0)),
            pl.BlockSpec((1, LANES), lambda b, q: (0, 0)),
        ],
        out_specs=pl.BlockSpec((TQ, width), lambda b, q: (b * nq + q, 0)),
        out_shape=jax.ShapeDtypeStruct((n, width), BF16),
        scratch_shapes=[
            pltpu.VMEM((DIFF_HEADS, LANES, 2 * TQ), BF16),
            pltpu.VMEM((DIFF_HEADS, 1, 2 * TQ), F32),
            pltpu.VMEM((DIFF_HEADS, VT_ROWS, 2 * TQ), F32),
            pltpu.VMEM((DIFF_HEADS, 2, TK, 2 * TQ), F32),
        ],
        compiler_params=_cparams(2),
        name="diff_attn",
    )(dqn, dkn, vt, bias, lam_p, sub_g)


def _dsa_body(q_ref, k_ref, vt_ref, iq_ref, ik_ref, iw_ref, bias_ref, o_ref,
              qt_ref, qit_ref, wt_ref, keys_ref, hi_ref, lo_ref, t_ref, m_ref, acc_ref, s_ref, *, top_k, pos_bits):
    qb = pl.program_id(1)
    nkb = qb + 1

    for p in range(IDX_HEADS // 2):
        gt = iq_ref[:, p * LANES:(p + 1) * LANES].astype(F32).T.astype(BF16)
        qit_ref[:, (2 * p) * TQ:(2 * p + 1) * TQ] = gt[:IDX_DIM]
        qit_ref[:, (2 * p + 1) * TQ:(2 * p + 2) * TQ] = gt[IDX_DIM:]
    for hd in range(DSA_HEADS):
        qt_ref[:, hd * TQ:(hd + 1) * TQ] = q_ref[:, hd * LANES:(hd + 1) * LANES].astype(F32).T.astype(BF16)
    wscale = (IDX_HEADS ** -0.5) * (IDX_DIM ** -0.5)
    wt_ref[...] = (iw_ref[...] * wscale).T[:IDX_HEADS, :]

    key_i = lax.broadcasted_iota(I32, (TK, TQ), 0)
    qry_i = lax.broadcasted_iota(I32, (TK, TQ), 1)
    causal = key_i <= qry_i

    def index_blocks(kbs, masked):
        iks = [ik_ref[pl.ds(pl.multiple_of(kb * TK, TK), TK), :IDX_DIM] for kb in kbs]
        idxs = [jnp.zeros((TK, TQ), F32) for _ in kbs]
        for hi in range(IDX_HEADS):
            ss = [_dot(ik, qit_ref[:, hi * TQ:(hi + 1) * TQ]) for ik in iks]
            idxs = [idx + jnp.maximum(s, 0.0) * wt_ref[hi:hi + 1, :] for idx, s in zip(idxs, ss)]
        for kb, idx in zip(kbs, idxs):
            idx = jnp.where(idx == 0.0, 0.0, idx)
            bits = pltpu.bitcast(idx, I32)
            key = jnp.where(bits < 0, bits ^ jnp.int32(0x7FFFFFFF), bits)
            if masked:
                key = jnp.where(causal, key, jnp.int32(INT_MIN))
            keys_ref[kb] = key
            hi_ref[kb] = jnp.right_shift(key, 16).astype(I16)
            lo_ref[kb] = ((key & 0xFFFF) - 0x8000).astype(I16)

    def index_pair(i, carry):
        index_blocks([2 * i, 2 * i + 1], False)
        return carry

    lax.fori_loop(0, (nkb - 1) // 2, index_pair, 0)

    @pl.when((nkb - 1) % 2 == 1)
    def _():
        index_blocks([nkb - 2], False)

    index_blocks([nkb - 1], True)

    kf = float(top_k)
    rows16 = SUBLANES_BF16
    i16_min = -2 ** 15

    def count16(ref, cand, strict=False):
        c16 = jnp.broadcast_to(cand, (rows16, TQ)).astype(I16)

        def body(kb, acc):
            x = ref[kb]
            parts = []
            for i in range(TK // rows16):
                slab = x[i * rows16:(i + 1) * rows16]
                hit = slab > c16 if strict else slab >= c16
                parts.append(jnp.where(hit, jnp.int16(1), jnp.int16(0)))
            while len(parts) > 1:
                parts = [parts[i] + parts[i + 1] for i in range(0, len(parts), 2)]
            return acc + parts[0]

        acc = lax.fori_loop(0, nkb, body, jnp.zeros((rows16, TQ), I16))
        return jnp.sum(acc.astype(F32), axis=0, keepdims=True)

    def digit_select(ref, need, fallback):
        c0 = count16(ref, jnp.zeros((1, TQ), I32))
        ok0 = c0 >= need
        d0 = jnp.where(ok0, jnp.int32(0), jnp.int32(i16_min))
        n0 = jnp.where(ok0, c0, fallback)

        def bit_step(i, carry):
            d, n = carry
            cand = d | jnp.left_shift(jnp.int32(1), 14 - i)
            cnt = count16(ref, cand)
            ok = cnt >= need
            return jnp.where(ok, cand, d), jnp.where(ok, cnt, n)

        return lax.fori_loop(0, 15, bit_step, (d0, n0))

    total = (nkb * TK).astype(F32) * jnp.ones((1, TQ), F32)
    t_hi, n_hi = digit_select(hi_ref, kf, total)
    n_above = count16(hi_ref, t_hi, strict=True)
    t_hi16 = jnp.broadcast_to(t_hi, (rows16, TQ)).astype(I16)

    def restrict(kb, carry):
        for i in range(TK // rows16):
            sl = slice(i * rows16, (i + 1) * rows16)
            lo_ref[kb, sl, :] = jnp.where(hi_ref[kb, sl, :] == t_hi16, lo_ref[kb, sl, :], jnp.int16(i16_min))
        return carry

    lax.fori_loop(0, nkb, restrict, 0)
    t_lo, n_lo = digit_select(lo_ref, kf - n_above, n_hi - n_above)
    t = jnp.left_shift(t_hi, 16) + (t_lo + 0x8000)
    cge = n_above + n_lo
    t_ref[...] = t

    @pl.when(jnp.max(cge) > kf)
    def _():
        tt = t_ref[...]
        r = kf - (n_above + count16(lo_ref, t_lo, strict=True))
        off_threshold = 2 ** 15 - 1

        def mark(kb, carry):
            lo_ref[kb] = jnp.where(keys_ref[kb] == tt, key_i + kb * TK, off_threshold).astype(I16)
            return carry

        lax.fori_loop(0, nkb, mark, 0)

        def pos_step(i, pcut):
            cand = pcut | jnp.left_shift(jnp.int32(1), pos_bits - 1 - i)
            tied_before = total - count16(lo_ref, cand)
            return jnp.where(tied_before <= r, cand, pcut)

        pcut = lax.fori_loop(0, pos_bits, pos_step, jnp.zeros((1, TQ), I32))

        def demote(kb, carry):
            kk = keys_ref[kb]
            drop = jnp.logical_and(kk == tt, key_i + kb * TK >= pcut)
            keys_ref[kb] = jnp.where(drop, kk - 1, kk)
            return carry

        lax.fori_loop(0, nkb, demote, 0)

    def logits_of(hd):
        def logits(kb):
            r0 = pl.multiple_of(kb * TK, TK)
            sel = keys_ref[kb] >= t_ref[...]
            s = _dot(k_ref[pl.ds(r0, TK), :], qt_ref[:, hd * TQ:(hd + 1) * TQ])
            return jnp.where(sel, s + bias_ref[hd, _bias_tile_index(kb, qb)], NEG)
        return logits

    heads = range(DSA_HEADS)
    _attend_tiles([logits_of(hd) for hd in heads], [lambda kb: vt_ref[kb]] * DSA_HEADS, nkb, s_ref, m_ref, acc_ref)

    for hd in heads:
        o_ref[:, hd * LANES:(hd + 1) * LANES] = (acc_ref[hd, :LANES, :] / acc_ref[hd, LANES:LANES + 1, :]).T.astype(BF16)


def _dsa(sqn, skn, vt, proj, iw, bias, batch, seq):
    n = sqn.shape[0]
    nq = seq // TQ
    top_k = min(IDX_TOPK_MAX, seq // 4)
    pos_bits = int(seq).bit_length()
    iq_block = 4
    ik_block = (5 * 512 + 2 * LANES) // LANES
    return pl.pallas_call(
        functools.partial(_dsa_body, top_k=top_k, pos_bits=pos_bits),
        grid=(batch, nq),
        in_specs=[
            pl.BlockSpec((TQ, DSA_HEADS * LANES), lambda b, q: (b * nq + q, 0)),
            pl.BlockSpec((seq, LANES), lambda b, q: (b, 0)),
            pl.BlockSpec((None, None, seq // TK, VT_ROWS, TK), lambda b, q: (b, DIFF_HEADS, 0, 0, 0)),
            pl.BlockSpec((TQ, IDX_HEADS * IDX_DIM), lambda b, q: (b * nq + q, iq_block)),
            pl.BlockSpec((seq, LANES), lambda b, q: (b, ik_block)),
            pl.BlockSpec((TQ, LANES), lambda b, q: (b * nq + q, 0)),
            pl.BlockSpec((DSA_HEADS, 3, TK, TQ), lambda b, q: (1, 0, 0, 0)),
        ],
        out_specs=pl.BlockSpec((TQ, DSA_HEADS * LANES), lambda b, q: (b * nq + q, 0)),
        out_shape=jax.ShapeDtypeStruct((n, DSA_HEADS * LANES), BF16),
        scratch_shapes=[
            pltpu.VMEM((LANES, DSA_HEADS * TQ), BF16),
            pltpu.VMEM((IDX_DIM, IDX_HEADS * TQ), BF16),
            pltpu.VMEM((IDX_HEADS, TQ), F32),
            pltpu.VMEM((seq // TK, TK, TQ), I32),
            pltpu.VMEM((seq // TK, TK, TQ), I16),
            pltpu.VMEM((seq // TK, TK, TQ), I16),
            pltpu.VMEM((1, TQ), I32),
            pltpu.VMEM((DSA_HEADS, 1, TQ), F32),
            pltpu.VMEM((DSA_HEADS, VT_ROWS, TQ), F32),
            pltpu.VMEM((DSA_HEADS, 2, TK, TQ), F32),
        ],
        compiler_params=_cparams(2),
        name="dsa",
    )(sqn, skn, vt, proj, proj, iw, bias)


def _rel_bucket(dist):
    exact = REL_BUCKETS // 2
    n = jnp.maximum(dist, 0)
    nf = jnp.maximum(n, exact).astype(F32)
    far = exact + (jnp.log(nf / exact) / math.log(REL_MAX_DIST / exact) * (REL_BUCKETS - exact)).astype(I32)
    return jnp.where(n < exact, n, jnp.minimum(far, REL_BUCKETS - 1))


def _bias_tiles_body(tab_ref, bucket_ref, o_ref):
    hd = pl.program_id(0)
    nh = DIFF_HEADS + DSA_HEADS
    key = lax.broadcasted_iota(I32, (TK, TQ), 0)
    qry = lax.broadcasted_iota(I32, (TK, TQ), 1)
    for tile in range(3):
        bucket = bucket_ref[tile]
        acc = jnp.zeros(bucket.shape, F32)
        for b in range(REL_BUCKETS):
            acc = jnp.where(bucket == b, tab_ref[b * nh + hd], acc)
        if tile == 2:
            acc = jnp.where(key <= qry, acc, NEG)
        o_ref[tile] = acc * LOG2E


def _bias_tables(rel_bias):
    nh = DIFF_HEADS + DSA_HEADS
    assert TK + 1 >= REL_MAX_DIST
    j = jnp.arange(TK, dtype=I32)[:, None]
    i = jnp.arange(TQ, dtype=I32)[None, :]
    dist = jnp.stack([jnp.full((TK, TQ), 2 * TK, I32), i - j + TK, i - j])
    return pl.pallas_call(
        _bias_tiles_body,
        grid=(nh,),
        in_specs=[pl.BlockSpec(memory_space=pltpu.SMEM), pl.BlockSpec((3, TK, TQ), lambda h: (0, 0, 0))],
        out_specs=pl.BlockSpec((None, 3, TK, TQ), lambda h: (h, 0, 0, 0)),
        out_shape=jax.ShapeDtypeStruct((nh, 3, TK, TQ), F32),
        compiler_params=_cparams(1),
        name="bias_tiles",
    )(rel_bias.astype(F32).reshape(-1), _rel_bucket(dist))


def _block_diag_ones(width, group):
    r = np.arange(width)
    return jnp.asarray((r[:, None] // group) == (r[None, :] // group), dtype=BF16)


def _pad_lanes(w, width=LANES):
    return jnp.pad(w, ((0, 0), (0, width - w.shape[1])))


def kernel(x, rel_bias, mix_norm_g, ev_w_in, ev_w_out, gmlp_w_s, gmlp_b_s, gdn_conv_w, gdn_a_log, gdn_dt_bias,
           gdn_norm_g, od_w_in, od_w_out, diff_q_norm_g, diff_k_norm_g, diff_lambda, diff_sub_norm_g,
           dsa_q_norm_g, dsa_k_norm_g, ffn_norm_g, ffn_w_up, ffn_conv_w, ffn_conv_b, ffn_w_down):
    batch, seq, d = x.shape
    n = batch * seq
    depth = mix_norm_g.shape[0]
    assert d == D_MODEL and seq % max(TM_PROJ, TL_PREP, TL_GDN, TQ) == 0
    h = x.reshape(n, d)
    bias_tiles = _bias_tables(rel_bias)
    bd64 = _block_diag_ones(512, DIFF_QK)
    bd128 = _block_diag_ones(512, DSA_DIM)
    gw = GMLP_GROUPS * LANES
    qkv_w = 3 * GDN_HEADS * GDN_DIM

    for layer in range(depth):
        j = layer // 2
        g_mix = mix_norm_g[layer].reshape(1, d)
        if layer % 2 == 0:
            w = ev_w_in[j]
            o_u, o_v, o_qkv = 0, gw, 2 * gw
            o_b = o_qkv + qkv_w
            o_a = o_b + GDN_HEADS
            o_z = o_a + GDN_HEADS
            w_main = jnp.concatenate(
                [w[:, o_u:o_v], w[:, o_v:o_qkv], w[:, o_z:o_z + GDN_HEADS * GDN_DIM], w[:, o_qkv:o_b]],
                axis=1).astype(BF16)
            w_gate = _pad_lanes(w[:, o_b:o_z]).astype(BF16)
            proj, gates = _in_proj(h, g_mix, w_main, w_gate)
            gp = jnp.stack([
                _pad_lanes(jnp.concatenate([jnp.zeros((GDN_HEADS,), F32), gdn_a_log[j]])[None])[0],
                _pad_lanes(jnp.concatenate([jnp.zeros((GDN_HEADS,), F32), gdn_dt_bias[j]])[None])[0]])
            y_a, qkvn, gates2 = _even_prep(proj, gates, gmlp_w_s[j], gmlp_b_s[j].T, gdn_conv_w[j], gp, batch, seq)
            y_b = _gdn(qkvn, gates2, proj, gdn_norm_g[j].reshape(1, GDN_DIM), batch, seq)
            y1, y2, w_out = y_a, y_b, ev_w_out[j]
        else:
            lambda_init = 0.8 - 0.6 * math.exp(-0.3 * layer)
            w = od_w_in[j]
            c = np.cumsum([0, 512, 512, 512, 512, 128, 128, 512, 64, 8])
            dq, dk, dv, sq, sk, sv, iq, ik, iw = [w[:, c[i]:c[i + 1]] for i in range(9)]
            w_main = jnp.concatenate([dq, dk, dv, sq, iq, sk, sv, ik, ik], axis=1).astype(BF16)
            w_gate = _pad_lanes(iw).astype(BF16)
            proj, iw_out = _in_proj(h, g_mix, w_main, w_gate)
            dqn, dkn, sqn, skn, vt = _odd_prep(
                proj,
                jnp.tile(diff_q_norm_g[j], 2 * DIFF_HEADS)[None], jnp.tile(diff_k_norm_g[j], 2 * DIFF_HEADS)[None],
                jnp.tile(dsa_q_norm_g[j], DSA_HEADS)[None], dsa_k_norm_g[j][None], bd64, bd128, batch, seq)
            y_c = _diff_attn(dqn, dkn, vt, bias_tiles, diff_lambda[j], diff_sub_norm_g[j][None],
                             batch, seq, lambda_init)
            y_d = _dsa(sqn, skn, vt, proj, iw_out, bias_tiles, batch, seq)
            y1, y2, w_out = y_c, y_d, od_w_out[j]
        h = _ffn(h, y1, y2, w_out.astype(BF16), ffn_norm_g[layer].reshape(1, d), ffn_w_up[layer].astype(BF16),
                 ffn_conv_w[layer], ffn_conv_b[layer].reshape(1, 2 * D_FF), ffn_w_down[layer].astype(BF16), seq)
    return h.reshape(batch, seq, d)
```

```python
import functools
import math

import numpy as np
import jax
import jax.numpy as jnp
from jax import lax
from jax.experimental import pallas as pl
from jax.experimental.pallas import tpu as pltpu

F32 = jnp.float32
BF16 = jnp.bfloat16
I32 = jnp.int32
I16 = jnp.int16
HIGHEST = lax.Precision.HIGHEST

D_MODEL = 1024
GMLP_GROUPS = 4
GMLP_CHUNK = 128
GDN_HEADS = 4
GDN_DIM = 128
GDN_CHUNK = 64
GDN_CONV = 4
DIFF_HEADS = 4
DIFF_QK = 64
DSA_HEADS = 4
DSA_DIM = 128
IDX_HEADS = 8
IDX_DIM = 64
IDX_TOPK_MAX = 256
REL_BUCKETS = 32
REL_MAX_DIST = 128
D_FF = 2816
FFN_CONV = 3
EPS = 1e-6

LANES = 128
SUBLANES_F32 = 8
SUBLANES_BF16 = 16
VMEM_LIMIT = 56 * 1024 * 1024
VT_ROWS = LANES + SUBLANES_BF16

TM_PROJ = 512
TM_FFN = 512
TL_PREP = 512
TL_GDN = 512
GDN_SEQS_PER_STEP = 4
GDN_PREP_PROBLEMS = 32
TQ = 256
TK = 256
FF_CHUNK = 256
NEG = -1e30
LOG2E = 1.0 / math.log(2.0)
INT_MIN = -2 ** 31


def _cparams(n_axes):
    return pltpu.CompilerParams(dimension_semantics=("arbitrary",) * n_axes, vmem_limit_bytes=VMEM_LIMIT)


def _dot(a, b, **kw):
    return jnp.dot(a, b, preferred_element_type=F32, **kw)


def _dot_nt(a, b, **kw):
    return lax.dot_general(a, b, (((1,), (1,)), ((), ())), preferred_element_type=F32, **kw)


def _dot_tn(a, b, **kw):
    return lax.dot_general(a, b, (((0,), (0,)), ((), ())), preferred_element_type=F32, **kw)


def _rms(x, g):
    return x * lax.rsqrt(jnp.mean(x * x, axis=-1, keepdims=True) + EPS) * g


def _sigmoid(x):
    return 1.0 / (1.0 + jnp.exp2(x * -LOG2E))


def _silu(x):
    return x / (1.0 + jnp.exp2(x * -LOG2E))


def _gelu_tanh(x):
    k0 = -2.0 * math.sqrt(2.0 / math.pi) * LOG2E
    return x / (1.0 + jnp.exp2(x * (k0 + (k0 * 0.044715) * (x * x))))


def _softplus(x):
    return jnp.maximum(x, 0.0) + jnp.log(1.0 + jnp.exp(-jnp.abs(x)))


def _in_proj_body(h_ref, g_ref, w_ref, wg_ref, o_ref, og_ref, *, nout):
    xn = _rms(h_ref[...], g_ref[...]).astype(BF16)
    for c in range(0, nout, 512):
        e = min(c + 512, nout)
        o_ref[:, c:e] = _dot(xn, w_ref[:, c:e]).astype(BF16)
    og_ref[...] = _dot(xn, wg_ref[...])


def _in_proj(h, g, w, wg):
    n, d = h.shape
    nout = w.shape[1]
    return pl.pallas_call(
        functools.partial(_in_proj_body, nout=nout),
        grid=(n // TM_PROJ,),
        in_specs=[
            pl.BlockSpec((TM_PROJ, d), lambda i: (i, 0)),
            pl.BlockSpec((1, d), lambda i: (0, 0)),
            pl.BlockSpec((d, nout), lambda i: (0, 0)),
            pl.BlockSpec((d, LANES), lambda i: (0, 0)),
        ],
        out_specs=[
            pl.BlockSpec((TM_PROJ, nout), lambda i: (i, 0)),
            pl.BlockSpec((TM_PROJ, LANES), lambda i: (i, 0)),
        ],
        out_shape=[jax.ShapeDtypeStruct((n, nout), BF16), jax.ShapeDtypeStruct((n, LANES), F32)],
        compiler_params=_cparams(1),
        name="in_proj",
    )(h, g, w, wg)


def _ffn_body(h_ref, halo_ref, y1_ref, y1h_ref, y2_ref, y2h_ref, wo_ref, g_ref, wup_ref, cw_ref, cb_ref, wdn_ref,
              o_ref, xn_ref, act_ref, *, tiles_per_seq):
    tm = h_ref.shape[0]
    i = pl.program_id(0)
    half = y1_ref.shape[1]

    def mixer_residual(hh, y1, y2):
        return hh + _dot(y1, wo_ref[:half, :]) + _dot(y2, wo_ref[half:, :])

    h = mixer_residual(h_ref[...], y1_ref[...], y2_ref[...])
    g = g_ref[...]
    hb = SUBLANES_BF16
    xn_ref[hb:, :] = _rms(h, g).astype(BF16)
    keep = jnp.where(i % tiles_per_seq == 0, 0.0, 1.0)
    halo = mixer_residual(halo_ref[...], y1h_ref[...], y2h_ref[...])
    xn_ref[:hb, :] = (_rms(halo, g) * keep).astype(BF16)
    xn = xn_ref[...]

    def up_pair(c):
        return [_dot(xn, wup_ref[:, base:base + FF_CHUNK]) for base in (c, D_FF + c)]

    def conv(up, base):
        cw = cw_ref[:, base:base + FF_CHUNK]
        return (cw[0:1] * up[hb - 2:tm + hb - 2] + cw[1:2] * up[hb - 1:tm + hb - 1] + cw[2:3] * up[hb:tm + hb]
                + cb_ref[:, base:base + FF_CHUNK])

    chunks = list(range(0, D_FF, FF_CHUNK))
    ups = up_pair(chunks[0])
    for n, c in enumerate(chunks):
        nxt = up_pair(chunks[n + 1]) if n + 1 < len(chunks) else None
        act_ref[:, c:c + FF_CHUNK] = (_silu(conv(ups[0], c)) * conv(ups[1], D_FF + c)).astype(BF16)
        ups = nxt
    o_ref[...] = h + _dot(act_ref[...], wdn_ref[...])


def _ffn(h, y1, y2, wo, g, wup, cw, cb, wdn, seq):
    n, d = h.shape
    half = y1.shape[1]
    tiles_per_seq = seq // TM_FFN
    halo_blocks = TM_FFN // SUBLANES_BF16
    const = dict(pipeline_mode=pl.Buffered(1))
    tile = lambda i: (i, 0)
    halo = lambda i: (jnp.maximum(i * halo_blocks - 1, 0), 0)
    return pl.pallas_call(
        functools.partial(_ffn_body, tiles_per_seq=tiles_per_seq),
        grid=(n // TM_FFN,),
        in_specs=[
            pl.BlockSpec((TM_FFN, d), tile),
            pl.BlockSpec((SUBLANES_BF16, d), halo),
            pl.BlockSpec((TM_FFN, half), tile),
            pl.BlockSpec((SUBLANES_BF16, half), halo),
            pl.BlockSpec((TM_FFN, half), tile),
            pl.BlockSpec((SUBLANES_BF16, half), halo),
            pl.BlockSpec((2 * half, d), lambda i: (0, 0), **const),
            pl.BlockSpec((1, d), lambda i: (0, 0)),
            pl.BlockSpec((d, 2 * D_FF), lambda i: (0, 0), **const),
            pl.BlockSpec((FFN_CONV, 2 * D_FF), lambda i: (0, 0)),
            pl.BlockSpec((1, 2 * D_FF), lambda i: (0, 0)),
            pl.BlockSpec((D_FF, d), lambda i: (0, 0), **const),
        ],
        out_specs=pl.BlockSpec((TM_FFN, d), lambda i: (i, 0)),
        out_shape=jax.ShapeDtypeStruct((n, d), F32),
        scratch_shapes=[pltpu.VMEM((TM_FFN + SUBLANES_BF16, d), BF16), pltpu.VMEM((TM_FFN, D_FF), BF16)],
        compiler_params=_cparams(1),
        name="ffn",
    )(h, h, y1, y1, y2, y2, wo, g, wup, cw, cb, wdn)


def _even_prep_body(uv_ref, qkv_ref, halo_ref, gate_ref, ws_ref, bs_ref, cw_ref, gp_ref,
                    ya_ref, qkvo_ref, gout_ref, xs_ref):
    tl = uv_ref.shape[0]
    t = pl.program_id(1)
    gw = LANES
    row = lax.broadcasted_iota(I32, (GMLP_CHUNK, GMLP_CHUNK), 0)
    col = lax.broadcasted_iota(I32, (GMLP_CHUNK, GMLP_CHUNK), 1)
    for gi in range(GMLP_GROUPS):
        u = _gelu_tanh(uv_ref[:, gi * gw:(gi + 1) * gw].astype(F32))
        v = _gelu_tanh(uv_ref[:, (GMLP_GROUPS + gi) * gw:(GMLP_GROUPS + gi + 1) * gw].astype(F32))
        vc = v - jnp.mean(v, axis=-1, keepdims=True)
        vn = (vc * lax.rsqrt(jnp.mean(vc * vc, axis=-1, keepdims=True) + EPS)).astype(BF16)
        w = jnp.where(col <= row, ws_ref[gi], 0.0).astype(BF16)
        b = bs_ref[:, gi:gi + 1]
        for c in range(tl // GMLP_CHUNK):
            r0 = c * GMLP_CHUNK
            mixed = _dot(w, vn[r0:r0 + GMLP_CHUNK]) + b
            ya_ref[r0:r0 + GMLP_CHUNK, gi * gw:(gi + 1) * gw] = (u[r0:r0 + GMLP_CHUNK] * mixed).astype(BF16)
    keep = jnp.where(t == 0, 0.0, 1.0)
    hb = SUBLANES_BF16
    for j in range(3 * GDN_HEADS):
        sl = slice(j * gw, (j + 1) * gw)
        xs_ref[:hb, :] = halo_ref[:, sl].astype(F32) * keep
        xs_ref[hb:, :] = qkv_ref[:, sl].astype(F32)
        cw = cw_ref[:, sl]
        y = cw[0:1] * xs_ref[hb - 3:tl + hb - 3, :]
        for k in range(1, GDN_CONV):
            y = y + cw[k:k + 1] * xs_ref[hb - 3 + k:tl + hb - 3 + k, :]
        y = _silu(y)
        if j < 2 * GDN_HEADS:
            y = y * lax.rsqrt(jnp.sum(y * y, axis=-1, keepdims=True) + EPS)
        if j < GDN_HEADS:
            y = y * (GDN_DIM ** -0.5)
        qkvo_ref[:, sl] = y.astype(BF16)
    x = gate_ref[...]
    lane = lax.broadcasted_iota(I32, x.shape, 1)
    beta = _sigmoid(x)
    gdec = -jnp.exp(gp_ref[0:1, :]) * _softplus(x + gp_ref[1:2, :])
    gout_ref[...] = jnp.where(lane < GDN_HEADS, beta, gdec)


def _even_prep(proj, gates, ws, bs_t, cw, gp, batch, seq):
    n = proj.shape[0]
    nt = seq // TL_PREP
    halo_blocks = TL_PREP // SUBLANES_BF16
    qkv_w = 3 * GDN_HEADS * GDN_DIM
    row = lambda b, t: b * nt + t
    return pl.pallas_call(
        _even_prep_body,
        grid=(batch, nt),
        in_specs=[
            pl.BlockSpec((TL_PREP, 2 * GMLP_GROUPS * LANES), lambda b, t: (row(b, t), 0)),
            pl.BlockSpec((TL_PREP, qkv_w), lambda b, t: (row(b, t), 1)),
            pl.BlockSpec((SUBLANES_BF16, qkv_w), lambda b, t: (jnp.maximum(row(b, t) * halo_blocks - 1, 0), 1)),
            pl.BlockSpec((TL_PREP, LANES), lambda b, t: (row(b, t), 0)),
            pl.BlockSpec((GMLP_GROUPS, GMLP_CHUNK, GMLP_CHUNK), lambda b, t: (0, 0, 0)),
            pl.BlockSpec((GMLP_CHUNK, GMLP_GROUPS), lambda b, t: (0, 0)),
            pl.BlockSpec((GDN_CONV, qkv_w), lambda b, t: (0, 0)),
            pl.BlockSpec((2, LANES), lambda b, t: (0, 0)),
        ],
        out_specs=[
            pl.BlockSpec((TL_PREP, GMLP_GROUPS * LANES), lambda b, t: (row(b, t), 0)),
            pl.BlockSpec((TL_PREP, qkv_w), lambda b, t: (row(b, t), 0)),
            pl.BlockSpec((TL_PREP, LANES), lambda b, t: (row(b, t), 0)),
        ],
        out_shape=[
            jax.ShapeDtypeStruct((n, GMLP_GROUPS * LANES), BF16),
            jax.ShapeDtypeStruct((n, qkv_w), BF16),
            jax.ShapeDtypeStruct((n, LANES), F32),
        ],
        scratch_shapes=[pltpu.VMEM((TL_PREP + SUBLANES_BF16, LANES), F32)],
        compiler_params=_cparams(2),
        name="even_prep",
    )(proj, proj, proj, gates, ws, bs_t, cw, gp)


def _split_bf16(x):
    hi = x.astype(BF16)
    return hi, (x - hi.astype(F32)).astype(BF16)


def _mm3(a, b):
    ah, al = _split_bf16(a)
    bh, bl = _split_bf16(b)
    return _dot(ah, bh) + (_dot(ah, bl) + _dot(al, bh))


def _unit_lower_inverses(lows):
    c = lows[0].shape[0]
    row = lax.broadcasted_iota(I32, (c, c), 0)
    col = lax.broadcasted_iota(I32, (c, c), 1)
    eye = jnp.where(row == col, 1.0, 0.0)
    same16 = jnp.right_shift(row, 4) == jnp.right_shift(col, 4)
    same32 = jnp.right_shift(row, 5) == jnp.right_shift(col, 5)
    mm1 = lambda a, b: _dot(a.astype(BF16), b.astype(BF16))
    ps = [jnp.where(same16, -low, 0.0) for low in lows]
    xs = [eye + p for p in ps]
    for mm in (_mm3, mm1, mm1):
        ps = [mm(p, p) for p in ps]
        xs = [x + mm(x, p) for x, p in zip(xs, ps)]
    mid = jnp.logical_and(same32, jnp.logical_not(same16))
    ys = [mm1(jnp.where(mid, low, 0.0), x) for low, x in zip(lows, xs)]
    xs = [x - mm1(x, y) for x, y in zip(xs, ys)]
    ys = [mm1(jnp.where(same32, 0.0, low), x) for low, x in zip(lows, xs)]
    return [x - mm1(x, y) for x, y in zip(xs, ys)]


def _gdn_body(qkv_ref, gate_ref, z_ref, gn_ref, o_ref,
              s_ref, u_ref, w_ref, qg_ref, kdt_ref, intra_ref, egl_ref):
    nseq, tl = qkv_ref.shape[0], qkv_ref.shape[1]
    c = GDN_CHUNK
    d = GDN_DIM
    lanes = [(bi, hd) for bi in range(nseq) for hd in range(GDN_HEADS)]
    flat = lambda bi, hd: bi * GDN_HEADS + hd
    prep_chunks = max(1, GDN_PREP_PROBLEMS // len(lanes))
    t = pl.program_id(1)

    @pl.when(t == 0)
    def _():
        s_ref[...] = jnp.zeros_like(s_ref)

    row = lax.broadcasted_iota(I32, (c, c), 0)
    col = lax.broadcasted_iota(I32, (c, c), 1)
    incl = col <= row
    strict = col < row
    lmat = jnp.where(incl, 1.0, 0.0)
    gn = gn_ref[...]

    def prepare(it, carry):
        cis = [it * prep_chunks + j for j in range(prep_chunks)]
        rows = [pl.ds(pl.multiple_of(ci * c, c), c) for ci in cis]
        gates = {(j, bi): gate_ref[bi, rows[j], :] for j in range(prep_chunks) for bi in range(nseq)}
        gcum = {key: _dot(lmat, g, precision=HIGHEST) for key, g in gates.items()}
        pairs = [(j, bi, hd) for j in range(prep_chunks) for bi, hd in lanes]
        qs = [qkv_ref[bi, rows[j], hd * d:(hd + 1) * d] for j, bi, hd in pairs]
        ks = [qkv_ref[bi, rows[j], (GDN_HEADS + hd) * d:(GDN_HEADS + hd + 1) * d] for j, bi, hd in pairs]
        vs = [qkv_ref[bi, rows[j], (2 * GDN_HEADS + hd) * d:(2 * GDN_HEADS + hd + 1) * d] for j, bi, hd in pairs]
        beta = [jnp.broadcast_to(gates[j, bi][:, hd:hd + 1], (c, d)) for j, bi, hd in pairs]
        gc = [jnp.broadcast_to(gcum[j, bi][:, GDN_HEADS + hd:GDN_HEADS + hd + 1], (c, d)) for j, bi, hd in pairs]
        decay = [jnp.where(incl, jnp.exp(jnp.where(incl, g[:, :c] - g.T[:c, :], 0.0)), 0.0) for g in gc]
        kb = [k.astype(F32) * b for k, b in zip(ks, beta)]
        lows = [jnp.where(strict, _dot_nt(x.astype(BF16), k) * dc, 0.0) for x, k, dc in zip(kb, ks, decay)]
        tinv = [x.astype(BF16) for x in _unit_lower_inverses(lows)]
        egc = [jnp.exp(g) for g in gc]
        rhs = [jnp.concatenate([(v.astype(F32) * b).astype(BF16), (x * e).astype(BF16)], axis=1)
               for v, b, x, e in zip(vs, beta, kb, egc)]
        uw = [_dot(ti, r) for ti, r in zip(tinv, rhs)]
        qk = [_dot_nt(q, k) for q, k in zip(qs, ks)]
        for n, (j, bi, hd) in enumerate(pairs):
            ci, f = cis[j], flat(bi, hd)
            u_ref[ci, f] = uw[n][:, :d]
            w_ref[ci, f] = uw[n][:, d:].astype(BF16)
            intra_ref[ci, f] = jnp.where(incl, qk[n] * decay[n], 0.0).astype(BF16)
            qg_ref[ci, f] = (qs[n].astype(F32) * egc[n]).astype(BF16)
            g_last = gc[n][c - 1:c, :]
            kdt_ref[ci, f] = (ks[n].astype(F32) * jnp.exp(g_last - gc[n])).T.astype(BF16)
            egl_ref[ci, f] = jnp.exp(g_last)
        return carry

    lax.fori_loop(0, tl // c // prep_chunks, prepare, 0)

    def scan(ci, carry):
        r0 = pl.multiple_of(ci * c, c)
        fs = [flat(bi, hd) for bi, hd in lanes]
        states = [s_ref[f] for f in fs]
        sb = [s.astype(BF16) for s in states]
        ws = [_dot(w_ref[ci, f], s) for f, s in zip(fs, sb)]
        qsd = [_dot(qg_ref[ci, f], s) for f, s in zip(fs, sb)]
        vnb = [(u_ref[ci, f] - w).astype(BF16) for f, w in zip(fs, ws)]
        for f, s, v in zip(fs, states, vnb):
            s_ref[f] = s * egl_ref[ci, f] + _dot(kdt_ref[ci, f], v)
        for (bi, hd), f, qd, v in zip(lanes, fs, qsd, vnb):
            out = qd + _dot(intra_ref[ci, f], v)
            z = z_ref[bi, pl.ds(r0, c), hd * d:(hd + 1) * d].astype(F32)
            o_ref[bi, pl.ds(r0, c), hd * d:(hd + 1) * d] = (_rms(out, gn) * _silu(z)).astype(BF16)
        return carry

    lax.fori_loop(0, tl // c, scan, 0)


def _gdn(qkv, gates, proj, gn, batch, seq):
    n = qkv.shape[0]
    nt = seq // TL_GDN
    nc = TL_GDN // GDN_CHUNK
    width = GDN_HEADS * GDN_DIM
    nseq = GDN_SEQS_PER_STEP if batch % GDN_SEQS_PER_STEP == 0 else 1
    nl = nseq * GDN_HEADS
    as3d = lambda a: a.reshape(batch, seq, a.shape[-1])
    out = pl.pallas_call(
        _gdn_body,
        grid=(batch // nseq, nt),
        in_specs=[
            pl.BlockSpec((nseq, TL_GDN, 3 * width), lambda b, t: (b, t, 0)),
            pl.BlockSpec((nseq, TL_GDN, LANES), lambda b, t: (b, t, 0)),
            pl.BlockSpec((nseq, TL_GDN, width), lambda b, t: (b, t, 2)),
            pl.BlockSpec((1, GDN_DIM), lambda b, t: (0, 0)),
        ],
        out_specs=pl.BlockSpec((nseq, TL_GDN, width), lambda b, t: (b, t, 0)),
        out_shape=jax.ShapeDtypeStruct((batch, seq, width), BF16),
        scratch_shapes=[
            pltpu.VMEM((nl, GDN_DIM, GDN_DIM), F32),
            pltpu.VMEM((nc, nl, GDN_CHUNK, GDN_DIM), F32),
            pltpu.VMEM((nc, nl, GDN_CHUNK, GDN_DIM), BF16),
            pltpu.VMEM((nc, nl, GDN_CHUNK, GDN_DIM), BF16),
            pltpu.VMEM((nc, nl, GDN_DIM, GDN_CHUNK), BF16),
            pltpu.VMEM((nc, nl, GDN_CHUNK, GDN_CHUNK), BF16),
            pltpu.VMEM((nc, nl, 1, GDN_DIM), F32),
        ],
        compiler_params=_cparams(2),
        name="gdn",
    )(as3d(qkv), as3d(gates), as3d(proj), gn)
    return out.reshape(n, width)


def _group_mean_sq(x, ones_bd, group):
    x2 = x * x
    hi = x2.astype(BF16)
    lo = (x2 - hi.astype(F32)).astype(BF16)
    return (_dot(hi, ones_bd) + _dot(lo, ones_bd)) * (1.0 / group)


def _odd_prep_body(dq_ref, dk_ref, dv_ref, sq_ref, sk_ref, sv_ref, gq_ref, gk_ref, gsq_ref, gsk_ref, bd64_ref,
                   bd128_ref, dqo_ref, dko_ref, sqo_ref, sko_ref, vt_ref):
    ones = jnp.ones((VT_ROWS - LANES, TK), BF16)
    for j in range(DIFF_HEADS + 1):
        src = sv_ref[...] if j == DIFF_HEADS else dv_ref[:, j * LANES:(j + 1) * LANES]
        xt = src.astype(F32).T
        for c in range(xt.shape[1] // TK):
            vt_ref[j, c, :LANES, :] = xt[:, c * TK:(c + 1) * TK].astype(BF16)
            vt_ref[j, c, LANES:, :] = ones
    bd64 = bd64_ref[...]
    x = dq_ref[...].astype(F32)
    dqo_ref[...] = (x * lax.rsqrt(_group_mean_sq(x, bd64, DIFF_QK) + EPS) * gq_ref[...]
                    * (DIFF_QK ** -0.5 * LOG2E)).astype(BF16)
    x = dk_ref[...].astype(F32)
    dko_ref[...] = (x * lax.rsqrt(_group_mean_sq(x, bd64, DIFF_QK) + EPS) * gk_ref[...]).astype(BF16)
    x = sq_ref[...].astype(F32)
    sqo_ref[...] = (x * lax.rsqrt(_group_mean_sq(x, bd128_ref[...], DSA_DIM) + EPS) * gsq_ref[...]
                    * (DSA_DIM ** -0.5 * LOG2E)).astype(BF16)
    x = sk_ref[...].astype(F32)
    sko_ref[...] = _rms(x, gsk_ref[...]).astype(BF16)


def _odd_prep(proj, gq, gk, gsq, gsk, bd64, bd128, batch, seq):
    n = proj.shape[0]
    w = 512
    nt = seq // TL_PREP
    kt = TL_PREP // TK
    full = lambda shape: pl.BlockSpec(shape, lambda b, t: (0,) * len(shape))
    row = lambda b, t: b * nt + t
    return pl.pallas_call(
        _odd_prep_body,
        grid=(batch, nt),
        in_specs=[
            pl.BlockSpec((TL_PREP, w), lambda b, t: (row(b, t), 0)),
            pl.BlockSpec((TL_PREP, w), lambda b, t: (row(b, t), 1)),
            pl.BlockSpec((TL_PREP, w), lambda b, t: (row(b, t), 2)),
            pl.BlockSpec((TL_PREP, w), lambda b, t: (row(b, t), 3)),
            pl.BlockSpec((TL_PREP, LANES), lambda b, t: (row(b, t), 5 * w // LANES)),
            pl.BlockSpec((TL_PREP, LANES), lambda b, t: (row(b, t), 5 * w // LANES + 1)),
            full((1, w)), full((1, w)), full((1, w)), full((1, LANES)), full((w, w)), full((w, w)),
        ],
        out_specs=[
            pl.BlockSpec((TL_PREP, w), lambda b, t: (row(b, t), 0)),
            pl.BlockSpec((TL_PREP, w), lambda b, t: (row(b, t), 0)),
            pl.BlockSpec((TL_PREP, w), lambda b, t: (row(b, t), 0)),
            pl.BlockSpec((TL_PREP, LANES), lambda b, t: (row(b, t), 0)),
            pl.BlockSpec((None, DIFF_HEADS + 1, kt, VT_ROWS, TK), lambda b, t: (b, 0, t, 0, 0)),
        ],
        out_shape=[
            jax.ShapeDtypeStruct((n, w), BF16), jax.ShapeDtypeStruct((n, w), BF16),
            jax.ShapeDtypeStruct((n, w), BF16), jax.ShapeDtypeStruct((n, LANES), BF16),
            jax.ShapeDtypeStruct((batch, DIFF_HEADS + 1, seq // TK, VT_ROWS, TK), BF16),
        ],
        compiler_params=_cparams(2),
        name="odd_prep",
    )(proj, proj, proj, proj, proj, proj, gq, gk, gsq, gsk, bd64, bd128)


def _bias_tile_index(kb, qb):
    return jnp.clip(kb - (qb - 2), 0, 2)


def _softmax_step_t(s_t, vt, m_ref, acc_ref):
    for g in range(s_t.shape[1] // LANES):
        sl = slice(g * LANES, (g + 1) * LANES)
        s = s_t[:, sl]
        m_prev = m_ref[:, sl]
        m_new = jnp.maximum(m_prev, jnp.max(s, axis=0, keepdims=True))
        alpha = jnp.exp2(m_prev - m_new)
        p = jnp.exp2(s - m_new)
        acc_ref[:, sl] = alpha * acc_ref[:, sl] + _dot(vt, p.astype(BF16))
        m_ref[:, sl] = m_new


def _attend_tiles(logits, values, nblk, s_ref, m_ref, acc_ref):
    streams = range(len(logits))
    m_ref[...] = jnp.full_like(m_ref, NEG)
    acc_ref[...] = jnp.zeros_like(acc_ref)
    for i in streams:
        s_ref[i, 0] = logits[i](0)

    def pair(it, carry):
        kb = 2 * it
        for i in streams:
            s_ref[i, 1] = logits[i](kb + 1)
        for i in streams:
            _softmax_step_t(s_ref[i, 0], values[i](kb), m_ref.at[i], acc_ref.at[i])
        for i in streams:
            s_ref[i, 0] = logits[i](jnp.minimum(kb + 2, nblk - 1))
        for i in streams:
            _softmax_step_t(s_ref[i, 1], values[i](kb + 1), m_ref.at[i], acc_ref.at[i])
        return carry

    lax.fori_loop(0, nblk // 2, pair, 0)

    @pl.when(nblk % 2 == 1)
    def _():
        for i in streams:
            _softmax_step_t(s_ref[i, 0], values[i](nblk - 1), m_ref.at[i], acc_ref.at[i])


def _diff_attn_body(q_ref, k_ref, vt_ref, bias_ref, lam_ref, sg_ref, o_ref,
                    qt_ref, m_ref, acc_ref, s_ref, *, lambda_init):
    qi = pl.program_id(1)
    nblk = qi + 1
    heads = range(DIFF_HEADS)
    for hd in heads:
        qt = q_ref[:, hd * LANES:(hd + 1) * LANES].astype(F32).T
        dim = lax.broadcasted_iota(I32, qt.shape, 0)
        qt_ref[hd, :, :TQ] = jnp.where(dim < DIFF_QK, qt, 0.0).astype(BF16)
        qt_ref[hd, :, TQ:] = jnp.where(dim >= DIFF_QK, qt, 0.0).astype(BF16)

    def logits_of(hd):
        def logits(kb):
            r0 = pl.multiple_of(kb * TK, TK)
            b = bias_ref[hd, _bias_tile_index(kb, qi)]
            s = _dot(k_ref[pl.ds(r0, TK), hd * LANES:(hd + 1) * LANES], qt_ref[hd])
            return s + jnp.concatenate([b, b], axis=1)
        return logits

    _attend_tiles([logits_of(hd) for hd in heads], [lambda kb, hd=hd: vt_ref[hd, kb] for hd in heads],
                  nblk, s_ref, m_ref, acc_ref)

    lf = lam_ref[...]
    lam = (jnp.exp(jnp.sum(lf[0:1] * lf[1:2], axis=-1, keepdims=True))
           - jnp.exp(jnp.sum(lf[2:3] * lf[3:4], axis=-1, keepdims=True)) + lambda_init)
    for hd in heads:
        den = acc_ref[hd, LANES:LANES + 1, :]
        ot = (acc_ref[hd, :LANES, :TQ] / den[:, :TQ] - lam * (acc_ref[hd, :LANES, TQ:] / den[:, TQ:]))
        ot = ot * lax.rsqrt(jnp.mean(ot * ot, axis=0, keepdims=True) + EPS)
        o_ref[:, hd * LANES:(hd + 1) * LANES] = (ot.T * sg_ref[...] * (1.0 - lambda_init)).astype(BF16)


def _diff_attn(dqn, dkn, vt, bias, lam_p, sub_g, batch, seq, lambda_init):
    n = dqn.shape[0]
    nq = seq // TQ
    width = DIFF_HEADS * LANES
    return pl.pallas_call(
        functools.partial(_diff_attn_body, lambda_init=lambda_init),
        grid=(batch, nq),
        in_specs=[
            pl.BlockSpec((TQ, width), lambda b, q: (b * nq + q, 0)),
            pl.BlockSpec((seq, width), lambda b, q: (b, 0)),
            pl.BlockSpec((None, DIFF_HEADS + 1, seq // TK, VT_ROWS, TK), lambda b, q: (b, 0, 0, 0, 0)),
            pl.BlockSpec((DIFF_HEADS, 3, TK, TQ), lambda b, q: (0, 0, 0, 0)),
            pl.BlockSpec((4, DIFF_QK), lambda b, q: (0, 0)),
            pl.BlockSpec((1, LANES), lambda b, q: (0, 0)),
        ],
        out_specs=pl.BlockSpec((TQ, width), lambda b, q: (b * nq + q, 0)),
        out_shape=jax.ShapeDtypeStruct((n, width), BF16),
        scratch_shapes=[
            pltpu.VMEM((DIFF_HEADS, LANES, 2 * TQ), BF16),
            pltpu.VMEM((DIFF_HEADS, 1, 2 * TQ), F32),
            pltpu.VMEM((DIFF_HEADS, VT_ROWS, 2 * TQ), F32),
            pltpu.VMEM((DIFF_HEADS, 2, TK, 2 * TQ), F32),
        ],
        compiler_params=_cparams(2),
        name="diff_attn",
    )(dqn, dkn, vt, bias, lam_p, sub_g)


def _dsa_body(q_ref, k_ref, vt_ref, iq_ref, ik_ref, iw_ref, bias_ref, o_ref,
              qt_ref, qit_ref, wt_ref, keys_ref, hi_ref, lo_ref, t_ref, m_ref, acc_ref, s_ref, *, top_k, pos_bits):
    qb = pl.program_id(1)
    nkb = qb + 1

    for p in range(IDX_HEADS // 2):
        gt = iq_ref[:, p * LANES:(p + 1) * LANES].astype(F32).T.astype(BF16)
        qit_ref[:, (2 * p) * TQ:(2 * p + 1) * TQ] = gt[:IDX_DIM]
        qit_ref[:, (2 * p + 1) * TQ:(2 * p + 2) * TQ] = gt[IDX_DIM:]
    for hd in range(DSA_HEADS):
        qt_ref[:, hd * TQ:(hd + 1) * TQ] = q_ref[:, hd * LANES:(hd + 1) * LANES].astype(F32).T.astype(BF16)
    wscale = (IDX_HEADS ** -0.5) * (IDX_DIM ** -0.5)
    wt_ref[...] = (iw_ref[...] * wscale).T[:IDX_HEADS, :]

    key_i = lax.broadcasted_iota(I32, (TK, TQ), 0)
    qry_i = lax.broadcasted_iota(I32, (TK, TQ), 1)
    causal = key_i <= qry_i

    def index_blocks(kbs, masked):
        iks = [ik_ref[pl.ds(pl.multiple_of(kb * TK, TK), TK), :IDX_DIM] for kb in kbs]
        idxs = [jnp.zeros((TK, TQ), F32) for _ in kbs]
        for hi in range(IDX_HEADS):
            ss = [_dot(ik, qit_ref[:, hi * TQ:(hi + 1) * TQ]) for ik in iks]
            idxs = [idx + jnp.maximum(s, 0.0) * wt_ref[hi:hi + 1, :] for idx, s in zip(idxs, ss)]
        for kb, idx in zip(kbs, idxs):
            idx = jnp.where(idx == 0.0, 0.0, idx)
            bits = pltpu.bitcast(idx, I32)
            key = jnp.where(bits < 0, bits ^ jnp.int32(0x7FFFFFFF), bits)
            if masked:
                key = jnp.where(causal, key, jnp.int32(INT_MIN))
            keys_ref[kb] = key
            hi_ref[kb] = jnp.right_shift(key, 16).astype(I16)
            lo_ref[kb] = ((key & 0xFFFF) - 0x8000).astype(I16)

    def index_pair(i, carry):
        index_blocks([2 * i, 2 * i + 1], False)
        return carry

    lax.fori_loop(0, (nkb - 1) // 2, index_pair, 0)

    @pl.when((nkb - 1) % 2 == 1)
    def _():
        index_blocks([nkb - 2], False)

    index_blocks([nkb - 1], True)

    kf = float(top_k)
    rows16 = SUBLANES_BF16
    i16_min = -2 ** 15

    def count16(ref, cand, strict=False):
        c16 = jnp.broadcast_to(cand, (rows16, TQ)).astype(I16)

        def body(kb, acc):
            x = ref[kb]
            parts = []
            for i in range(TK // rows16):
                slab = x[i * rows16:(i + 1) * rows16]
                hit = slab > c16 if strict else slab >= c16
                parts.append(jnp.where(hit, jnp.int16(1), jnp.int16(0)))
            while len(parts) > 1:
                parts = [parts[i] + parts[i + 1] for i in range(0, len(parts), 2)]
            return acc + parts[0]

        acc = lax.fori_loop(0, nkb, body, jnp.zeros((rows16, TQ), I16))
        return jnp.sum(acc.astype(F32), axis=0, keepdims=True)

    def digit_select(ref, need, fallback):
        c0 = count16(ref, jnp.zeros((1, TQ), I32))
        ok0 = c0 >= need
        d0 = jnp.where(ok0, jnp.int32(0), jnp.int32(i16_min))
        n0 = jnp.where(ok0, c0, fallback)

        def bit_step(i, carry):
            d, n = carry
            cand = d | jnp.left_shift(jnp.int32(1), 14 - i)
            cnt = count16(ref, cand)
            ok = cnt >= need
            return jnp.where(ok, cand, d), jnp.where(ok, cnt, n)

        return lax.fori_loop(0, 15, bit_step, (d0, n0))

    total = (nkb * TK).astype(F32) * jnp.ones((1, TQ), F32)
    t_hi, n_hi = digit_select(hi_ref, kf, total)
    n_above = count16(hi_ref, t_hi, strict=True)
    t_hi16 = jnp.broadcast_to(t_hi, (rows16, TQ)).astype(I16)

    def restrict(kb, carry):
        for i in range(TK // rows16):
            sl = slice(i * rows16, (i + 1) * rows16)
            lo_ref[kb, sl, :] = jnp.where(hi_ref[kb, sl, :] == t_hi16, lo_ref[kb, sl, :], jnp.int16(i16_min))
        return carry

    lax.fori_loop(0, nkb, restrict, 0)
    t_lo, n_lo = digit_select(lo_ref, kf - n_above, n_hi - n_above)
    t = jnp.left_shift(t_hi, 16) + (t_lo + 0x8000)
    cge = n_above + n_lo
    t_ref[...] = t

    @pl.when(jnp.max(cge) > kf)
    def _():
        tt = t_ref[...]
        r = kf - (n_above + count16(lo_ref, t_lo, strict=True))
        off_threshold = 2 ** 15 - 1

        def mark(kb, carry):
            lo_ref[kb] = jnp.where(keys_ref[kb] == tt, key_i + kb * TK, off_threshold).astype(I16)
            return carry

        lax.fori_loop(0, nkb, mark, 0)

        def pos_step(i, pcut):
            cand = pcut | jnp.left_shift(jnp.int32(1), pos_bits - 1 - i)
            tied_before = total - count16(lo_ref, cand)
            return jnp.where(tied_before <= r, cand, pcut)

        pcut = lax.fori_loop(0, pos_bits, pos_step, jnp.zeros((1, TQ), I32))

        def demote(kb, carry):
            kk = keys_ref[kb]
            drop = jnp.logical_and(kk == tt, key_i + kb * TK >= pcut)
            keys_ref[kb] = jnp.where(drop, kk - 1, kk)
            return carry

        lax.fori_loop(0, nkb, demote, 0)

    def logits_of(hd):
        def logits(kb):
            r0 = pl.multiple_of(kb * TK, TK)
            sel = keys_ref[kb] >= t_ref[...]
            s = _dot(k_ref[pl.ds(r0, TK), :], qt_ref[:, hd * TQ:(hd + 1) * TQ])
            return jnp.where(sel, s + bias_ref[hd, _bias_tile_index(kb, qb)], NEG)
        return logits

    heads = range(DSA_HEADS)
    _attend_tiles([logits_of(hd) for hd in heads], [lambda kb: vt_ref[kb]] * DSA_HEADS, nkb, s_ref, m_ref, acc_ref)

    for hd in heads:
        o_ref[:, hd * LANES:(hd + 1) * LANES] = (acc_ref[hd, :LANES, :] / acc_ref[hd, LANES:LANES + 1, :]).T.astype(BF16)


def _dsa(sqn, skn, vt, proj, iw, bias, batch, seq):
    n = sqn.shape[0]
    nq = seq // TQ
    top_k = min(IDX_TOPK_MAX, seq // 4)
    pos_bits = int(seq).bit_length()
    iq_block = 4
    ik_block = (5 * 512 + 2 * LANES) // LANES
    return pl.pallas_call(
        functools.partial(_dsa_body, top_k=top_k, pos_bits=pos_bits),
        grid=(batch, nq),
        in_specs=[
            pl.BlockSpec((TQ, DSA_HEADS * LANES), lambda b, q: (b * nq + q, 0)),
            pl.BlockSpec((seq, LANES), lambda b, q: (b, 0)),
            pl.BlockSpec((None, None, seq // TK, VT_ROWS, TK), lambda b, q: (b, DIFF_HEADS, 0, 0, 0)),
            pl.BlockSpec((TQ, IDX_HEADS * IDX_DIM), lambda b, q: (b * nq + q, iq_block)),
            pl.BlockSpec((seq, LANES), lambda b, q: (b, ik_block)),
            pl.BlockSpec((TQ, LANES), lambda b, q: (b * nq + q, 0)),
            pl.BlockSpec((DSA_HEADS, 3, TK, TQ), lambda b, q: (1, 0, 0, 0)),
        ],
        out_specs=pl.BlockSpec((TQ, DSA_HEADS * LANES), lambda b, q: (b * nq + q, 0)),
        out_shape=jax.ShapeDtypeStruct((n, DSA_HEADS * LANES), BF16),
        scratch_shapes=[
            pltpu.VMEM((LANES, DSA_HEADS * TQ), BF16),
            pltpu.VMEM((IDX_DIM, IDX_HEADS * TQ), BF16),
            pltpu.VMEM((IDX_HEADS, TQ), F32),
            pltpu.VMEM((seq // TK, TK, TQ), I32),
            pltpu.VMEM((seq // TK, TK, TQ), I16),
            pltpu.VMEM((seq // TK, TK, TQ), I16),
            pltpu.VMEM((1, TQ), I32),
            pltpu.VMEM((DSA_HEADS, 1, TQ), F32),
            pltpu.VMEM((DSA_HEADS, VT_ROWS, TQ), F32),
            pltpu.VMEM((DSA_HEADS, 2, TK, TQ), F32),
        ],
        compiler_params=_cparams(2),
        name="dsa",
    )(sqn, skn, vt, proj, proj, iw, bias)


def _rel_bucket(dist):
    exact = REL_BUCKETS // 2
    n = jnp.maximum(dist, 0)
    nf = jnp.maximum(n, exact).astype(F32)
    far = exact + (jnp.log(nf / exact) / math.log(REL_MAX_DIST / exact) * (REL_BUCKETS - exact)).astype(I32)
    return jnp.where(n < exact, n, jnp.minimum(far, REL_BUCKETS - 1))


def _bias_tiles_body(tab_ref, bucket_ref, o_ref):
    hd = pl.program_id(0)
    nh = DIFF_HEADS + DSA_HEADS
    key = lax.broadcasted_iota(I32, (TK, TQ), 0)
    qry = lax.broadcasted_iota(I32, (TK, TQ), 1)
    for tile in range(3):
        bucket = bucket_ref[tile]
        acc = jnp.zeros(bucket.shape, F32)
        for b in range(REL_BUCKETS):
            acc = jnp.where(bucket == b, tab_ref[b * nh + hd], acc)
        if tile == 2:
            acc = jnp.where(key <= qry, acc, NEG)
        o_ref[tile] = acc * LOG2E


def _bias_tables(rel_bias):
    nh = DIFF_HEADS + DSA_HEADS
    assert TK + 1 >= REL_MAX_DIST
    j = jnp.arange(TK, dtype=I32)[:, None]
    i = jnp.arange(TQ, dtype=I32)[None, :]
    dist = jnp.stack([jnp.full((TK, TQ), 2 * TK, I32), i - j + TK, i - j])
    return pl.pallas_call(
        _bias_tiles_body,
        grid=(nh,),
        in_specs=[pl.BlockSpec(memory_space=pltpu.SMEM), pl.BlockSpec((3, TK, TQ), lambda h: (0, 0, 0))],
        out_specs=pl.BlockSpec((None, 3, TK, TQ), lambda h: (h, 0, 0, 0)),
        out_shape=jax.ShapeDtypeStruct((nh, 3, TK, TQ), F32),
        compiler_params=_cparams(1),
        name="bias_tiles",
    )(rel_bias.astype(F32).reshape(-1), _rel_bucket(dist))


def _block_diag_ones(width, group):
    r = np.arange(width)
    return jnp.asarray((r[:, None] // group) == (r[None, :] // group), dtype=BF16)


def _pad_lanes(w, width=LANES):
    return jnp.pad(w, ((0, 0), (0, width - w.shape[1])))


def kernel(x, rel_bias, mix_norm_g, ev_w_in, ev_w_out, gmlp_w_s, gmlp_b_s, gdn_conv_w, gdn_a_log, gdn_dt_bias,
           gdn_norm_g, od_w_in, od_w_out, diff_q_norm_g, diff_k_norm_g, diff_lambda, diff_sub_norm_g,
           dsa_q_norm_g, dsa_k_norm_g, ffn_norm_g, ffn_w_up, ffn_conv_w, ffn_conv_b, ffn_w_down):
    batch, seq, d = x.shape
    n = batch * seq
    depth = mix_norm_g.shape[0]
    assert d == D_MODEL and seq % max(TM_PROJ, TL_PREP, TL_GDN, TQ) == 0
    h = x.reshape(n, d)
    bias_tiles = _bias_tables(rel_bias)
    bd64 = _block_diag_ones(512, DIFF_QK)
    bd128 = _block_diag_ones(512, DSA_DIM)
    gw = GMLP_GROUPS * LANES
    qkv_w = 3 * GDN_HEADS * GDN_DIM

    for layer in range(depth):
        j = layer // 2
        g_mix = mix_norm_g[layer].reshape(1, d)
        if layer % 2 == 0:
            w = ev_w_in[j]
            o_u, o_v, o_qkv = 0, gw, 2 * gw
            o_b = o_qkv + qkv_w
            o_a = o_b + GDN_HEADS
            o_z = o_a + GDN_HEADS
            w_main = jnp.concatenate(
                [w[:, o_u:o_v], w[:, o_v:o_qkv], w[:, o_z:o_z + GDN_HEADS * GDN_DIM], w[:, o_qkv:o_b]],
                axis=1).astype(BF16)
            w_gate = _pad_lanes(w[:, o_b:o_z]).astype(BF16)
            proj, gates = _in_proj(h, g_mix, w_main, w_gate)
            gp = jnp.stack([
                _pad_lanes(jnp.concatenate([jnp.zeros((GDN_HEADS,), F32), gdn_a_log[j]])[None])[0],
                _pad_lanes(jnp.concatenate([jnp.zeros((GDN_HEADS,), F32), gdn_dt_bias[j]])[None])[0]])
            y_a, qkvn, gates2 = _even_prep(proj, gates, gmlp_w_s[j], gmlp_b_s[j].T, gdn_conv_w[j], gp, batch, seq)
            y_b = _gdn(qkvn, gates2, proj, gdn_norm_g[j].reshape(1, GDN_DIM), batch, seq)
            y1, y2, w_out = y_a, y_b, ev_w_out[j]
        else:
            lambda_init = 0.8 - 0.6 * math.exp(-0.3 * layer)
            w = od_w_in[j]
            c = np.cumsum([0, 512, 512, 512, 512, 128, 128, 512, 64, 8])
            dq, dk, dv, sq, sk, sv, iq, ik, iw = [w[:, c[i]:c[i + 1]] for i in range(9)]
            w_main = jnp.concatenate([dq, dk, dv, sq, iq, sk, sv, ik, ik], axis=1).astype(BF16)
            w_gate = _pad_lanes(iw).astype(BF16)
            proj, iw_out = _in_proj(h, g_mix, w_main, w_gate)
            dqn, dkn, sqn, skn, vt = _odd_prep(
                proj,
                jnp.tile(diff_q_norm_g[j], 2 * DIFF_HEADS)[None], jnp.tile(diff_k_norm_g[j], 2 * DIFF_HEADS)[None],
                jnp.tile(dsa_q_norm_g[j], DSA_HEADS)[None], dsa_k_norm_g[j][None], bd64, bd128, batch, seq)
            y_c = _diff_attn(dqn, dkn, vt, bias_tiles, diff_lambda[j], diff_sub_norm_g[j][None],
                             batch, seq, lambda_init)
            y_d = _dsa(sqn, skn, vt, proj, iw_out, bias_tiles, batch, seq)
            y1, y2, w_out = y_c, y_d, od_w_out[j]
        h = _ffn(h, y1, y2, w_out.astype(BF16), ffn_norm_g[layer].reshape(1, d), ffn_w_up[layer].astype(BF16),
                 ffn_conv_w[layer], ffn_conv_b[layer].reshape(1, 2 * D_FF), ffn_w_down[layer].astype(BF16), seq)
    return h.reshape(batch, seq, d)
```

```python
import functools
import math

import numpy as np
import jax
import jax.numpy as jnp
from jax import lax
from jax.experimental import pallas as pl
from jax.experimental.pallas import tpu as pltpu

F32 = jnp.float32
BF16 = jnp.bfloat16
I32 = jnp.int32
I16 = jnp.int16
HIGHEST = lax.Precision.HIGHEST

D_MODEL = 1024
GMLP_GROUPS = 4
GMLP_CHUNK = 128
GDN_HEADS = 4
GDN_DIM = 128
GDN_CHUNK = 64
GDN_CONV = 4
DIFF_HEADS = 4
DIFF_QK = 64
DSA_HEADS = 4
DSA_DIM = 128
IDX_HEADS = 8
IDX_DIM = 64
IDX_TOPK_MAX = 256
REL_BUCKETS = 32
REL_MAX_DIST = 128
D_FF = 2816
FFN_CONV = 3
EPS = 1e-6

LANES = 128
SUBLANES_F32 = 8
SUBLANES_BF16 = 16
VMEM_LIMIT = 56 * 1024 * 1024
VT_ROWS = LANES + SUBLANES_BF16

TM_PROJ = 512
TM_FFN = 512
TL_PREP = 512
TL_GDN = 512
GDN_SEQS_PER_STEP = 4
GDN_PREP_PROBLEMS = 32
TQ = 256
TK = 256
FF_CHUNK = 256
NEG = -1e30
LOG2E = 1.0 / math.log(2.0)
INT_MIN = -2 ** 31


def _cparams(n_axes):
    return pltpu.CompilerParams(dimension_semantics=("arbitrary",) * n_axes, vmem_limit_bytes=VMEM_LIMIT)


def _dot(a, b, **kw):
    return jnp.dot(a, b, preferred_element_type=F32, **kw)


def _dot_nt(a, b, **kw):
    return lax.dot_general(a, b, (((1,), (1,)), ((), ())), preferred_element_type=F32, **kw)


def _dot_tn(a, b, **kw):
    return lax.dot_general(a, b, (((0,), (0,)), ((), ())), preferred_element_type=F32, **kw)


def _rms(x, g):
    return x * lax.rsqrt(jnp.mean(x * x, axis=-1, keepdims=True) + EPS) * g


def _sigmoid(x):
    return 1.0 / (1.0 + jnp.exp2(x * -LOG2E))


def _silu(x):
    return x / (1.0 + jnp.exp2(x * -LOG2E))


def _gelu_tanh(x):
    k0 = -2.0 * math.sqrt(2.0 / math.pi) * LOG2E
    return x / (1.0 + jnp.exp2(x * (k0 + (k0 * 0.044715) * (x * x))))


def _softplus(x):
    return jnp.maximum(x, 0.0) + jnp.log(1.0 + jnp.exp(-jnp.abs(x)))


def _in_proj_body(h_ref, g_ref, w_ref, wg_ref, o_ref, og_ref, *, nout):
    xn = _rms(h_ref[...], g_ref[...]).astype(BF16)
    for c in range(0, nout, 512):
        e = min(c + 512, nout)
        o_ref[:, c:e] = _dot(xn, w_ref[:, c:e]).astype(BF16)
    og_ref[...] = _dot(xn, wg_ref[...])


def _in_proj(h, g, w, wg):
    n, d = h.shape
    nout = w.shape[1]
    return pl.pallas_call(
        functools.partial(_in_proj_body, nout=nout),
        grid=(n // TM_PROJ,),
        in_specs=[
            pl.BlockSpec((TM_PROJ, d), lambda i: (i, 0)),
            pl.BlockSpec((1, d), lambda i: (0, 0)),
            pl.BlockSpec((d, nout), lambda i: (0, 0)),
            pl.BlockSpec((d, LANES), lambda i: (0, 0)),
        ],
        out_specs=[
            pl.BlockSpec((TM_PROJ, nout), lambda i: (i, 0)),
            pl.BlockSpec((TM_PROJ, LANES), lambda i: (i, 0)),
        ],
        out_shape=[jax.ShapeDtypeStruct((n, nout), BF16), jax.ShapeDtypeStruct((n, LANES), F32)],
        compiler_params=_cparams(1),
        name="in_proj",
    )(h, g, w, wg)


def _ffn_body(h_ref, halo_ref, y1_ref, y1h_ref, y2_ref, y2h_ref, wo_ref, g_ref, wup_ref, cw_ref, cb_ref, wdn_ref,
              o_ref, xn_ref, act_ref, *, tiles_per_seq):
    tm = h_ref.shape[0]
    i = pl.program_id(0)
    half = y1_ref.shape[1]

    def mixer_residual(hh, y1, y2):
        return hh + _dot(y1, wo_ref[:half, :]) + _dot(y2, wo_ref[half:, :])

    h = mixer_residual(h_ref[...], y1_ref[...], y2_ref[...])
    g = g_ref[...]
    hb = SUBLANES_BF16
    xn_ref[hb:, :] = _rms(h, g).astype(BF16)
    keep = jnp.where(i % tiles_per_seq == 0, 0.0, 1.0)
    halo = mixer_residual(halo_ref[...], y1h_ref[...], y2h_ref[...])
    xn_ref[:hb, :] = (_rms(halo, g) * keep).astype(BF16)
    rows = tm // 2
    halves = (0, rows)

    def up_pair(c):
        return [[_dot(xn_ref[r0:r0 + rows + hb, :], wup_ref[:, base:base + FF_CHUNK]) for base in (c, D_FF + c)]
                for r0 in halves]

    def conv(up, base):
        cw = cw_ref[:, base:base + FF_CHUNK]
        return (cw[0:1] * up[hb - 2:rows + hb - 2] + cw[1:2] * up[hb - 1:rows + hb - 1] + cw[2:3] * up[hb:rows + hb]
                + cb_ref[:, base:base + FF_CHUNK])

    chunks = list(range(0, D_FF, FF_CHUNK))
    ups = up_pair(chunks[0])
    for n, c in enumerate(chunks):
        nxt = up_pair(chunks[n + 1]) if n + 1 < len(chunks) else None
        for r0, (gate, val) in zip(halves, ups):
            act_ref[r0:r0 + rows, c:c + FF_CHUNK] = (_silu(conv(gate, c)) * conv(val, D_FF + c)).astype(BF16)
        ups = nxt
    o_ref[...] = h + _dot(act_ref[...], wdn_ref[...])


def _ffn(h, y1, y2, wo, g, wup, cw, cb, wdn, seq):
    n, d = h.shape
    half = y1.shape[1]
    tiles_per_seq = seq // TM_FFN
    halo_blocks = TM_FFN // SUBLANES_BF16
    const = dict(pipeline_mode=pl.Buffered(1))
    tile = lambda i: (i, 0)
    halo = lambda i: (jnp.maximum(i * halo_blocks - 1, 0), 0)
    return pl.pallas_call(
        functools.partial(_ffn_body, tiles_per_seq=tiles_per_seq),
        grid=(n // TM_FFN,),
        in_specs=[
            pl.BlockSpec((TM_FFN, d), tile),
            pl.BlockSpec((SUBLANES_BF16, d), halo),
            pl.BlockSpec((TM_FFN, half), tile),
            pl.BlockSpec((SUBLANES_BF16, half), halo),
            pl.BlockSpec((TM_FFN, half), tile),
            pl.BlockSpec((SUBLANES_BF16, half), halo),
            pl.BlockSpec((2 * half, d), lambda i: (0, 0), **const),
            pl.BlockSpec((1, d), lambda i: (0, 0)),
            pl.BlockSpec((d, 2 * D_FF), lambda i: (0, 0), **const),
            pl.BlockSpec((FFN_CONV, 2 * D_FF), lambda i: (0, 0)),
            pl.BlockSpec((1, 2 * D_FF), lambda i: (0, 0)),
            pl.BlockSpec((D_FF, d), lambda i: (0, 0), **const),
        ],
        out_specs=pl.BlockSpec((TM_FFN, d), lambda i: (i, 0)),
        out_shape=jax.ShapeDtypeStruct((n, d), F32),
        scratch_shapes=[pltpu.VMEM((TM_FFN + SUBLANES_BF16, d), BF16), pltpu.VMEM((TM_FFN, D_FF), BF16)],
        compiler_params=_cparams(1),
        name="ffn",
    )(h, h, y1, y1, y2, y2, wo, g, wup, cw, cb, wdn)


def _even_prep_body(uv_ref, qkv_ref, halo_ref, gate_ref, ws_ref, bs_ref, cw_ref, gp_ref,
                    ya_ref, qkvo_ref, gout_ref, xs_ref):
    tl = uv_ref.shape[0]
    t = pl.program_id(1)
    gw = LANES
    row = lax.broadcasted_iota(I32, (GMLP_CHUNK, GMLP_CHUNK), 0)
    col = lax.broadcasted_iota(I32, (GMLP_CHUNK, GMLP_CHUNK), 1)
    for gi in range(GMLP_GROUPS):
        u = _gelu_tanh(uv_ref[:, gi * gw:(gi + 1) * gw].astype(F32))
        v = _gelu_tanh(uv_ref[:, (GMLP_GROUPS + gi) * gw:(GMLP_GROUPS + gi + 1) * gw].astype(F32))
        vc = v - jnp.mean(v, axis=-1, keepdims=True)
        vn = (vc * lax.rsqrt(jnp.mean(vc * vc, axis=-1, keepdims=True) + EPS)).astype(BF16)
        w = jnp.where(col <= row, ws_ref[gi], 0.0).astype(BF16)
        b = bs_ref[:, gi:gi + 1]
        for c in range(tl // GMLP_CHUNK):
            r0 = c * GMLP_CHUNK
            mixed = _dot(w, vn[r0:r0 + GMLP_CHUNK]) + b
            ya_ref[r0:r0 + GMLP_CHUNK, gi * gw:(gi + 1) * gw] = (u[r0:r0 + GMLP_CHUNK] * mixed).astype(BF16)
    keep = jnp.where(t == 0, 0.0, 1.0)
    hb = SUBLANES_BF16
    for j in range(3 * GDN_HEADS):
        sl = slice(j * gw, (j + 1) * gw)
        xs_ref[:hb, :] = halo_ref[:, sl].astype(F32) * keep
        xs_ref[hb:, :] = qkv_ref[:, sl].astype(F32)
        cw = cw_ref[:, sl]
        y = cw[0:1] * xs_ref[hb - 3:tl + hb - 3, :]
        for k in range(1, GDN_CONV):
            y = y + cw[k:k + 1] * xs_ref[hb - 3 + k:tl + hb - 3 + k, :]
        y = _silu(y)
        if j < 2 * GDN_HEADS:
            y = y * lax.rsqrt(jnp.sum(y * y, axis=-1, keepdims=True) + EPS)
        if j < GDN_HEADS:
            y = y * (GDN_DIM ** -0.5)
        qkvo_ref[:, sl] = y.astype(BF16)
    x = gate_ref[...]
    lane = lax.broadcasted_iota(I32, x.shape, 1)
    beta = _sigmoid(x)
    gdec = -jnp.exp(gp_ref[0:1, :]) * _softplus(x + gp_ref[1:2, :])
    gout_ref[...] = jnp.where(lane < GDN_HEADS, beta, gdec)


def _even_prep(proj, gates, ws, bs_t, cw, gp, batch, seq):
    n = proj.shape[0]
    nt = seq // TL_PREP
    halo_blocks = TL_PREP // SUBLANES_BF16
    qkv_w = 3 * GDN_HEADS * GDN_DIM
    row = lambda b, t: b * nt + t
    return pl.pallas_call(
        _even_prep_body,
        grid=(batch, nt),
        in_specs=[
            pl.BlockSpec((TL_PREP, 2 * GMLP_GROUPS * LANES), lambda b, t: (row(b, t), 0)),
            pl.BlockSpec((TL_PREP, qkv_w), lambda b, t: (row(b, t), 1)),
            pl.BlockSpec((SUBLANES_BF16, qkv_w), lambda b, t: (jnp.maximum(row(b, t) * halo_blocks - 1, 0), 1)),
            pl.BlockSpec((TL_PREP, LANES), lambda b, t: (row(b, t), 0)),
            pl.BlockSpec((GMLP_GROUPS, GMLP_CHUNK, GMLP_CHUNK), lambda b, t: (0, 0, 0)),
            pl.BlockSpec((GMLP_CHUNK, GMLP_GROUPS), lambda b, t: (0, 0)),
            pl.BlockSpec((GDN_CONV, qkv_w), lambda b, t: (0, 0)),
            pl.BlockSpec((2, LANES), lambda b, t: (0, 0)),
        ],
        out_specs=[
            pl.BlockSpec((TL_PREP, GMLP_GROUPS * LANES), lambda b, t: (row(b, t), 0)),
            pl.BlockSpec((TL_PREP, qkv_w), lambda b, t: (row(b, t), 0)),
            pl.BlockSpec((TL_PREP, LANES), lambda b, t: (row(b, t), 0)),
        ],
        out_shape=[
            jax.ShapeDtypeStruct((n, GMLP_GROUPS * LANES), BF16),
            jax.ShapeDtypeStruct((n, qkv_w), BF16),
            jax.ShapeDtypeStruct((n, LANES), F32),
        ],
        scratch_shapes=[pltpu.VMEM((TL_PREP + SUBLANES_BF16, LANES), F32)],
        compiler_params=_cparams(2),
        name="even_prep",
    )(proj, proj, proj, gates, ws, bs_t, cw, gp)


def _split_bf16(x):
    hi = x.astype(BF16)
    return hi, (x - hi.astype(F32)).astype(BF16)


def _mm3(a, b):
    ah, al = _split_bf16(a)
    bh, bl = _split_bf16(b)
    return _dot(ah, bh) + (_dot(ah, bl) + _dot(al, bh))


def _unit_lower_inverses(lows):
    c = lows[0].shape[0]
    row = lax.broadcasted_iota(I32, (c, c), 0)
    col = lax.broadcasted_iota(I32, (c, c), 1)
    eye = jnp.where(row == col, 1.0, 0.0)
    same16 = jnp.right_shift(row, 4) == jnp.right_shift(col, 4)
    same32 = jnp.right_shift(row, 5) == jnp.right_shift(col, 5)
    mm1 = lambda a, b: _dot(a.astype(BF16), b.astype(BF16))
    ps = [jnp.where(same16, -low, 0.0) for low in lows]
    xs = [eye + p for p in ps]
    for mm in (_mm3, mm1, mm1):
        ps = [mm(p, p) for p in ps]
        xs = [x + mm(x, p) for x, p in zip(xs, ps)]
    mid = jnp.logical_and(same32, jnp.logical_not(same16))
    ys = [mm1(jnp.where(mid, low, 0.0), x) for low, x in zip(lows, xs)]
    xs = [x - mm1(x, y) for x, y in zip(xs, ys)]
    ys = [mm1(jnp.where(same32, 0.0, low), x) for low, x in zip(lows, xs)]
    return [x - mm1(x, y) for x, y in zip(xs, ys)]


def _gdn_body(qkv_ref, gate_ref, z_ref, gn_ref, o_ref,
              s_ref, u_ref, w_ref, qg_ref, kdt_ref, intra_ref, egl_ref):
    nseq, tl = qkv_ref.shape[0], qkv_ref.shape[1]
    c = GDN_CHUNK
    d = GDN_DIM
    lanes = [(bi, hd) for bi in range(nseq) for hd in range(GDN_HEADS)]
    flat = lambda bi, hd: bi * GDN_HEADS + hd
    prep_chunks = max(1, GDN_PREP_PROBLEMS // len(lanes))
    t = pl.program_id(1)

    @pl.when(t == 0)
    def _():
        s_ref[...] = jnp.zeros_like(s_ref)

    row = lax.broadcasted_iota(I32, (c, c), 0)
    col = lax.broadcasted_iota(I32, (c, c), 1)
    incl = col <= row
    strict = col < row
    lmat = jnp.where(incl, 1.0, 0.0)
    gn = gn_ref[...]

    def prepare(it, carry):
        cis = [it * prep_chunks + j for j in range(prep_chunks)]
        rows = [pl.ds(pl.multiple_of(ci * c, c), c) for ci in cis]
        gates = {(j, bi): gate_ref[bi, rows[j], :] for j in range(prep_chunks) for bi in range(nseq)}
        gcum = {key: _dot(lmat, g, precision=HIGHEST) for key, g in gates.items()}
        pairs = [(j, bi, hd) for j in range(prep_chunks) for bi, hd in lanes]
        qs = [qkv_ref[bi, rows[j], hd * d:(hd + 1) * d] for j, bi, hd in pairs]
        ks = [qkv_ref[bi, rows[j], (GDN_HEADS + hd) * d:(GDN_HEADS + hd + 1) * d] for j, bi, hd in pairs]
        vs = [qkv_ref[bi, rows[j], (2 * GDN_HEADS + hd) * d:(2 * GDN_HEADS + hd + 1) * d] for j, bi, hd in pairs]
        beta = [jnp.broadcast_to(gates[j, bi][:, hd:hd + 1], (c, d)) for j, bi, hd in pairs]
        gc = [jnp.broadcast_to(gcum[j, bi][:, GDN_HEADS + hd:GDN_HEADS + hd + 1], (c, d)) for j, bi, hd in pairs]
        decay = [jnp.where(incl, jnp.exp(jnp.where(incl, g[:, :c] - g.T[:c, :], 0.0)), 0.0) for g in gc]
        kb = [k.astype(F32) * b for k, b in zip(ks, beta)]
        lows = [jnp.where(strict, _dot_nt(x.astype(BF16), k) * dc, 0.0) for x, k, dc in zip(kb, ks, decay)]
        tinv = [x.astype(BF16) for x in _unit_lower_inverses(lows)]
        egc = [jnp.exp(g) for g in gc]
        rhs = [jnp.concatenate([(v.astype(F32) * b).astype(BF16), (x * e).astype(BF16)], axis=1)
               for v, b, x, e in zip(vs, beta, kb, egc)]
        uw = [_dot(ti, r) for ti, r in zip(tinv, rhs)]
        qk = [_dot_nt(q, k) for q, k in zip(qs, ks)]
        for n, (j, bi, hd) in enumerate(pairs):
            ci, f = cis[j], flat(bi, hd)
            u_ref[ci, f] = uw[n][:, :d]
            w_ref[ci, f] = uw[n][:, d:].astype(BF16)
            intra_ref[ci, f] = jnp.where(incl, qk[n] * decay[n], 0.0).astype(BF16)
            qg_ref[ci, f] = (qs[n].astype(F32) * egc[n]).astype(BF16)
            g_last = gc[n][c - 1:c, :]
            kdt_ref[ci, f] = (ks[n].astype(F32) * jnp.exp(g_last - gc[n])).T.astype(BF16)
            egl_ref[ci, f] = jnp.exp(g_last)
        return carry

    lax.fori_loop(0, tl // c // prep_chunks, prepare, 0)

    def scan(ci, carry):
        r0 = pl.multiple_of(ci * c, c)
        fs = [flat(bi, hd) for bi, hd in lanes]
        states = [s_ref[f] for f in fs]
        sb = [s.astype(BF16) for s in states]
        ws = [_dot(w_ref[ci, f], s) for f, s in zip(fs, sb)]
        qsd = [_dot(qg_ref[ci, f], s) for f, s in zip(fs, sb)]
        vnb = [(u_ref[ci, f] - w).astype(BF16) for f, w in zip(fs, ws)]
        for f, s, v in zip(fs, states, vnb):
            s_ref[f] = s * egl_ref[ci, f] + _dot(kdt_ref[ci, f], v)
        for (bi, hd), f, qd, v in zip(lanes, fs, qsd, vnb):
            out = qd + _dot(intra_ref[ci, f], v)
            z = z_ref[bi, pl.ds(r0, c), hd * d:(hd + 1) * d].astype(F32)
            o_ref[bi, pl.ds(r0, c), hd * d:(hd + 1) * d] = (_rms(out, gn) * _silu(z)).astype(BF16)
        return carry

    lax.fori_loop(0, tl // c, scan, 0)


def _gdn(qkv, gates, proj, gn, batch, seq):
    n = qkv.shape[0]
    nt = seq // TL_GDN
    nc = TL_GDN // GDN_CHUNK
    width = GDN_HEADS * GDN_DIM
    nseq = GDN_SEQS_PER_STEP if batch % GDN_SEQS_PER_STEP == 0 else 1
    nl = nseq * GDN_HEADS
    as3d = lambda a: a.reshape(batch, seq, a.shape[-1])
    out = pl.pallas_call(
        _gdn_body,
        grid=(batch // nseq, nt),
        in_specs=[
            pl.BlockSpec((nseq, TL_GDN, 3 * width), lambda b, t: (b, t, 0)),
            pl.BlockSpec((nseq, TL_GDN, LANES), lambda b, t: (b, t, 0)),
            pl.BlockSpec((nseq, TL_GDN, width), lambda b, t: (b, t, 2)),
            pl.BlockSpec((1, GDN_DIM), lambda b, t: (0, 0)),
        ],
        out_specs=pl.BlockSpec((nseq, TL_GDN, width), lambda b, t: (b, t, 0)),
        out_shape=jax.ShapeDtypeStruct((batch, seq, width), BF16),
        scratch_shapes=[
            pltpu.VMEM((nl, GDN_DIM, GDN_DIM), F32),
            pltpu.VMEM((nc, nl, GDN_CHUNK, GDN_DIM), F32),
            pltpu.VMEM((nc, nl, GDN_CHUNK, GDN_DIM), BF16),
            pltpu.VMEM((nc, nl, GDN_CHUNK, GDN_DIM), BF16),
            pltpu.VMEM((nc, nl, GDN_DIM, GDN_CHUNK), BF16),
            pltpu.VMEM((nc, nl, GDN_CHUNK, GDN_CHUNK), BF16),
            pltpu.VMEM((nc, nl, 1, GDN_DIM), F32),
        ],
        compiler_params=_cparams(2),
        name="gdn",
    )(as3d(qkv), as3d(gates), as3d(proj), gn)
    return out.reshape(n, width)


def _group_mean_sq(x, ones_bd, group):
    x2 = x * x
    hi = x2.astype(BF16)
    lo = (x2 - hi.astype(F32)).astype(BF16)
    return (_dot(hi, ones_bd) + _dot(lo, ones_bd)) * (1.0 / group)


def _odd_prep_body(dq_ref, dk_ref, dv_ref, sq_ref, sk_ref, sv_ref, gq_ref, gk_ref, gsq_ref, gsk_ref, bd64_ref,
                   bd128_ref, dqo_ref, dko_ref, sqo_ref, sko_ref, vt_ref):
    ones = jnp.ones((VT_ROWS - LANES, TK), BF16)
    for j in range(DIFF_HEADS + 1):
        src = sv_ref[...] if j == DIFF_HEADS else dv_ref[:, j * LANES:(j + 1) * LANES]
        xt = src.astype(F32).T
        for c in range(xt.shape[1] // TK):
            vt_ref[j, c, :LANES, :] = xt[:, c * TK:(c + 1) * TK].astype(BF16)
            vt_ref[j, c, LANES:, :] = ones
    bd64 = bd64_ref[...]
    x = dq_ref[...].astype(F32)
    dqo_ref[...] = (x * lax.rsqrt(_group_mean_sq(x, bd64, DIFF_QK) + EPS) * gq_ref[...]
                    * (DIFF_QK ** -0.5 * LOG2E)).astype(BF16)
    x = dk_ref[...].astype(F32)
    dko_ref[...] = (x * lax.rsqrt(_group_mean_sq(x, bd64, DIFF_QK) + EPS) * gk_ref[...]).astype(BF16)
    x = sq_ref[...].astype(F32)
    sqo_ref[...] = (x * lax.rsqrt(_group_mean_sq(x, bd128_ref[...], DSA_DIM) + EPS) * gsq_ref[...]
                    * (DSA_DIM ** -0.5 * LOG2E)).astype(BF16)
    x = sk_ref[...].astype(F32)
    sko_ref[...] = _rms(x, gsk_ref[...]).astype(BF16)


def _odd_prep(proj, gq, gk, gsq, gsk, bd64, bd128, batch, seq):
    n = proj.shape[0]
    w = 512
    nt = seq // TL_PREP
    kt = TL_PREP // TK
    full = lambda shape: pl.BlockSpec(shape, lambda b, t: (0,) * len(shape))
    row = lambda b, t: b * nt + t
    return pl.pallas_call(
        _odd_prep_body,
        grid=(batch, nt),
        in_specs=[
            pl.BlockSpec((TL_PREP, w), lambda b, t: (row(b, t), 0)),
            pl.BlockSpec((TL_PREP, w), lambda b, t: (row(b, t), 1)),
            pl.BlockSpec((TL_PREP, w), lambda b, t: (row(b, t), 2)),
            pl.BlockSpec((TL_PREP, w), lambda b, t: (row(b, t), 3)),
            pl.BlockSpec((TL_PREP, LANES), lambda b, t: (row(b, t), 5 * w // LANES)),
            pl.BlockSpec((TL_PREP, LANES), lambda b, t: (row(b, t), 5 * w // LANES + 1)),
            full((1, w)), full((1, w)), full((1, w)), full((1, LANES)), full((w, w)), full((w, w)),
        ],
        out_specs=[
            pl.BlockSpec((TL_PREP, w), lambda b, t: (row(b, t), 0)),
            pl.BlockSpec((TL_PREP, w), lambda b, t: (row(b, t), 0)),
            pl.BlockSpec((TL_PREP, w), lambda b, t: (row(b, t), 0)),
            pl.BlockSpec((TL_PREP, LANES), lambda b, t: (row(b, t), 0)),
            pl.BlockSpec((None, DIFF_HEADS + 1, kt, VT_ROWS, TK), lambda b, t: (b, 0, t, 0, 0)),
        ],
        out_shape=[
            jax.ShapeDtypeStruct((n, w), BF16), jax.ShapeDtypeStruct((n, w), BF16),
            jax.ShapeDtypeStruct((n, w), BF16), jax.ShapeDtypeStruct((n, LANES), BF16),
            jax.ShapeDtypeStruct((batch, DIFF_HEADS + 1, seq // TK, VT_ROWS, TK), BF16),
        ],
        compiler_params=_cparams(2),
        name="odd_prep",
    )(proj, proj, proj, proj, proj, proj, gq, gk, gsq, gsk, bd64, bd128)


def _bias_tile_index(kb, qb):
    return jnp.clip(kb - (qb - 2), 0, 2)


def _softmax_step_t(s_t, vt, m_ref, acc_ref):
    for g in range(s_t.shape[1] // LANES):
        sl = slice(g * LANES, (g + 1) * LANES)
        s = s_t[:, sl]
        m_prev = m_ref[:, sl]
        m_new = jnp.maximum(m_prev, jnp.max(s, axis=0, keepdims=True))
        alpha = jnp.exp2(m_prev - m_new)
        p = jnp.exp2(s - m_new)
        acc_ref[:, sl] = alpha * acc_ref[:, sl] + _dot(vt, p.astype(BF16))
        m_ref[:, sl] = m_new


def _attend_tiles(logits, values, nblk, s_ref, m_ref, acc_ref):
    streams = range(len(logits))
    m_ref[...] = jnp.full_like(m_ref, NEG)
    acc_ref[...] = jnp.zeros_like(acc_ref)
    for i in streams:
        s_ref[i, 0] = logits[i](0)

    def pair(it, carry):
        kb = 2 * it
        for i in streams:
            s_ref[i, 1] = logits[i](kb + 1)
        for i in streams:
            _softmax_step_t(s_ref[i, 0], values[i](kb), m_ref.at[i], acc_ref.at[i])
        for i in streams:
            s_ref[i, 0] = logits[i](jnp.minimum(kb + 2, nblk - 1))
        for i in streams:
            _softmax_step_t(s_ref[i, 1], values[i](kb + 1), m_ref.at[i], acc_ref.at[i])
        return carry

    lax.fori_loop(0, nblk // 2, pair, 0)

    @pl.when(nblk % 2 == 1)
    def _():
        for i in streams:
            _softmax_step_t(s_ref[i, 0], values[i](nblk - 1), m_ref.at[i], acc_ref.at[i])


def _diff_attn_body(q_ref, k_ref, vt_ref, bias_ref, lam_ref, sg_ref, o_ref,
                    qt_ref, m_ref, acc_ref, s_ref, *, lambda_init):
    qi = pl.program_id(1)
    nblk = qi + 1
    heads = range(DIFF_HEADS)
    for hd in heads:
        qt = q_ref[:, hd * LANES:(hd + 1) * LANES].astype(F32).T
        dim = lax.broadcasted_iota(I32, qt.shape, 0)
        qt_ref[hd, :, :TQ] = jnp.where(dim < DIFF_QK, qt, 0.0).astype(BF16)
        qt_ref[hd, :, TQ:] = jnp.where(dim >= DIFF_QK, qt, 0.0).astype(BF16)

    def logits_of(hd):
        def logits(kb):
            r0 = pl.multiple_of(kb * TK, TK)
            b = bias_ref[hd, _bias_tile_index(kb, qi)]
            s = _dot(k_ref[pl.ds(r0, TK), hd * LANES:(hd + 1) * LANES], qt_ref[hd])
            return s + jnp.concatenate([b, b], axis=1)
        return logits

    _attend_tiles([logits_of(hd) for hd in heads], [lambda kb, hd=hd: vt_ref[hd, kb] for hd in heads],
                  nblk, s_ref, m_ref, acc_ref)

    lf = lam_ref[...]
    lam = (jnp.exp(jnp.sum(lf[0:1] * lf[1:2], axis=-1, keepdims=True))
           - jnp.exp(jnp.sum(lf[2:3] * lf[3:4], axis=-1, keepdims=True)) + lambda_init)
    for hd in heads:
        den = acc_ref[hd, LANES:LANES + 1, :]
        ot = (acc_ref[hd, :LANES, :TQ] / den[:, :TQ] - lam * (acc_ref[hd, :LANES, TQ:] / den[:, TQ:]))
        ot = ot * lax.rsqrt(jnp.mean(ot * ot, axis=0, keepdims=True) + EPS)
        o_ref[:, hd * LANES:(hd + 1) * LANES] = (ot.T * sg_ref[...] * (1.0 - lambda_init)).astype(BF16)


def _diff_attn(dqn, dkn, vt, bias, lam_p, sub_g, batch, seq, lambda_init):
    n = dqn.shape[0]
    nq = seq // TQ
    width = DIFF_HEADS * LANES
    return pl.pallas_call(
        functools.partial(_diff_attn_body, lambda_init=lambda_init),
        grid=(batch, nq),
        in_specs=[
            pl.BlockSpec((TQ, width), lambda b, q: (b * nq + q, 0)),
            pl.BlockSpec((seq, width), lambda b, q: (b, 0)),
            pl.BlockSpec((None, DIFF_HEADS + 1, seq // TK, VT_ROWS, TK), lambda b, q: (b, 0, 0, 0, 0)),
            pl.BlockSpec((DIFF_HEADS, 3, TK, TQ), lambda b, q: (0, 0, 0, 0)),
            pl.BlockSpec((4, DIFF_QK), lambda b, q: (0, 0)),
            pl.BlockSpec((1, LANES), lambda b, q: (0, 0)),
        ],
        out_specs=pl.BlockSpec((TQ, width), lambda b, q: (b * nq + q, 0)),
        out_shape=jax.ShapeDtypeStruct((n, width), BF16),
        scratch_shapes=[
            pltpu.VMEM((DIFF_HEADS, LANES, 2 * TQ), BF16),
            pltpu.VMEM((DIFF_HEADS, 1, 2 * TQ), F32),
            pltpu.VMEM((DIFF_HEADS, VT_ROWS, 2 * TQ), F32),
            pltpu.VMEM((DIFF_HEADS, 2, TK, 2 * TQ), F32),
        ],
        compiler_params=_cparams(2),
        name="diff_attn",
    )(dqn, dkn, vt, bias, lam_p, sub_g)


def _dsa_body(q_ref, k_ref, vt_ref, iq_ref, ik_ref, iw_ref, bias_ref, o_ref,
              qt_ref, qit_ref, wt_ref, keys_ref, hi_ref, lo_ref, t_ref, m_ref, acc_ref, s_ref, *, top_k, pos_bits):
    qb = pl.program_id(1)
    nkb = qb + 1

    for p in range(IDX_HEADS // 2):
        gt = iq_ref[:, p * LANES:(p + 1) * LANES].astype(F32).T.astype(BF16)
        qit_ref[:, (2 * p) * TQ:(2 * p + 1) * TQ] = gt[:IDX_DIM]
        qit_ref[:, (2 * p + 1) * TQ:(2 * p + 2) * TQ] = gt[IDX_DIM:]
    for hd in range(DSA_HEADS):
        qt_ref[:, hd * TQ:(hd + 1) * TQ] = q_ref[:, hd * LANES:(hd + 1) * LANES].astype(F32).T.astype(BF16)
    wscale = (IDX_HEADS ** -0.5) * (IDX_DIM ** -0.5)
    wt_ref[...] = (iw_ref[...] * wscale).T[:IDX_HEADS, :]

    key_i = lax.broadcasted_iota(I32, (TK, TQ), 0)
    qry_i = lax.broadcasted_iota(I32, (TK, TQ), 1)
    causal = key_i <= qry_i

    def index_blocks(kbs, masked):
        iks = [ik_ref[pl.ds(pl.multiple_of(kb * TK, TK), TK), :IDX_DIM] for kb in kbs]
        idxs = [jnp.zeros((TK, TQ), F32) for _ in kbs]
        for hi in range(IDX_HEADS):
            ss = [_dot(ik, qit_ref[:, hi * TQ:(hi + 1) * TQ]) for ik in iks]
            idxs = [idx + jnp.maximum(s, 0.0) * wt_ref[hi:hi + 1, :] for idx, s in zip(idxs, ss)]
        for kb, idx in zip(kbs, idxs):
            idx = jnp.where(idx == 0.0, 0.0, idx)
            bits = pltpu.bitcast(idx, I32)
            key = jnp.where(bits < 0, bits ^ jnp.int32(0x7FFFFFFF), bits)
            if masked:
                key = jnp.where(causal, key, jnp.int32(INT_MIN))
            keys_ref[kb] = key
            hi_ref[kb] = jnp.right_shift(key, 16).astype(I16)
            lo_ref[kb] = ((key & 0xFFFF) - 0x8000).astype(I16)

    def index_pair(i, carry):
        index_blocks([2 * i, 2 * i + 1], False)
        return carry

    lax.fori_loop(0, (nkb - 1) // 2, index_pair, 0)

    @pl.when((nkb - 1) % 2 == 1)
    def _():
        index_blocks([nkb - 2], False)

    index_blocks([nkb - 1], True)

    kf = float(top_k)
    rows16 = SUBLANES_BF16
    i16_min = -2 ** 15

    def count16(ref, cand, strict=False):
        c16 = jnp.broadcast_to(cand, (rows16, TQ)).astype(I16)

        def body(kb, acc):
            x = ref[kb]
            parts = []
            for i in range(TK // rows16):
                slab = x[i * rows16:(i + 1) * rows16]
                hit = slab > c16 if strict else slab >= c16
                parts.append(jnp.where(hit, jnp.int16(1), jnp.int16(0)))
            while len(parts) > 1:
                parts = [parts[i] + parts[i + 1] for i in range(0, len(parts), 2)]
            return acc + parts[0]

        acc = lax.fori_loop(0, nkb, body, jnp.zeros((rows16, TQ), I16))
        return jnp.sum(acc.astype(F32), axis=0, keepdims=True)

    def digit_select(ref, need, fallback):
        c0 = count16(ref, jnp.zeros((1, TQ), I32))
        ok0 = c0 >= need
        d0 = jnp.where(ok0, jnp.int32(0), jnp.int32(i16_min))
        n0 = jnp.where(ok0, c0, fallback)

        def bit_step(i, carry):
            d, n = carry
            cand = d | jnp.left_shift(jnp.int32(1), 14 - i)
            cnt = count16(ref, cand)
            ok = cnt >= need
            return jnp.where(ok, cand, d), jnp.where(ok, cnt, n)

        return lax.fori_loop(0, 15, bit_step, (d0, n0))

    total = (nkb * TK).astype(F32) * jnp.ones((1, TQ), F32)
    t_hi, n_hi = digit_select(hi_ref, kf, total)
    n_above = count16(hi_ref, t_hi, strict=True)
    t_hi16 = jnp.broadcast_to(t_hi, (rows16, TQ)).astype(I16)

    def restrict(kb, carry):
        for i in range(TK // rows16):
            sl = slice(i * rows16, (i + 1) * rows16)
            lo_ref[kb, sl, :] = jnp.where(hi_ref[kb, sl, :] == t_hi16, lo_ref[kb, sl, :], jnp.int16(i16_min))
        return carry

    lax.fori_loop(0, nkb, restrict, 0)
    t_lo, n_lo = digit_select(lo_ref, kf - n_above, n_hi - n_above)
    t = jnp.left_shift(t_hi, 16) + (t_lo + 0x8000)
    cge = n_above + n_lo
    t_ref[...] = t

    @pl.when(jnp.max(cge) > kf)
    def _():
        tt = t_ref[...]
        r = kf - (n_above + count16(lo_ref, t_lo, strict=True))
        off_threshold = 2 ** 15 - 1

        def mark(kb, carry):
            lo_ref[kb] = jnp.where(keys_ref[kb] == tt, key_i + kb * TK, off_threshold).astype(I16)
            return carry

        lax.fori_loop(0, nkb, mark, 0)

        def pos_step(i, pcut):
            cand = pcut | jnp.left_shift(jnp.int32(1), pos_bits - 1 - i)
            tied_before = total - count16(lo_ref, cand)
            return jnp.where(tied_before <= r, cand, pcut)

        pcut = lax.fori_loop(0, pos_bits, pos_step, jnp.zeros((1, TQ), I32))

        def demote(kb, carry):
            kk = keys_ref[kb]
            drop = jnp.logical_and(kk == tt, key_i + kb * TK >= pcut)
            keys_ref[kb] = jnp.where(drop, kk - 1, kk)
            return carry

        lax.fori_loop(0, nkb, demote, 0)

    def logits_of(hd):
        def logits(kb):
            r0 = pl.multiple_of(kb * TK, TK)
            sel = keys_ref[kb] >= t_ref[...]
            s = _dot(k_ref[pl.ds(r0, TK), :], qt_ref[:, hd * TQ:(hd + 1) * TQ])
            return jnp.where(sel, s + bias_ref[hd, _bias_tile_index(kb, qb)], NEG)
        return logits

    heads = range(DSA_HEADS)
    _attend_tiles([logits_of(hd) for hd in heads], [lambda kb: vt_ref[kb]] * DSA_HEADS, nkb, s_ref, m_ref, acc_ref)

    for hd in heads:
        o_ref[:, hd * LANES:(hd + 1) * LANES] = (acc_ref[hd, :LANES, :] / acc_ref[hd, LANES:LANES + 1, :]).T.astype(BF16)


def _dsa(sqn, skn, vt, proj, iw, bias, batch, seq):
    n = sqn.shape[0]
    nq = seq // TQ
    top_k = min(IDX_TOPK_MAX, seq // 4)
    pos_bits = int(seq).bit_length()
    iq_block = 4
    ik_block = (5 * 512 + 2 * LANES) // LANES
    return pl.pallas_call(
        functools.partial(_dsa_body, top_k=top_k, pos_bits=pos_bits),
        grid=(batch, nq),
        in_specs=[
            pl.BlockSpec((TQ, DSA_HEADS * LANES), lambda b, q: (b * nq + q, 0)),
            pl.BlockSpec((seq, LANES), lambda b, q: (b, 0)),
            pl.BlockSpec((None, None, seq // TK, VT_ROWS, TK), lambda b, q: (b, DIFF_HEADS, 0, 0, 0)),
            pl.BlockSpec((TQ, IDX_HEADS * IDX_DIM), lambda b, q: (b * nq + q, iq_block)),
            pl.BlockSpec((seq, LANES), lambda b, q: (b, ik_block)),
            pl.BlockSpec((TQ, LANES), lambda b, q: (b * nq + q, 0)),
            pl.BlockSpec((DSA_HEADS, 3, TK, TQ), lambda b, q: (1, 0, 0, 0)),
        ],
        out_specs=pl.BlockSpec((TQ, DSA_HEADS * LANES), lambda b, q: (b * nq + q, 0)),
        out_shape=jax.ShapeDtypeStruct((n, DSA_HEADS * LANES), BF16),
        scratch_shapes=[
            pltpu.VMEM((LANES, DSA_HEADS * TQ), BF16),
            pltpu.VMEM((IDX_DIM, IDX_HEADS * TQ), BF16),
            pltpu.VMEM((IDX_HEADS, TQ), F32),
            pltpu.VMEM((seq // TK, TK, TQ), I32),
            pltpu.VMEM((seq // TK, TK, TQ), I16),
            pltpu.VMEM((seq // TK, TK, TQ), I16),
            pltpu.VMEM((1, TQ), I32),
            pltpu.VMEM((DSA_HEADS, 1, TQ), F32),
            pltpu.VMEM((DSA_HEADS, VT_ROWS, TQ), F32),
            pltpu.VMEM((DSA_HEADS, 2, TK, TQ), F32),
        ],
        compiler_params=_cparams(2),
        name="dsa",
    )(sqn, skn, vt, proj, proj, iw, bias)


def _rel_bucket(dist):
    exact = REL_BUCKETS // 2
    n = jnp.maximum(dist, 0)
    nf = jnp.maximum(n, exact).astype(F32)
    far = exact + (jnp.log(nf / exact) / math.log(REL_MAX_DIST / exact) * (REL_BUCKETS - exact)).astype(I32)
    return jnp.where(n < exact, n, jnp.minimum(far, REL_BUCKETS - 1))


def _bias_tiles_body(tab_ref, bucket_ref, o_ref):
    hd = pl.program_id(0)
    nh = DIFF_HEADS + DSA_HEADS
    key = lax.broadcasted_iota(I32, (TK, TQ), 0)
    qry = lax.broadcasted_iota(I32, (TK, TQ), 1)
    for tile in range(3):
        bucket = bucket_ref[tile]
        acc = jnp.zeros(bucket.shape, F32)
        for b in range(REL_BUCKETS):
            acc = jnp.where(bucket == b, tab_ref[b * nh + hd], acc)
        if tile == 2:
            acc = jnp.where(key <= qry, acc, NEG)
        o_ref[tile] = acc * LOG2E


def _bias_tables(rel_bias):
    nh = DIFF_HEADS + DSA_HEADS
    assert TK + 1 >= REL_MAX_DIST
    j = jnp.arange(TK, dtype=I32)[:, None]
    i = jnp.arange(TQ, dtype=I32)[None, :]
    dist = jnp.stack([jnp.full((TK, TQ), 2 * TK, I32), i - j + TK, i - j])
    return pl.pallas_call(
        _bias_tiles_body,
        grid=(nh,),
        in_specs=[pl.BlockSpec(memory_space=pltpu.SMEM), pl.BlockSpec((3, TK, TQ), lambda h: (0, 0, 0))],
        out_specs=pl.BlockSpec((None, 3, TK, TQ), lambda h: (h, 0, 0, 0)),
        out_shape=jax.ShapeDtypeStruct((nh, 3, TK, TQ), F32),
        compiler_params=_cparams(1),
        name="bias_tiles",
    )(rel_bias.astype(F32).reshape(-1), _rel_bucket(dist))


def _block_diag_ones(width, group):
    r = np.arange(width)
    return jnp.asarray((r[:, None] // group) == (r[None, :] // group), dtype=BF16)


def _pad_lanes(w, width=LANES):
    return jnp.pad(w, ((0, 0), (0, width - w.shape[1])))


def kernel(x, rel_bias, mix_norm_g, ev_w_in, ev_w_out, gmlp_w_s, gmlp_b_s, gdn_conv_w, gdn_a_log, gdn_dt_bias,
           gdn_norm_g, od_w_in, od_w_out, diff_q_norm_g, diff_k_norm_g, diff_lambda, diff_sub_norm_g,
           dsa_q_norm_g, dsa_k_norm_g, ffn_norm_g, ffn_w_up, ffn_conv_w, ffn_conv_b, ffn_w_down):
    batch, seq, d = x.shape
    n = batch * seq
    depth = mix_norm_g.shape[0]
    assert d == D_MODEL and seq % max(TM_PROJ, TL_PREP, TL_GDN, TQ) == 0
    h = x.reshape(n, d)
    bias_tiles = _bias_tables(rel_bias)
    bd64 = _block_diag_ones(512, DIFF_QK)
    bd128 = _block_diag_ones(512, DSA_DIM)
    gw = GMLP_GROUPS * LANES
    qkv_w = 3 * GDN_HEADS * GDN_DIM

    for layer in range(depth):
        j = layer // 2
        g_mix = mix_norm_g[layer].reshape(1, d)
        if layer % 2 == 0:
            w = ev_w_in[j]
            o_u, o_v, o_qkv = 0, gw, 2 * gw
            o_b = o_qkv + qkv_w
            o_a = o_b + GDN_HEADS
            o_z = o_a + GDN_HEADS
            w_main = jnp.concatenate(
                [w[:, o_u:o_v], w[:, o_v:o_qkv], w[:, o_z:o_z + GDN_HEADS * GDN_DIM], w[:, o_qkv:o_b]],
                axis=1).astype(BF16)
            w_gate = _pad_lanes(w[:, o_b:o_z]).astype(BF16)
            proj, gates = _in_proj(h, g_mix, w_main, w_gate)
            gp = jnp.stack([
                _pad_lanes(jnp.concatenate([jnp.zeros((GDN_HEADS,), F32), gdn_a_log[j]])[None])[0],
                _pad_lanes(jnp.concatenate([jnp.zeros((GDN_HEADS,), F32), gdn_dt_bias[j]])[None])[0]])
            y_a, qkvn, gates2 = _even_prep(proj, gates, gmlp_w_s[j], gmlp_b_s[j].T, gdn_conv_w[j], gp, batch, seq)
            y_b = _gdn(qkvn, gates2, proj, gdn_norm_g[j].reshape(1, GDN_DIM), batch, seq)
            y1, y2, w_out = y_a, y_b, ev_w_out[j]
        else:
            lambda_init = 0.8 - 0.6 * math.exp(-0.3 * layer)
            w = od_w_in[j]
            c = np.cumsum([0, 512, 512, 512, 512, 128, 128, 512, 64, 8])
            dq, dk, dv, sq, sk, sv, iq, ik, iw = [w[:, c[i]:c[i + 1]] for i in range(9)]
            w_main = jnp.concatenate([dq, dk, dv, sq, iq, sk, sv, ik, ik], axis=1).astype(BF16)
            w_gate = _pad_lanes(iw).astype(BF16)
            proj, iw_out = _in_proj(h, g_mix, w_main, w_gate)
            dqn, dkn, sqn, skn, vt = _odd_prep(
                proj,
                jnp.tile(diff_q_norm_g[j], 2 * DIFF_HEADS)[None], jnp.tile(diff_k_norm_g[j], 2 * DIFF_HEADS)[None],
                jnp.tile(dsa_q_norm_g[j], DSA_HEADS)[None], dsa_k_norm_g[j][None], bd64, bd128, batch, seq)
            y_c = _diff_attn(dqn, dkn, vt, bias_tiles, diff_lambda[j], diff_sub_norm_g[j][None],
                             batch, seq, lambda_init)
            y_d = _dsa(sqn, skn, vt, proj, iw_out, bias_tiles, batch, seq)
            y1, y2, w_out = y_c, y_d, od_w_out[j]
        h = _ffn(h, y1, y2, w_out.astype(BF16), ffn_norm_g[layer].reshape(1, d), ffn_w_up[layer].astype(BF16),
                 ffn_conv_w[layer], ffn_conv_b[layer].reshape(1, 2 * D_FF), ffn_w_down[layer].astype(BF16), seq)
    return h.reshape(batch, seq, d)
```
